```python
import math
import jax, jax.numpy as jnp
from jax import lax
import numpy as np

D_MODEL = 4096
BATCH = 4
SEQ = 4096
DEPTH = 2

CTX_LEN = 256
GRID_W = 64
MIX_WIDTH = D_MODEL
ATTN_WIDTH = MIX_WIDTH // 2
SSM_WIDTH = MIX_WIDTH - ATTN_WIDTH
HEAD_DIM = 128
N_HEADS = ATTN_WIDTH // HEAD_DIM
N_KV_HEADS = N_HEADS // 4
GQA_GROUP = N_HEADS // N_KV_HEADS
KV_WIDTH = N_KV_HEADS * HEAD_DIM
IN_WIDTH = ATTN_WIDTH + 2 * KV_WIDTH + SSM_WIDTH
IN_SPLITS = (ATTN_WIDTH, ATTN_WIDTH + KV_WIDTH, ATTN_WIDTH + 2 * KV_WIDTH)
Q_BLOCK = 128
ROPE_AXIS_DIM = HEAD_DIM // 2
ROPE_THETA = 10000.0
SSM_GROUP = 16
SSM_GROUPS = SSM_WIDTH // SSM_GROUP
SSM_STATE = 64
FFN_HIDDEN = 2 * D_MODEL
N_MOD = 9
EPS = 1e-6

kernel_name = "hymba_attn_s5_macaron_dit_prefix"


def rmsnorm(h, gain):
    hf = h.astype(jnp.float32)
    r = lax.rsqrt(jnp.mean(hf * hf, axis=-1, keepdims=True) + EPS)
    return (hf * r).astype(h.dtype) * gain


def ada_chunk(mod, k):
    return mod[:, 3 * k][:, None], mod[:, 3 * k + 1][:, None], mod[:, 3 * k + 2][:, None]


def modulate(h, shift, scale):
    return h * (1.0 + scale) + shift


def swiglu(h, w_in, w_out):
    g, u = jnp.split(h @ w_in, 2, axis=-1)
    return (jax.nn.silu(g) * u) @ w_out


def ffn_sublayer(h, mod, k, gain, w_in, w_out):
    shift, scale, gate = ada_chunk(mod, k)
    return h + 0.5 * gate * swiglu(modulate(rmsnorm(h, gain), shift, scale), w_in, w_out)


def axial_rope_tables(n_tokens):
    rows_n = n_tokens // GRID_W
    grid_r, grid_c = jnp.meshgrid(jnp.arange(rows_n), jnp.arange(GRID_W), indexing="ij")
    rows = grid_r.reshape(-1).astype(jnp.float32)
    cols = grid_c.reshape(-1).astype(jnp.float32)
    inv_freq = ROPE_THETA ** (-jnp.arange(0, ROPE_AXIS_DIM, 2, dtype=jnp.float32) / ROPE_AXIS_DIM)
    ang = jnp.stack([rows[:, None] * inv_freq, cols[:, None] * inv_freq], axis=1)
    return jnp.cos(ang), jnp.sin(ang)


def apply_rope(t, cos, sin):
    b, l, h, _ = t.shape
    tf = t.astype(jnp.float32).reshape(b, l, h, 2, 2, ROPE_AXIS_DIM // 2)
    t1, t2 = tf[..., 0, :], tf[..., 1, :]
    cs, sn = cos[None, :, None], sin[None, :, None]
    out = jnp.stack([t1 * cs - t2 * sn, t2 * cs + t1 * sn], axis=-2)
    return out.reshape(b, l, h, HEAD_DIM).astype(t.dtype)


def heads(t, gain, cos, sin):
    b, l, _ = t.shape
    t = rmsnorm(t.reshape(b, l, -1, HEAD_DIM), gain)
    if cos is not None:
        t = apply_rope(t, cos, sin)
    return t


def query_heads(q, gain, cos, sin):
    b, l, _ = q.shape
    q = heads(q, gain, cos, sin) * (HEAD_DIM ** -0.5)
    return q.reshape(b, l, N_KV_HEADS, GQA_GROUP, HEAD_DIM)


def gqa_attend(q, k, v):
    s = jnp.einsum("bqkgd,bskd->bkgqs", q, k).astype(jnp.float32)
    p = jax.nn.softmax(s, axis=-1).astype(v.dtype)
    return jnp.einsum("bkgqs,bskd->bqkgd", p, v)


def zoh(lam_re, lam_im, log_dt, b_re, b_im):
    f32 = jnp.float32
    lam_re, lam_im = lam_re.astype(f32), lam_im.astype(f32)
    dt = jnp.exp(log_dt.astype(f32))[:, None]
    mag = jnp.exp(lam_re * dt)
    ab_r, ab_i = mag * jnp.cos(lam_im * dt), mag * jnp.sin(lam_im * dt)
    nr, ni = ab_r - 1.0, ab_i
    den = lam_re * lam_re + lam_im * lam_im
    cr = (nr * lam_re + ni * lam_im) / den
    ci = (ni * lam_re - nr * lam_im) / den
    b_re, b_im = b_re.astype(f32), b_im.astype(f32)
    bb_r = cr[..., None] * b_re - ci[..., None] * b_im
    bb_i = cr[..., None] * b_im + ci[..., None] * b_re
    return ab_r, ab_i, bb_r, bb_i


def _ssm_combine(e1, e2):
    a1r, a1i, b1r, b1i = e1
    a2r, a2i, b2r, b2i = e2
    return (a1r * a2r - a1i * a2i,
            a1r * a2i + a1i * a2r,
            a2r * b1r - a2i * b1i + b2r,
            a2r * b1i + a2i * b1r + b2i)


def complex_scan(ab_r, ab_i, bu_r, bu_i, h0=None):
    l = bu_r.shape[1]
    shape = (1, l) + ab_r.shape
    elems = (jnp.broadcast_to(ab_r, shape), jnp.broadcast_to(ab_i, shape), bu_r, bu_i)
    ca_r, ca_i, h_r, h_i = lax.associative_scan(_ssm_combine, elems, axis=1)
    if h0 is not None:
        h0_r, h0_i = h0[0][:, None], h0[1][:, None]
        h_r, h_i = h_r + ca_r * h0_r - ca_i * h0_i, h_i + ca_r * h0_i + ca_i * h0_r
    return h_r, h_i


def ssm_drive(u, bb_r, bb_i):
    return jnp.einsum("blgh,gph->blgp", u, bb_r), jnp.einsum("blgh,gph->blgp", u, bb_i)


def ssm_readout(c_re, c_im, h_r, h_i):
    return (jnp.einsum("gnp,blgp->blgn", c_re.astype(jnp.float32), h_r)
            - jnp.einsum("gnp,blgp->blgn", c_im.astype(jnp.float32), h_i))


def ssm_output(y, u, d_skip, w_glu, b_glu, dtype):
    b, l = y.shape[:2]
    y = y.reshape(b, l, SSM_WIDTH) + d_skip.astype(jnp.float32) * u.reshape(b, l, SSM_WIDTH)
    y = jax.nn.gelu(y)
    return (y * jax.nn.sigmoid(y @ w_glu.astype(jnp.float32) + b_glu.astype(jnp.float32))).astype(dtype)


def s5_mixer(u_x, u_c, lam_re, lam_im, log_dt, b_re, b_im, c_re, c_im, d_skip, w_glu, b_glu, with_ctx_out):
    b, l = u_x.shape[:2]
    lc = u_c.shape[1]
    ux = u_x.astype(jnp.float32).reshape(b, l, SSM_GROUPS, SSM_GROUP)
    uc = u_c.astype(jnp.float32).reshape(b, lc, SSM_GROUPS, SSM_GROUP)
    ys_x, ys_c = [], []
    for d in range(2):
        ab_r, ab_i, bb_r, bb_i = zoh(lam_re[d], lam_im[d], log_dt[d], b_re[d], b_im[d])
        ux_d = ux if d == 0 else jnp.flip(ux, axis=1)
        uc_d = uc if d == 0 else jnp.flip(uc, axis=1)
        hc_r, hc_i = complex_scan(ab_r, ab_i, *ssm_drive(uc_d, bb_r, bb_i))
        hx_r, hx_i = complex_scan(ab_r, ab_i, *ssm_drive(ux_d, bb_r, bb_i),
                                  h0=(hc_r[:, -1], hc_i[:, -1]))
        yx = ssm_readout(c_re[d], c_im[d], hx_r, hx_i)
        ys_x.append(yx if d == 0 else jnp.flip(yx, axis=1))
        if with_ctx_out:
            yc = ssm_readout(c_re[d], c_im[d], hc_r, hc_i)
            ys_c.append(yc if d == 0 else jnp.flip(yc, axis=1))
    out_x = ssm_output(ys_x[0] + ys_x[1], ux, d_skip, w_glu, b_glu, u_x.dtype)
    out_c = ssm_output(ys_c[0] + ys_c[1], uc, d_skip, w_glu, b_glu, u_c.dtype) if with_ctx_out else None
    return out_x, out_c


def setup_inputs(seed: int = 0) -> dict:
    key = jax.random.key(seed)
    ks = jax.random.split(key, 32)
    f32 = jnp.float32

    def nrm(k, shape, scale):
        return jax.random.normal(k, shape, f32) * scale

    def gain(k, shape):
        return 1.0 + 0.02 * jax.random.normal(k, shape, f32)

    L, D, F = DEPTH, D_MODEL, FFN_HIDDEN
    G, P, H = SSM_GROUPS, SSM_STATE, SSM_GROUP
    lam_im = math.pi * jnp.arange(P, dtype=f32)
    return {
        "x": nrm(ks[0], (BATCH, SEQ, D), 1.0),
        "c": nrm(ks[1], (BATCH, D), 1.0),
        "ctx": nrm(ks[2], (BATCH, CTX_LEN, D), 1.0),
        "c_ctx": nrm(ks[3], (D,), 1.0),
        "w_ada": nrm(ks[4], (L, D, N_MOD * D), 0.5 * D ** -0.5),
        "b_ada": nrm(ks[5], (L, N_MOD * D), 0.01),
        "norm_ffn1": gain(ks[6], (L, D)),
        "ffn1_w_in": nrm(ks[7], (L, D, 2 * F), D ** -0.5),
        "ffn1_w_out": nrm(ks[8], (L, F, D), F ** -0.5),
        "norm_mix": gain(ks[9], (L, D)),
        "w_mix_in": nrm(ks[10], (L, D, IN_WIDTH), D ** -0.5),
        "q_norm": gain(ks[11], (L, HEAD_DIM)),
        "k_norm": gain(ks[12], (L, HEAD_DIM)),
        "ssm_lam_re": -0.5 + nrm(ks[13], (L, 2, G, P), 0.01),
        "ssm_lam_im": lam_im + nrm(ks[14], (L, 2, G, P), 0.01),
        "ssm_log_dt": jax.random.uniform(ks[15], (L, 2, G), f32, math.log(1e-3), math.log(1e-1)),
        "ssm_b_re": nrm(ks[16], (L, 2, G, P, H), (2 * H) ** -0.5),
        "ssm_b_im": nrm(ks[17], (L, 2, G, P, H), (2 * H) ** -0.5),
        "ssm_c_re": nrm(ks[18], (L, 2, G, H, P), (2 * P) ** -0.5),
        "ssm_c_im": nrm(ks[19], (L, 2, G, H, P), (2 * P) ** -0.5),
        "ssm_d": nrm(ks[20], (L, SSM_WIDTH), 1.0),
        "w_glu": nrm(ks[21], (L, SSM_WIDTH, SSM_WIDTH), SSM_WIDTH ** -0.5),
        "b_glu": nrm(ks[22], (L, SSM_WIDTH), 0.01),
        "w_mix_out": nrm(ks[23], (L, MIX_WIDTH, D), MIX_WIDTH ** -0.5),
        "norm_ffn2": gain(ks[24], (L, D)),
        "ffn2_w_in": nrm(ks[25], (L, D, 2 * F), D ** -0.5),
        "ffn2_w_out": nrm(ks[26], (L, F, D), F ** -0.5),
        "norm_final": gain(ks[27], (D,)),
    }


def reference(x, c, ctx, c_ctx, w_ada, b_ada, norm_ffn1, ffn1_w_in, ffn1_w_out, norm_mix, w_mix_in,
              q_norm, k_norm, ssm_lam_re, ssm_lam_im, ssm_log_dt, ssm_b_re, ssm_b_im, ssm_c_re, ssm_c_im,
              ssm_d, w_glu, b_glu, w_mix_out, norm_ffn2, ffn2_w_in, ffn2_w_out, norm_final):
    bsz, n_lat = x.shape[0], x.shape[1]
    n_ctx = ctx.shape[1]
    n_blocks = n_lat // Q_BLOCK
    cos, sin = axial_rope_tables(n_lat)
    silu_c = jax.nn.silu(c)
    silu_cc = jax.nn.silu(c_ctx)[None]

    for i in range(DEPTH):
        last = i == DEPTH - 1
        mod_x = (silu_c @ w_ada[i] + b_ada[i]).reshape(bsz, N_MOD, D_MODEL)
        mod_c = (silu_cc @ w_ada[i] + b_ada[i]).reshape(1, N_MOD, D_MODEL)

        x = ffn_sublayer(x, mod_x, 0, norm_ffn1[i], ffn1_w_in[i], ffn1_w_out[i])
        ctx = ffn_sublayer(ctx, mod_c, 0, norm_ffn1[i], ffn1_w_in[i], ffn1_w_out[i])

        sh_x, sc_x, g_x = ada_chunk(mod_x, 1)
        sh_c, sc_c, g_c = ada_chunk(mod_c, 1)
        hx = modulate(rmsnorm(x, norm_mix[i]), sh_x, sc_x)
        hc = modulate(rmsnorm(ctx, norm_mix[i]), sh_c, sc_c)
        q_x, k_x, v_x, u_x = jnp.split(hx @ w_mix_in[i], IN_SPLITS, axis=-1)
        if last:
            k_c, v_c, u_c = jnp.split(hc @ w_mix_in[i][:, ATTN_WIDTH:], [KV_WIDTH, 2 * KV_WIDTH], axis=-1)
        else:
            q_c, k_c, v_c, u_c = jnp.split(hc @ w_mix_in[i], IN_SPLITS, axis=-1)

        qh_x = query_heads(q_x, q_norm[i], cos, sin)
        kh_x = heads(k_x, k_norm[i], cos, sin)
        kh_c = heads(k_c, k_norm[i], None, None)
        vh_x = v_x.reshape(bsz, n_lat, N_KV_HEADS, HEAD_DIM)
        vh_c = v_c.reshape(bsz, n_ctx, N_KV_HEADS, HEAD_DIM)
        k_all = jnp.concatenate([kh_x, kh_c], axis=1)
        v_all = jnp.concatenate([vh_x, vh_c], axis=1)
        qb = qh_x.reshape(bsz, n_blocks, Q_BLOCK, N_KV_HEADS, GQA_GROUP, HEAD_DIM).transpose(1, 0, 2, 3, 4, 5)
        ob = lax.map(lambda qi: gqa_attend(qi, k_all, v_all), qb)
        attn_x = ob.transpose(1, 0, 2, 3, 4, 5).reshape(bsz, n_lat, ATTN_WIDTH)

        ssm_x, ssm_c = s5_mixer(u_x, u_c, ssm_lam_re[i], ssm_lam_im[i], ssm_log_dt[i], ssm_b_re[i],
                                ssm_b_im[i], ssm_c_re[i], ssm_c_im[i], ssm_d[i], w_glu[i], b_glu[i],
                                with_ctx_out=not last)
        x = x + g_x * (jnp.concatenate([attn_x, ssm_x], axis=-1) @ w_mix_out[i])

        if not last:
            qh_c = query_heads(q_c, q_norm[i], None, None)
            attn_c = gqa_attend(qh_c, kh_c, vh_c).reshape(bsz, n_ctx, ATTN_WIDTH)
            ctx = ctx + g_c * (jnp.concatenate([attn_c, ssm_c], axis=-1) @ w_mix_out[i])
            ctx = ffn_sublayer(ctx, mod_c, 2, norm_ffn2[i], ffn2_w_in[i], ffn2_w_out[i])

        x = ffn_sublayer(x, mod_x, 2, norm_ffn2[i], ffn2_w_in[i], ffn2_w_out[i])

    return rmsnorm(x, norm_final)
```

```python
import functools
import math

import jax
import jax.numpy as jnp
from jax import lax
from jax.experimental import pallas as pl
from jax.experimental.pallas import tpu as pltpu

F32 = jnp.float32
BF16 = jnp.bfloat16

HEAD_DIM = 128
GQA_GROUP = 4
GRID_W = 64
ROPE_THETA = 10000.0
SSM_GROUP = 16
SSM_STATE = 64
SSM_CHUNK = 16
LANES = 128
GROUPS_PER_LANE_BLOCK = LANES // SSM_GROUP
EPS = 1e-6
VMEM_LIMIT_BYTES = 56 * 1024 * 1024


def _params(*sem):
    return pltpu.CompilerParams(dimension_semantics=sem, vmem_limit_bytes=VMEM_LIMIT_BYTES)


def _tile(n, pref, mult=LANES):
    if n <= pref:
        return n
    t = (pref // mult) * mult
    while t > mult and n % t:
        t -= mult
    assert n % t == 0, (n, pref, mult)
    return t


def _mod_kernel(c_ref, w_ref, b_ref, o_ref):
    c = c_ref[...]
    s = (c * jax.nn.sigmoid(c)).astype(BF16)
    o_ref[...] = jnp.dot(s, w_ref[...].astype(BF16), preferred_element_type=F32) + b_ref[...]


def _mod_table(cc, w_ada, b_ada3, layer):
    rows, d = cc.shape
    n = w_ada.shape[2]
    tn = _tile(n, 512)
    return pl.pallas_call(
        _mod_kernel,
        grid=(n // tn,),
        in_specs=[
            pl.BlockSpec((rows, d), lambda j: (0, 0)),
            pl.BlockSpec((None, d, tn), lambda j: (layer, 0, j)),
            pl.BlockSpec((None, 1, tn), lambda j: (layer, 0, j)),
        ],
        out_specs=pl.BlockSpec((rows, tn), lambda j: (0, j)),
        out_shape=jax.ShapeDtypeStruct((rows, n), F32),
        compiler_params=_params("parallel"),
        name="mod_table",
    )(cc, w_ada, b_ada3)


def _norm_kernel(x_ref, g_ref, *rest, modulate):
    o_ref = rest[-1]
    x = x_ref[...]
    r = lax.rsqrt(jnp.mean(x * x, axis=-1, keepdims=True) + EPS)
    h = (x * r) * g_ref[...]
    if modulate:
        sh_ref, sc_ref = rest[0], rest[1]
        h = h * (1.0 + sc_ref[...]) + sh_ref[...]
    o_ref[...] = h.astype(o_ref.dtype)


def _norm(x, gain3, layer, *, rows, tr, out_dtype, mod4=None, k=None, seq=None, n_batch=None):
    d = x.shape[1]
    modulate = mod4 is not None
    in_specs = [
        pl.BlockSpec((tr, d), lambda i: (i, 0)),
        pl.BlockSpec((None, 1, d), lambda i: (layer, 0, 0)),
    ]
    args = [x, gain3]
    if modulate:
        def mrow(i):
            return jnp.minimum((i * tr) // seq, n_batch)
        in_specs += [
            pl.BlockSpec((None, None, 1, d), lambda i: (mrow(i), 3 * k, 0, 0)),
            pl.BlockSpec((None, None, 1, d), lambda i: (mrow(i), 3 * k + 1, 0, 0)),
        ]
        args += [mod4, mod4]
    return pl.pallas_call(
        functools.partial(_norm_kernel, modulate=modulate),
        grid=(rows // tr,),
        in_specs=in_specs,
        out_specs=pl.BlockSpec((tr, d), lambda i: (i, 0)),
        out_shape=jax.ShapeDtypeStruct((rows, d), out_dtype),
        compiler_params=_params("parallel"),
        name="rmsnorm_mod" if modulate else "rmsnorm",
    )(*args)


def _accumulate(part, acc_ref, nk, finish):
    if nk == 1:
        finish(part)
        return
    kk = pl.program_id(2)

    @pl.when(kk == 0)
    def _():
        acc_ref[...] = part

    if nk > 2:
        @pl.when(jnp.logical_and(kk > 0, kk < nk - 1))
        def _():
            acc_ref[...] += part

    @pl.when(kk == nk - 1)
    def _():
        finish(acc_ref[...] + part)


def _mm_plain_kernel(x_ref, w_ref, o_ref, *scratch, nk):
    part = jnp.dot(x_ref[...], w_ref[...], preferred_element_type=F32)

    def finish(total):
        o_ref[...] = total.astype(o_ref.dtype)

    _accumulate(part, scratch[0] if scratch else None, nk, finish)


def _mm_plain(x, w3, layer, *, rows, tm, tn, tk, out_dtype):
    kdim, n = w3.shape[1], w3.shape[2]
    nk = kdim // tk
    return pl.pallas_call(
        functools.partial(_mm_plain_kernel, nk=nk),
        grid=(rows // tm, n // tn, nk),
        in_specs=[
            pl.BlockSpec((tm, tk), lambda i, j, k: (i, k)),
            pl.BlockSpec((None, tk, tn), lambda i, j, k: (layer, k, j)),
        ],
        out_specs=pl.BlockSpec((tm, tn), lambda i, j, k: (i, j)),
        out_shape=jax.ShapeDtypeStruct((rows, n), out_dtype),
        scratch_shapes=[pltpu.VMEM((tm, tn), F32)] if nk > 1 else [],
        compiler_params=_params("parallel", "parallel", "arbitrary"),
        name="mm_plain",
    )(x, w3)


def _mm_swiglu_kernel(x_ref, wg_ref, wu_ref, o_ref):
    x = x_ref[...]
    g = jnp.dot(x, wg_ref[...], preferred_element_type=F32)
    u = jnp.dot(x, wu_ref[...], preferred_element_type=F32)
    o_ref[...] = (g * jax.nn.sigmoid(g) * u).astype(o_ref.dtype)


def _mm_swiglu(x, w3, layer, *, rows, tm, tn):
    kdim, f = w3.shape[1], w3.shape[2] // 2
    nj = f // tn
    return pl.pallas_call(
        _mm_swiglu_kernel,
        grid=(rows // tm, nj),
        in_specs=[
            pl.BlockSpec((tm, kdim), lambda i, j: (i, 0)),
            pl.BlockSpec((None, kdim, tn), lambda i, j: (layer, 0, j)),
            pl.BlockSpec((None, kdim, tn), lambda i, j: (layer, 0, j + nj)),
        ],
        out_specs=pl.BlockSpec((tm, tn), lambda i, j: (i, j)),
        out_shape=jax.ShapeDtypeStruct((rows, f), BF16),
        compiler_params=_params("parallel", "parallel"),
        name="mm_swiglu",
    )(x, w3, w3)


def _mm_resid_kernel(x_ref, w_ref, res_ref, gate_ref, o_ref, *scratch, nk, coef):
    part = jnp.dot(x_ref[...], w_ref[...], preferred_element_type=F32)

    def finish(total):
        o_ref[...] = res_ref[...] + (coef * gate_ref[...]) * total

    _accumulate(part, scratch[0] if scratch else None, nk, finish)


def _mm_resid(x, w3, layer, res, mod4, k, coef, *, rows, tm, tn, tk, seq, n_batch):
    kdim, n = w3.shape[1], w3.shape[2]
    nk = kdim // tk

    def mrow(i):
        return jnp.minimum((i * tm) // seq, n_batch)

    return pl.pallas_call(
        functools.partial(_mm_resid_kernel, nk=nk, coef=coef),
        grid=(rows // tm, n // tn, nk),
        in_specs=[
            pl.BlockSpec((tm, tk), lambda i, j, kk: (i, kk)),
            pl.BlockSpec((None, tk, tn), lambda i, j, kk: (layer, kk, j)),
            pl.BlockSpec((tm, tn), lambda i, j, kk: (i, j)),
            pl.BlockSpec((None, None, 1, tn), lambda i, j, kk: (mrow(i), 3 * k + 2, 0, j)),
        ],
        out_specs=pl.BlockSpec((tm, tn), lambda i, j, kk: (i, j)),
        out_shape=jax.ShapeDtypeStruct((rows, n), F32),
        scratch_shapes=[pltpu.VMEM((tm, tn), F32)] if nk > 1 else [],
        compiler_params=_params("parallel", "parallel", "arbitrary"),
        name="mm_resid",
    )(x, w3, res, mod4)


def _mm_glu_kernel(y_ref, w_ref, b_ref, yj_ref, o_ref):
    z = jnp.dot(y_ref[...], w_ref[...], preferred_element_type=F32) + b_ref[...]
    o_ref[...] = (yj_ref[...].astype(F32) * jax.nn.sigmoid(z)).astype(o_ref.dtype)


def _mm_glu(y, w3, b3, layer, *, rows, tm, tn):
    kdim, n = w3.shape[1], w3.shape[2]
    return pl.pallas_call(
        _mm_glu_kernel,
        grid=(rows // tm, n // tn),
        in_specs=[
            pl.BlockSpec((tm, kdim), lambda i, j: (i, 0)),
            pl.BlockSpec((None, kdim, tn), lambda i, j: (layer, 0, j)),
            pl.BlockSpec((None, 1, tn), lambda i, j: (layer, 0, j)),
            pl.BlockSpec((tm, tn), lambda i, j: (i, j)),
        ],
        out_specs=pl.BlockSpec((tm, tn), lambda i, j: (i, j)),
        out_shape=jax.ShapeDtypeStruct((rows, n), BF16),
        compiler_params=_params("parallel", "parallel"),
        name="mm_glu",
    )(y, w3, b3, y)


def _bmm_kernel(x_ref, w_ref, *rest, has_add):
    o_ref = rest[-1]
    acc = jnp.dot(x_ref[...], w_ref[...], preferred_element_type=F32)
    if has_add:
        acc = acc + rest[0][...]
    o_ref[...] = acc


def _bmm(x, w, add=None, *, tn):
    nj, m, kdim = x.shape
    n = w.shape[2]
    in_specs = [
        pl.BlockSpec((None, m, kdim), lambda j, c: (j, 0, 0)),
        pl.BlockSpec((None, kdim, tn), lambda j, c: (j, 0, c)),
    ]
    args = [x, w]
    if add is not None:
        in_specs.append(pl.BlockSpec((None, m, tn), lambda j, c: (j, 0, c)))
        args.append(add)
    return pl.pallas_call(
        functools.partial(_bmm_kernel, has_add=add is not None),
        grid=(nj, n // tn),
        in_specs=in_specs,
        out_specs=pl.BlockSpec((None, m, tn), lambda j, c: (j, 0, c)),
        out_shape=jax.ShapeDtypeStruct((nj, m, n), F32),
        compiler_params=_params("parallel", "parallel"),
        name="ssm_bmm_add" if add is not None else "ssm_bmm",
    )(*args)


def _qkv_prep_kernel(p_ref, cos_ref, sin_ref, qg_ref, kg_ref, o_ref, *, n_q, n_kv):
    cosf = cos_ref[...]
    sins = sin_ref[...]
    lane = lax.broadcasted_iota(jnp.int32, cosf.shape, 1)
    first_half = (lane % (HEAD_DIM // 2)) < (HEAD_DIM // 4)
    q_scale = HEAD_DIM ** -0.5
    for h in range(n_q + n_kv):
        sl = slice(h * HEAD_DIM, (h + 1) * HEAD_DIM)
        t = p_ref[:, sl]
        r = lax.rsqrt(jnp.mean(t * t, axis=-1, keepdims=True) + EPS)
        t = (t * r) * (qg_ref[...] if h < n_q else kg_ref[...])
        partner = jnp.where(first_half,
                            pltpu.roll(t, HEAD_DIM - HEAD_DIM // 4, axis=1),
                            pltpu.roll(t, HEAD_DIM // 4, axis=1))
        t = t * cosf + partner * sins
        if h < n_q:
            t = t * q_scale
        o_ref[:, sl] = t.astype(o_ref.dtype)
    v0 = (n_q + n_kv) * HEAD_DIM
    v1 = v0 + n_kv * HEAD_DIM
    o_ref[:, v0:v1] = p_ref[:, v0:v1].astype(o_ref.dtype)


def _qkv_prep(proj, cosf, sins, qg3, kg3, layer, *, n_q, n_kv):
    rows = proj.shape[0]
    width = (n_q + 2 * n_kv) * HEAD_DIM
    tr = 128
    return pl.pallas_call(
        functools.partial(_qkv_prep_kernel, n_q=n_q, n_kv=n_kv),
        grid=(rows // tr,),
        in_specs=[
            pl.BlockSpec((tr, width), lambda i: (i, 0)),
            pl.BlockSpec((tr, HEAD_DIM), lambda i: (i, 0)),
            pl.BlockSpec((tr, HEAD_DIM), lambda i: (i, 0)),
            pl.BlockSpec((None, 1, HEAD_DIM), lambda i: (layer, 0, 0)),
            pl.BlockSpec((None, 1, HEAD_DIM), lambda i: (layer, 0, 0)),
        ],
        out_specs=pl.BlockSpec((tr, width), lambda i: (i, 0)),
        out_shape=jax.ShapeDtypeStruct((rows, width), BF16),
        compiler_params=_params("parallel"),
        name="qkv_prep",
    )(proj, cosf, sins, qg3, kg3)


def _flash_kernel(q_ref, *rest, tq, kv_lens, tkv):
    o_ref = rest[-1]
    n_src = len(kv_lens)
    k_refs, v_refs = rest[:n_src], rest[n_src:2 * n_src]
    q = jnp.concatenate([q_ref[:, h * HEAD_DIM:(h + 1) * HEAD_DIM] for h in range(GQA_GROUP)], axis=0)
    nrow = GQA_GROUP * tq

    def step(k, v, carry):
        m, l, acc = carry
        s = lax.dot_general(q, k, (((1,), (1,)), ((), ())), preferred_element_type=F32)
        m_new = jnp.maximum(m, jnp.max(s, axis=-1, keepdims=True))
        alpha = jnp.exp(m - m_new)
        p = jnp.exp(s - m_new)
        l = alpha * l + jnp.sum(p, axis=-1, keepdims=True)
        acc = alpha * acc + jnp.dot(p.astype(BF16), v, preferred_element_type=F32)
        return m_new, l, acc

    carry = (jnp.full((nrow, 1), -jnp.inf, F32), jnp.zeros((nrow, 1), F32), jnp.zeros((nrow, HEAD_DIM), F32))
    for k_ref, v_ref, length in zip(k_refs, v_refs, kv_lens):
        t = min(tkv, length)
        n_chunks = length // t
        if n_chunks == 1:
            carry = step(k_ref[...], v_ref[...], carry)
        else:
            def body(c, carry, k_ref=k_ref, v_ref=v_ref, t=t):
                off = pl.multiple_of(c * t, t)
                return step(k_ref[pl.ds(off, t), :], v_ref[pl.ds(off, t), :], carry)
            carry = lax.fori_loop(0, n_chunks, body, carry)
    _, l, acc = carry
    out = acc / l
    for h in range(GQA_GROUP):
        o_ref[:, h * HEAD_DIM:(h + 1) * HEAD_DIM] = out[h * tq:(h + 1) * tq].astype(o_ref.dtype)


def _flash(qkv, *, n_q, n_kv, n_batch, q_row0, q_len, kv_srcs, tq):
    qw = GQA_GROUP * HEAD_DIM
    nq_t = q_len // tq
    assert q_row0 % tq == 0
    in_specs = [pl.BlockSpec((tq, qw), lambda b, g, i: (q_row0 // tq + b * nq_t + i, g))]
    args = [qkv]
    for col0 in (n_q, n_q + n_kv):
        for (row0, length) in kv_srcs:
            assert row0 % length == 0
            in_specs.append(pl.BlockSpec(
                (length, HEAD_DIM),
                lambda b, g, i, row0=row0, length=length, col0=col0: (row0 // length + b, col0 + g)))
            args.append(qkv)
    return pl.pallas_call(
        functools.partial(_flash_kernel, tq=tq, kv_lens=tuple(l for _, l in kv_srcs), tkv=512),
        grid=(n_batch, n_kv, nq_t),
        in_specs=in_specs,
        out_specs=pl.BlockSpec((tq, qw), lambda b, g, i: (b * nq_t + i, g)),
        out_shape=jax.ShapeDtypeStruct((n_batch * q_len, n_q * HEAD_DIM), BF16),
        compiler_params=_params("parallel", "parallel", "arbitrary"),
        name="gqa_attention",
    )(*args)


def _ssm_taps_kernel(z_ref, c_ref, o_ref):
    for g in range(z_ref.shape[0]):
        o_ref[g] = lax.dot_general(z_ref[g], c_ref[g], (((1,), (1,)), ((), ())),
                                   precision=lax.Precision.HIGHEST, preferred_element_type=F32)


def _ssm_taps(z, c):
    ng, th, p2 = z.shape
    h = c.shape[1]
    gb = 8
    return pl.pallas_call(
        _ssm_taps_kernel,
        grid=(ng // gb,),
        in_specs=[
            pl.BlockSpec((gb, th, p2), lambda i: (i, 0, 0)),
            pl.BlockSpec((gb, h, p2), lambda i: (i, 0, 0)),
        ],
        out_specs=pl.BlockSpec((gb, th, h), lambda i: (i, 0, 0)),
        out_shape=jax.ShapeDtypeStruct((ng, th, h), F32),
        compiler_params=_params("parallel"),
        name="ssm_taps",
    )(z, c)


def _ssm_scan_kernel(s_ref, a_ref, o_ref, *, n_ctx_tiles, n_tiles, half):
    row = lax.broadcasted_iota(jnp.int32, (8, half), 0)
    top = row < 4

    def run(col0, a_re, a_im, tiles_in_order, second_first):
        first = jnp.logical_not(top) if second_first else top

        def dup(t):
            return jnp.where(first, t, pltpu.roll(t, 4, axis=0))

        def tile_step(ti, carry):
            hr, hi = carry
            r0 = pl.multiple_of(ti * 8, 8)
            sr = s_ref[pl.ds(r0, 8), col0:col0 + half]
            si = s_ref[pl.ds(r0, 8), col0 + half:col0 + 2 * half]
            t1r = a_re * hr - a_im * hi + sr
            t1i = a_re * hi + a_im * hr + si
            h1r, h1i = dup(t1r), dup(t1i)
            o_ref[pl.ds(r0, 8), col0:col0 + half] = jnp.where(first, hr, h1r)
            o_ref[pl.ds(r0, 8), col0 + half:col0 + 2 * half] = jnp.where(first, hi, h1i)
            t2r = a_re * h1r - a_im * h1i + sr
            t2i = a_re * h1i + a_im * h1r + si
            second = jnp.logical_not(first)
            nr = jnp.where(second, t2r, pltpu.roll(t2r, 4, axis=0))
            ni = jnp.where(second, t2i, pltpu.roll(t2i, 4, axis=0))
            return nr, ni

        carry = (jnp.zeros((8, half), F32), jnp.zeros((8, half), F32))
        for (start, count, reverse) in tiles_in_order:
            if reverse:
                carry = lax.fori_loop(0, count, lambda i, c, s=start, n=count: tile_step(s + n - 1 - i, c), carry)
            else:
                carry = lax.fori_loop(0, count, lambda i, c, s=start: tile_step(s + i, c), carry)

    n_lat = n_tiles - n_ctx_tiles
    a = a_ref[...]
    run(0, a[0:1], a[1:2], [(0, n_ctx_tiles, False), (n_ctx_tiles, n_lat, False)], False)
    run(2 * half, a[2:3], a[3:4], [(0, n_ctx_tiles, True), (n_ctx_tiles, n_lat, True)], True)


def _ssm_scan(s, a_pow, *, n_ctx_tiles):
    nj, m, w = s.shape
    half = w // 4
    return pl.pallas_call(
        functools.partial(_ssm_scan_kernel, n_ctx_tiles=n_ctx_tiles, n_tiles=m // 8, half=half),
        grid=(nj,),
        in_specs=[
            pl.BlockSpec((None, m, w), lambda j: (j, 0, 0)),
            pl.BlockSpec((None, 8, half), lambda j: (j, 0, 0)),
        ],
        out_specs=pl.BlockSpec((None, m, w), lambda j: (j, 0, 0)),
        out_shape=jax.ShapeDtypeStruct((nj, m, w), F32),
        compiler_params=_params("parallel"),
        name="ssm_scan",
    )(s, a_pow)


def _ssm_gelu_kernel(y_ref, u_ref, d_ref, o_ref):
    y = y_ref[...] + d_ref[...] * u_ref[...]
    o_ref[...] = jax.nn.gelu(y).astype(o_ref.dtype)


def _ssm_gelu(y, proj, d3, layer, *, rows, tr, u_col0):
    w = y.shape[1]
    tc = math.gcd(math.gcd(u_col0, w), 1024)
    return pl.pallas_call(
        _ssm_gelu_kernel,
        grid=(rows // tr, w // tc),
        in_specs=[
            pl.BlockSpec((tr, tc), lambda i, j: (i, j)),
            pl.BlockSpec((tr, tc), lambda i, j: (i, u_col0 // tc + j)),
            pl.BlockSpec((None, 1, tc), lambda i, j: (layer, 0, j)),
        ],
        out_specs=pl.BlockSpec((tr, tc), lambda i, j: (i, j)),
        out_shape=jax.ShapeDtypeStruct((rows, w), BF16),
        compiler_params=_params("parallel", "parallel"),
        name="ssm_gelu",
    )(y, proj, d3)


def _ssm_weights(lam_re, lam_im, log_dt, b_re, b_im, c_re, c_im):
    T, H, P = SSM_CHUNK, SSM_GROUP, SSM_STATE
    G = lam_re.shape[1]
    GL = GROUPS_PER_LANE_BLOCK
    J = G // GL
    lam_re, lam_im = lam_re.astype(F32), lam_im.astype(F32)
    dt = jnp.exp(log_dt.astype(F32))[..., None]
    mag = jnp.exp(lam_re * dt)
    ab_r, ab_i = mag * jnp.cos(lam_im * dt), mag * jnp.sin(lam_im * dt)
    nr, ni = ab_r - 1.0, ab_i
    den = lam_re * lam_re + lam_im * lam_im
    cr = (nr * lam_re + ni * lam_im) / den
    ci = (ni * lam_re - nr * lam_im) / den
    b_re, b_im = b_re.astype(F32), b_im.astype(F32)
    bb_r = cr[..., None] * b_re - ci[..., None] * b_im
    bb_i = cr[..., None] * b_im + ci[..., None] * b_re
    c_re, c_im = c_re.astype(F32), c_im.astype(F32)

    def powers(d, exps):
        e = jnp.asarray(exps, F32)[:, None, None]
        pmag = jnp.exp(e * (lam_re[d] * dt[d])[None])
        pang = e * (lam_im[d] * dt[d])[None]
        return pmag * jnp.cos(pang), pmag * jnp.sin(pang)

    def drive(d, exps):
        pr, pi = powers(d, exps)
        return (pr[..., None] * bb_r[d][None] - pi[..., None] * bb_i[d][None],
                pr[..., None] * bb_i[d][None] + pi[..., None] * bb_r[d][None])

    asc = list(range(T))
    desc = [T - 1 - s for s in range(T)]

    def tap_lhs(d, exps):
        zr, zi = drive(d, exps)
        return jnp.concatenate([zr, zi], axis=2).transpose(1, 0, 3, 2)
    zc = jnp.stack([tap_lhs(0, asc), tap_lhs(1, desc)], axis=0).reshape(2 * G, T * H, 2 * P)
    cc = jnp.concatenate([c_re, -c_im], axis=-1).reshape(2 * G, H, 2 * P)
    taps = _ssm_taps(zc, cc).reshape(2, G, T, H, H)

    lag_f = jnp.concatenate([jnp.zeros((G, T - 1, H, H), F32), taps[0]], axis=1)
    lag_b = jnp.concatenate([taps[1], jnp.zeros((G, T - 1, H, H), F32)], axis=1)
    lag = lag_f + lag_b
    toep = jnp.stack([lag[:, T - 1 - s:2 * T - 1 - s] for s in range(T)], axis=1)
    toep = toep.transpose(0, 1, 3, 2, 4).reshape(J, GL, T, H, T, H)
    eye = jnp.eye(GL, dtype=F32)
    w_intra = jnp.einsum("jgshtn,gk->jsghtkn", toep, eye).reshape(J, T * LANES, T * LANES)

    def summary(zr, zi):
        def one(zz):
            zz = zz.transpose(1, 0, 3, 2).reshape(J, GL, T, H, P)
            return jnp.einsum("jgshp,gk->jsghkp", zz, eye).reshape(J, T * LANES, GL * P)
        return jnp.concatenate([one(zr), one(zi)], axis=-1)
    w_sf = summary(*drive(0, desc))
    w_sb = summary(*drive(1, asc))
    w_cat = jnp.concatenate([w_intra, w_sf, w_sb], axis=-1).astype(BF16)

    def readout(pr, pi, d):
        vr = c_re[d][None] * pr[:, :, None, :] - c_im[d][None] * pi[:, :, None, :]
        vi = c_re[d][None] * pi[:, :, None, :] + c_im[d][None] * pr[:, :, None, :]
        def one(vv):
            vv = vv.transpose(1, 3, 0, 2).reshape(J, GL, P, T, H)
            return jnp.einsum("jgptn,gk->jgptkn", vv, eye).reshape(J, GL * P, T * LANES)
        return jnp.concatenate([one(vr), one(-vi)], axis=1)
    v_f = readout(*powers(0, [t + 1 for t in range(T)]), 0)
    v_b = readout(*powers(1, [T - t for t in range(T)]), 1)
    v_cat = jnp.concatenate([v_f, v_b], axis=1).astype(BF16)

    def lanes(x):
        return x.reshape(J, GL * P)
    a_pow = jnp.stack([lanes(a) for d in range(2) for a in powers(d, [T])], axis=1)
    a_pow = jnp.concatenate([a_pow, jnp.zeros_like(a_pow)], axis=1)
    return w_cat, v_cat, a_pow


def _s5_mixer(proj, u_col0, n_batch, seq, n_ctx, weights):
    w_cat, v_cat, a_pow = weights
    T = SSM_CHUNK
    J = w_cat.shape[0]
    n_lat_rows = n_batch * seq
    u = proj[:, u_col0:].astype(BF16)

    def to_chunks(rows, length):
        r = rows.reshape(n_batch, length // T, T, J, LANES).transpose(3, 1, 0, 2, 4)
        return r.reshape(J, (length // T) * n_batch, T * LANES)

    lhs = jnp.concatenate([to_chunks(u[n_lat_rows:], n_ctx), to_chunks(u[:n_lat_rows], seq)], axis=1)
    tl = T * LANES
    first = _bmm(lhs, w_cat, tn=_tile(w_cat.shape[2], 1024))
    h_in = _ssm_scan(first[:, :, tl:], a_pow, n_ctx_tiles=(n_ctx // T) * n_batch // 8)
    y = _bmm(h_in.astype(BF16), v_cat, add=first[:, :, :tl], tn=_tile(tl, 1024))

    def from_chunks(part, length):
        r = part.reshape(J, length // T, n_batch, T, LANES).transpose(2, 1, 3, 0, 4)
        return r.reshape(n_batch * length, J * LANES)

    n_ctx_chunk_rows = (n_ctx // T) * n_batch
    return jnp.concatenate([from_chunks(y[:, n_ctx_chunk_rows:], seq), from_chunks(y[:, :n_ctx_chunk_rows], n_ctx)],
                           axis=0)


def _rope_tables(seq, n_batch, n_ctx_rows):
    rows_n = seq // GRID_W
    pos = jnp.arange(seq)
    rows = (pos // GRID_W).astype(F32)
    cols = (pos % GRID_W).astype(F32)
    del rows_n
    axis_dim = HEAD_DIM // 2
    inv_freq = ROPE_THETA ** (-jnp.arange(0, axis_dim, 2, dtype=F32) / axis_dim)
    ang_r, ang_c = rows[:, None] * inv_freq, cols[:, None] * inv_freq
    cosf = jnp.concatenate([jnp.cos(ang_r)] * 2 + [jnp.cos(ang_c)] * 2, axis=1)
    sins = jnp.concatenate([-jnp.sin(ang_r), jnp.sin(ang_r), -jnp.sin(ang_c), jnp.sin(ang_c)], axis=1)
    cosf = jnp.concatenate([jnp.tile(cosf, (n_batch, 1)), jnp.ones((n_ctx_rows, HEAD_DIM), F32)], axis=0)
    sins = jnp.concatenate([jnp.tile(sins, (n_batch, 1)), jnp.zeros((n_ctx_rows, HEAD_DIM), F32)], axis=0)
    return cosf, sins


def kernel(x, c, ctx, c_ctx, w_ada, b_ada, norm_ffn1, ffn1_w_in, ffn1_w_out, norm_mix, w_mix_in, q_norm, k_norm, ssm_lam_re, ssm_lam_im, ssm_log_dt, ssm_b_re, ssm_b_im, ssm_c_re, ssm_c_im, ssm_d, w_glu, b_glu, w_mix_out, norm_ffn2, ffn2_w_in, ffn2_w_out, norm_final):
    n_batch, seq, d = x.shape
    n_ctx = ctx.shape[1]
    depth = w_ada.shape[0]
    ssm_width = w_glu.shape[1]
    attn_width = w_mix_out.shape[1] - ssm_width
    n_q = attn_width // HEAD_DIM
    n_kv = n_q // GQA_GROUP
    u_col0 = attn_width + 2 * n_kv * HEAD_DIM
    n_lat = n_batch * seq
    n_ctx_rows = n_batch * n_ctx
    n_all = n_lat + n_ctx_rows
    tm = n_ctx_rows
    tr = _tile(tm, 256, 8)
    assert seq % tm == 0 and n_batch == 4

    h = jnp.concatenate([x.reshape(n_lat, d), ctx.reshape(n_ctx_rows, d)], axis=0)
    cc = jnp.concatenate([c, c_ctx[None], jnp.zeros((8 - n_batch - 1, d), F32)], axis=0)
    cosf, sins = _rope_tables(seq, n_batch, n_ctx_rows)

    def row3(a):
        return a.reshape(a.shape[0], 1, a.shape[1])

    b_ada3 = row3(b_ada)
    g_ffn1, g_mix, g_ffn2 = row3(norm_ffn1), row3(norm_mix), row3(norm_ffn2)
    qg3, kg3, d3, bglu3 = row3(q_norm), row3(k_norm), row3(ssm_d), row3(b_glu)
    ffn1_in, ffn1_out = ffn1_w_in.astype(BF16), ffn1_w_out.astype(BF16)
    ffn2_in, ffn2_out = ffn2_w_in.astype(BF16), ffn2_w_out.astype(BF16)
    mix_in, mix_out, glu = w_mix_in.astype(BF16), w_mix_out.astype(BF16), w_glu.astype(BF16)

    def ffn(h, mod4, k, gain3, w_in, w_out, layer, rows):
        hn = _norm(h, gain3, layer, rows=rows, tr=tr, out_dtype=BF16, mod4=mod4, k=k, seq=seq, n_batch=n_batch)
        hid =_mm_swiglu(hn, w_in, layer, rows=rows, tm=tm, tn=_tile(w_in.shape[2] // 2, 512))
        return _mm_resid(hid, w_out, layer, h, mod4, k, 0.5, rows=rows, tm=tm, tn=_tile(d, 1024),
                         tk=_tile(w_out.shape[1], 2048), seq=seq, n_batch=n_batch)

    for layer in range(depth):
        last = layer == depth - 1
        mod4 = _mod_table(cc, w_ada, b_ada3, layer).reshape(8, 9, 1, d)

        h = ffn(h, mod4, 0, g_ffn1, ffn1_in, ffn1_out, layer, n_all)

        hn = _norm(h, g_mix, layer, rows=n_all, tr=tr, out_dtype=BF16, mod4=mod4, k=1, seq=seq, n_batch=n_batch)
        proj = _mm_plain(hn, mix_in, layer, rows=n_all, tm=tm, tn=_tile(mix_in.shape[2], 512), tk=d, out_dtype=F32)
        qkv = _qkv_prep(proj, cosf, sins, qg3, kg3, layer, n_q=n_q, n_kv=n_kv)
        attn = _flash(qkv, n_q=n_q, n_kv=n_kv, n_batch=n_batch, q_row0=0, q_len=seq,
                      kv_srcs=[(0, seq), (n_lat, n_ctx)], tq=min(128, seq))
        out_rows = n_lat if last else n_all
        if not last:
            attn_c = _flash(qkv, n_q=n_q, n_kv=n_kv, n_batch=n_batch, q_row0=n_lat, q_len=n_ctx,
                            kv_srcs=[(n_lat, n_ctx)], tq=min(128, n_ctx))
            attn = jnp.concatenate([attn, attn_c], axis=0)

        weights = _ssm_weights(ssm_lam_re[layer], ssm_lam_im[layer], ssm_log_dt[layer], ssm_b_re[layer],
                               ssm_b_im[layer], ssm_c_re[layer], ssm_c_im[layer])
        y = _s5_mixer(proj, u_col0, n_batch, seq, n_ctx, weights)
        yg = _ssm_gelu(y, proj, d3, layer, rows=out_rows, tr=tr, u_col0=u_col0)
        ssm_out = _mm_glu(yg, glu, bglu3, layer, rows=out_rows, tm=tm, tn=_tile(ssm_width, 512))

        mixed = jnp.concatenate([attn, ssm_out], axis=1)
        h = _mm_resid(mixed, mix_out, layer, h, mod4, 1, 1.0, rows=out_rows, tm=tm, tn=_tile(d, 512),
                      tk=mix_out.shape[1], seq=seq, n_batch=n_batch)
        h = ffn(h, mod4, 2, g_ffn2, ffn2_in, ffn2_out, layer, out_rows)

    out = _norm(h, row3(norm_final[None]), 0, rows=n_lat, tr=tr, out_dtype=F32)
    return out.reshape(n_batch, seq, d)
```

```python
import functools
import math

import jax
import jax.numpy as jnp
from jax import lax
from jax.experimental import pallas as pl
from jax.experimental.pallas import tpu as pltpu

F32 = jnp.float32
BF16 = jnp.bfloat16

HEAD_DIM = 128
GQA_GROUP = 4
GRID_W = 64
ROPE_THETA = 10000.0
SSM_GROUP = 16
SSM_STATE = 64
SSM_CHUNK = 16
LANES = 128
GROUPS_PER_LANE_BLOCK = LANES // SSM_GROUP
EPS = 1e-6
VMEM_LIMIT_BYTES = 56 * 1024 * 1024


def _params(*sem):
    return pltpu.CompilerParams(dimension_semantics=sem, vmem_limit_bytes=VMEM_LIMIT_BYTES)


def _tile(n, pref, mult=LANES):
    if n <= pref:
        return n
    t = (pref // mult) * mult
    while t > mult and n % t:
        t -= mult
    assert n % t == 0, (n, pref, mult)
    return t


def _mod_kernel(c_ref, w_ref, b_ref, o_ref):
    c = c_ref[...]
    s = (c * jax.nn.sigmoid(c)).astype(BF16)
    o_ref[...] = jnp.dot(s, w_ref[...].astype(BF16), preferred_element_type=F32) + b_ref[...]


def _mod_table(cc, w_ada, b_ada3, layer):
    rows, d = cc.shape
    n = w_ada.shape[2]
    tn = _tile(n, 512)
    return pl.pallas_call(
        _mod_kernel,
        grid=(n // tn,),
        in_specs=[
            pl.BlockSpec((rows, d), lambda j: (0, 0)),
            pl.BlockSpec((None, d, tn), lambda j: (layer, 0, j)),
            pl.BlockSpec((None, 1, tn), lambda j: (layer, 0, j)),
        ],
        out_specs=pl.BlockSpec((rows, tn), lambda j: (0, j)),
        out_shape=jax.ShapeDtypeStruct((rows, n), F32),
        compiler_params=_params("parallel"),
        name="mod_table",
    )(cc, w_ada, b_ada3)


def _norm_kernel(x_ref, g_ref, *rest, modulate):
    o_ref = rest[-1]
    x = x_ref[...]
    r = lax.rsqrt(jnp.mean(x * x, axis=-1, keepdims=True) + EPS)
    h = (x * r) * g_ref[...]
    if modulate:
        sh_ref, sc_ref = rest[0], rest[1]
        h = h * (1.0 + sc_ref[...]) + sh_ref[...]
    o_ref[...] = h.astype(o_ref.dtype)


def _norm(x, gain3, layer, *, rows, tr, out_dtype, mod4=None, k=None, seq=None, n_batch=None):
    d = x.shape[1]
    modulate = mod4 is not None
    in_specs = [
        pl.BlockSpec((tr, d), lambda i: (i, 0)),
        pl.BlockSpec((None, 1, d), lambda i: (layer, 0, 0)),
    ]
    args = [x, gain3]
    if modulate:
        def mrow(i):
            return jnp.minimum((i * tr) // seq, n_batch)
        in_specs += [
            pl.BlockSpec((None, None, 1, d), lambda i: (mrow(i), 3 * k, 0, 0)),
            pl.BlockSpec((None, None, 1, d), lambda i: (mrow(i), 3 * k + 1, 0, 0)),
        ]
        args += [mod4, mod4]
    return pl.pallas_call(
        functools.partial(_norm_kernel, modulate=modulate),
        grid=(rows // tr,),
        in_specs=in_specs,
        out_specs=pl.BlockSpec((tr, d), lambda i: (i, 0)),
        out_shape=jax.ShapeDtypeStruct((rows, d), out_dtype),
        compiler_params=_params("parallel"),
        name="rmsnorm_mod" if modulate else "rmsnorm",
    )(*args)


def _accumulate(part, acc_ref, nk, finish):
    if nk == 1:
        finish(part)
        return
    kk = pl.program_id(2)

    @pl.when(kk == 0)
    def _():
        acc_ref[...] = part

    if nk > 2:
        @pl.when(jnp.logical_and(kk > 0, kk < nk - 1))
        def _():
            acc_ref[...] += part

    @pl.when(kk == nk - 1)
    def _():
        finish(acc_ref[...] + part)


def _mm_plain_kernel(x_ref, w_ref, o_ref, *scratch, nk):
    part = jnp.dot(x_ref[...], w_ref[...], preferred_element_type=F32)

    def finish(total):
        o_ref[...] = total.astype(o_ref.dtype)

    _accumulate(part, scratch[0] if scratch else None, nk, finish)


def _mm_plain(x, w3, layer, *, rows, tm, tn, tk, out_dtype):
    kdim, n = w3.shape[1], w3.shape[2]
    nk = kdim // tk
    return pl.pallas_call(
        functools.partial(_mm_plain_kernel, nk=nk),
        grid=(rows // tm, n // tn, nk),
        in_specs=[
            pl.BlockSpec((tm, tk), lambda i, j, k: (i, k)),
            pl.BlockSpec((None, tk, tn), lambda i, j, k: (layer, k, j)),
        ],
        out_specs=pl.BlockSpec((tm, tn), lambda i, j, k: (i, j)),
        out_shape=jax.ShapeDtypeStruct((rows, n), out_dtype),
        scratch_shapes=[pltpu.VMEM((tm, tn), F32)] if nk > 1 else [],
        compiler_params=_params("parallel", "parallel", "arbitrary"),
        name="mm_plain",
    )(x, w3)


def _mm_swiglu_kernel(x_ref, wg_ref, wu_ref, o_ref):
    x = x_ref[...]
    g = jnp.dot(x, wg_ref[...], preferred_element_type=F32)
    u = jnp.dot(x, wu_ref[...], preferred_element_type=F32)
    o_ref[...] = (g * jax.nn.sigmoid(g) * u).astype(o_ref.dtype)


def _mm_swiglu(x, w3, layer, *, rows, tm, tn):
    kdim, f = w3.shape[1], w3.shape[2] // 2
    nj = f // tn
    return pl.pallas_call(
        _mm_swiglu_kernel,
        grid=(rows // tm, nj),
        in_specs=[
            pl.BlockSpec((tm, kdim), lambda i, j: (i, 0)),
            pl.BlockSpec((None, kdim, tn), lambda i, j: (layer, 0, j)),
            pl.BlockSpec((None, kdim, tn), lambda i, j: (layer, 0, j + nj)),
        ],
        out_specs=pl.BlockSpec((tm, tn), lambda i, j: (i, j)),
        out_shape=jax.ShapeDtypeStruct((rows, f), BF16),
        compiler_params=_params("parallel", "parallel"),
        name="mm_swiglu",
    )(x, w3, w3)


def _mm_resid_kernel(x_ref, w_ref, res_ref, gate_ref, o_ref, *, nk, coef):
    def part():
        return (coef * gate_ref[...]) * jnp.dot(x_ref[...], w_ref[...], preferred_element_type=F32)

    if nk == 1:
        o_ref[...] = res_ref[...] + part()
    else:
        @pl.when(pl.program_id(2) == 0)
        def _():
            o_ref[...] = res_ref[...] + part()

        @pl.when(pl.program_id(2) != 0)
        def _():
            o_ref[...] += part()


def _mm_resid(x, w3, layer, res, mod4, k, coef, *, rows, tm, tn, tk, seq, n_batch):
    kdim, n = w3.shape[1], w3.shape[2]
    nk = kdim // tk

    def mrow(i):
        return jnp.minimum((i * tm) // seq, n_batch)

    return pl.pallas_call(
        functools.partial(_mm_resid_kernel, nk=nk, coef=coef),
        grid=(rows // tm, n // tn, nk),
        in_specs=[
            pl.BlockSpec((tm, tk), lambda i, j, kk: (i, kk)),
            pl.BlockSpec((None, tk, tn), lambda i, j, kk: (layer, kk, j)),
            pl.BlockSpec((tm, tn), lambda i, j, kk: (i, j)),
            pl.BlockSpec((None, None, 1, tn), lambda i, j, kk: (mrow(i), 3 * k + 2, 0, j)),
        ],
        out_specs=pl.BlockSpec((tm, tn), lambda i, j, kk: (i, j)),
        out_shape=jax.ShapeDtypeStruct((rows, n), F32),
        compiler_params=_params("parallel", "parallel", "arbitrary"),
        name="mm_resid",
    )(x, w3, res, mod4)


def _mm_glu_kernel(y_ref, w_ref, b_ref, yj_ref, o_ref):
    z = jnp.dot(y_ref[...], w_ref[...], preferred_element_type=F32) + b_ref[...]
    o_ref[...] = (yj_ref[...].astype(F32) * jax.nn.sigmoid(z)).astype(o_ref.dtype)


def _mm_glu(y, w3, b3, layer, *, rows, tm, tn):
    kdim, n = w3.shape[1], w3.shape[2]
    return pl.pallas_call(
        _mm_glu_kernel,
        grid=(rows // tm, n // tn),
        in_specs=[
            pl.BlockSpec((tm, kdim), lambda i, j: (i, 0)),
            pl.BlockSpec((None, kdim, tn), lambda i, j: (layer, 0, j)),
            pl.BlockSpec((None, 1, tn), lambda i, j: (layer, 0, j)),
            pl.BlockSpec((tm, tn), lambda i, j: (i, j)),
        ],
        out_specs=pl.BlockSpec((tm, tn), lambda i, j: (i, j)),
        out_shape=jax.ShapeDtypeStruct((rows, n), BF16),
        compiler_params=_params("parallel", "parallel"),
        name="mm_glu",
    )(y, w3, b3, y)


def _mm_mix_out_kernel(a_ref, ac_ref, s_ref, w_ref, res_ref, gate_ref, o_ref, *, n_lat_blocks, ka):
    a = jnp.where(pl.program_id(0) < n_lat_blocks, a_ref[...], ac_ref[...])
    total = jnp.dot(a, w_ref[:ka, :], preferred_element_type=F32)
    total += jnp.dot(s_ref[...], w_ref[ka:, :], preferred_element_type=F32)
    o_ref[...] = res_ref[...] + gate_ref[...] * total


def _mm_mix_out(attn, attn_c, ssm, w3, layer, res, mod4, k, *, rows, tm, tn, seq, n_batch):
    ka, ks = attn.shape[1], ssm.shape[1]
    n = w3.shape[2]
    n_lat_blocks = attn.shape[0] // tm
    if attn_c is None:
        attn_c = attn
    n_ctx_blocks = attn_c.shape[0] // tm

    def mrow(i):
        return jnp.minimum((i * tm) // seq, n_batch)

    return pl.pallas_call(
        functools.partial(_mm_mix_out_kernel, n_lat_blocks=n_lat_blocks, ka=ka),
        grid=(rows // tm, n // tn),
        in_specs=[
            pl.BlockSpec((tm, ka), lambda i, j: (jnp.minimum(i, n_lat_blocks - 1), 0)),
            pl.BlockSpec((tm, ka), lambda i, j: (jnp.clip(i - n_lat_blocks, 0, n_ctx_blocks - 1), 0)),
            pl.BlockSpec((tm, ks), lambda i, j: (i, 0)),
            pl.BlockSpec((None, ka + ks, tn), lambda i, j: (layer, 0, j)),
            pl.BlockSpec((tm, tn), lambda i, j: (i, j)),
            pl.BlockSpec((None, None, 1, tn), lambda i, j: (mrow(i), 3 * k + 2, 0, j)),
        ],
        out_specs=pl.BlockSpec((tm, tn), lambda i, j: (i, j)),
        out_shape=jax.ShapeDtypeStruct((rows, n), F32),
        compiler_params=_params("parallel", "parallel"),
        name="mm_mix_out",
    )(attn, attn_c, ssm, w3, res, mod4)


def _bmm_kernel(x_ref, w_ref, *rest, has_add):
    o_ref = rest[-1]
    acc = jnp.dot(x_ref[...].astype(BF16), w_ref[...], preferred_element_type=F32)
    if has_add:
        acc = acc + rest[0][...]
    o_ref[...] = acc


def _bmm(x, w, add=None, *, tn):
    nj, m, kdim = x.shape
    n = w.shape[2]
    in_specs = [
        pl.BlockSpec((None, m, kdim), lambda j, c: (j, 0, 0)),
        pl.BlockSpec((None, kdim, tn), lambda j, c: (j, 0, c)),
    ]
    args = [x, w]
    if add is not None:
        in_specs.append(pl.BlockSpec((None, m, tn), lambda j, c: (j, 0, c)))
        args.append(add)
    return pl.pallas_call(
        functools.partial(_bmm_kernel, has_add=add is not None),
        grid=(nj, n // tn),
        in_specs=in_specs,
        out_specs=pl.BlockSpec((None, m, tn), lambda j, c: (j, 0, c)),
        out_shape=jax.ShapeDtypeStruct((nj, m, n), F32),
        compiler_params=_params("parallel", "parallel"),
        name="ssm_bmm_add" if add is not None else "ssm_bmm",
    )(*args)


def _ssm_intra_kernel(x_ref, t_ref, o_ref, w_ref, *, n_t):
    T = SSM_CHUNK
    for half in range(T // n_t):
        @pl.when(pl.program_id(1) == half)
        def _(half=half):
            for s in range(T):
                for tt in range(n_t):
                    lag = half * n_t + tt - s + T - 1
                    w_ref[s * LANES:(s + 1) * LANES, tt * LANES:(tt + 1) * LANES] = t_ref[lag]
    o_ref[...] = jnp.dot(x_ref[...], w_ref[...], preferred_element_type=F32)


def _ssm_intra(x, tiles):
    nj, m, kdim = x.shape
    n_t = SSM_CHUNK // 2
    tn = n_t * LANES
    return pl.pallas_call(
        functools.partial(_ssm_intra_kernel, n_t=n_t),
        grid=(nj, kdim // tn),
        in_specs=[
            pl.BlockSpec((None, m, kdim), lambda j, c: (j, 0, 0)),
            pl.BlockSpec((None, 2 * SSM_CHUNK - 1, LANES, LANES), lambda j, c: (j, 0, 0, 0)),
        ],
        out_specs=pl.BlockSpec((None, m, tn), lambda j, c: (j, 0, c)),
        out_shape=jax.ShapeDtypeStruct((nj, m, kdim), F32),
        scratch_shapes=[pltpu.VMEM((kdim, tn), BF16)],
        compiler_params=_params("parallel", "arbitrary"),
        name="ssm_intra",
    )(x, tiles)


def _qkv_prep_kernel(p_ref, cos_ref, sin_ref, qg_ref, kg_ref, o_ref, *, n_q, n_kv):
    cosf = cos_ref[...]
    sins = sin_ref[...]
    lane = lax.broadcasted_iota(jnp.int32, cosf.shape, 1)
    first_half = (lane % (HEAD_DIM // 2)) < (HEAD_DIM // 4)
    q_scale = HEAD_DIM ** -0.5 * math.log2(math.e)
    for h in range(n_q + n_kv):
        sl = slice(h * HEAD_DIM, (h + 1) * HEAD_DIM)
        t = p_ref[:, sl]
        r = lax.rsqrt(jnp.mean(t * t, axis=-1, keepdims=True) + EPS)
        t = (t * r) * (qg_ref[...] if h < n_q else kg_ref[...])
        partner = jnp.where(first_half,
                            pltpu.roll(t, HEAD_DIM - HEAD_DIM // 4, axis=1),
                            pltpu.roll(t, HEAD_DIM // 4, axis=1))
        t = t * cosf + partner * sins
        if h < n_q:
            t = t * q_scale
        o_ref[:, sl] = t.astype(o_ref.dtype)
    v0 = (n_q + n_kv) * HEAD_DIM
    v1 = v0 + n_kv * HEAD_DIM
    o_ref[:, v0:v1] = p_ref[:, v0:v1].astype(o_ref.dtype)


def _qkv_prep(proj, cosf, sins, qg3, kg3, layer, *, n_q, n_kv):
    rows = proj.shape[0]
    width = (n_q + 2 * n_kv) * HEAD_DIM
    tr = 128
    return pl.pallas_call(
        functools.partial(_qkv_prep_kernel, n_q=n_q, n_kv=n_kv),
        grid=(rows // tr,),
        in_specs=[
            pl.BlockSpec((tr, width), lambda i: (i, 0)),
            pl.BlockSpec((tr, HEAD_DIM), lambda i: (i, 0)),
            pl.BlockSpec((tr, HEAD_DIM), lambda i: (i, 0)),
            pl.BlockSpec((None, 1, HEAD_DIM), lambda i: (layer, 0, 0)),
            pl.BlockSpec((None, 1, HEAD_DIM), lambda i: (layer, 0, 0)),
        ],
        out_specs=pl.BlockSpec((tr, width), lambda i: (i, 0)),
        out_shape=jax.ShapeDtypeStruct((rows, width), BF16),
        compiler_params=_params("parallel"),
        name="qkv_prep",
    )(proj, cosf, sins, qg3, kg3)


def _flash_kernel(q_ref, *rest, tq, kv_lens, tkv):
    o_ref = rest[-1]
    n_src = len(kv_lens)
    k_refs, v_refs = rest[:n_src], rest[n_src:2 * n_src]
    q = jnp.concatenate([q_ref[:, h * HEAD_DIM:(h + 1) * HEAD_DIM] for h in range(GQA_GROUP)], axis=0)
    nrow = GQA_GROUP * tq

    def step(k, v, carry):
        m, l, acc = carry
        s = lax.dot_general(q, k, (((1,), (1,)), ((), ())), preferred_element_type=F32)
        m_new = jnp.maximum(m, jnp.max(s, axis=-1, keepdims=True))
        alpha = jnp.exp2(m - m_new)
        p = jnp.exp2(s - m_new)
        l = alpha * l + jnp.sum(p, axis=-1, keepdims=True)
        acc = alpha * acc + jnp.dot(p.astype(BF16), v, preferred_element_type=F32)
        return m_new, l, acc

    carry = (jnp.full((nrow, 1), -jnp.inf, F32), jnp.zeros((nrow, 1), F32), jnp.zeros((nrow, HEAD_DIM), F32))
    for k_ref, v_ref, length in zip(k_refs, v_refs, kv_lens):
        t = min(tkv, length)
        for c in range(length // t):
            carry = step(k_ref[c * t:(c + 1) * t, :], v_ref[c * t:(c + 1) * t, :], carry)
    _, l, acc = carry
    out = acc / l
    for h in range(GQA_GROUP):
        o_ref[:, h * HEAD_DIM:(h + 1) * HEAD_DIM] = out[h * tq:(h + 1) * tq].astype(o_ref.dtype)


def _flash(qkv, *, n_q, n_kv, n_batch, q_row0, q_len, kv_srcs, tq):
    qw = GQA_GROUP * HEAD_DIM
    nq_t = q_len // tq
    assert q_row0 % tq == 0
    in_specs = [pl.BlockSpec((tq, qw), lambda b, g, i: (q_row0 // tq + b * nq_t + i, g))]
    args = [qkv]
    for col0 in (n_q, n_q + n_kv):
        for (row0, length) in kv_srcs:
            assert row0 % length == 0
            in_specs.append(pl.BlockSpec(
                (length, HEAD_DIM),
                lambda b, g, i, row0=row0, length=length, col0=col0: (row0 // length + b, col0 + g)))
            args.append(qkv)
    return pl.pallas_call(
        functools.partial(_flash_kernel, tq=tq, kv_lens=tuple(l for _, l in kv_srcs), tkv=512),
        grid=(n_batch, n_kv, nq_t),
        in_specs=in_specs,
        out_specs=pl.BlockSpec((tq, qw), lambda b, g, i: (b * nq_t + i, g)),
        out_shape=jax.ShapeDtypeStruct((n_batch * q_len, n_q * HEAD_DIM), BF16),
        compiler_params=_params("parallel", "parallel", "arbitrary"),
        name="gqa_attention",
    )(*args)


def _ssm_taps_kernel(z_ref, c_ref, o_ref):
    o_ref[...] = lax.dot_general(z_ref[...], c_ref[...], (((1,), (1,)), ((), ())),
                                 precision=lax.Precision.HIGHEST, preferred_element_type=F32)


def _ssm_taps(z, c):
    nj, nd, rows, p2 = z.shape
    return pl.pallas_call(
        _ssm_taps_kernel,
        grid=(nj, nd),
        in_specs=[
            pl.BlockSpec((None, None, rows, p2), lambda j, d: (j, d, 0, 0)),
            pl.BlockSpec((None, None, LANES, p2), lambda j, d: (j, d, 0, 0)),
        ],
        out_specs=pl.BlockSpec((None, None, rows, LANES), lambda j, d: (j, d, 0, 0)),
        out_shape=jax.ShapeDtypeStruct((nj, nd, rows, LANES), F32),
        compiler_params=_params("parallel", "parallel"),
        name="ssm_taps",
    )(z, c)


def _ssm_scan_kernel(s_ref, a_ref, o_ref, *, n_ctx_tiles, n_tiles, half):
    row = lax.broadcasted_iota(jnp.int32, (8, half), 0)
    top = row < 4

    def run(col0, a_re, a_im, tiles_in_order, second_first):
        first = jnp.logical_not(top) if second_first else top

        def dup(t):
            return jnp.where(first, t, pltpu.roll(t, 4, axis=0))

        def tile_step(ti, carry):
            hr, hi = carry
            r0 = pl.multiple_of(ti * 8, 8)
            sr = s_ref[pl.ds(r0, 8), col0:col0 + half]
            si = s_ref[pl.ds(r0, 8), col0 + half:col0 + 2 * half]
            t1r = a_re * hr - a_im * hi + sr
            t1i = a_re * hi + a_im * hr + si
            h1r, h1i = dup(t1r), dup(t1i)
            o_ref[pl.ds(r0, 8), col0:col0 + half] = jnp.where(first, hr, h1r)
            o_ref[pl.ds(r0, 8), col0 + half:col0 + 2 * half] = jnp.where(first, hi, h1i)
            t2r = a_re * h1r - a_im * h1i + sr
            t2i = a_re * h1i + a_im * h1r + si
            second = jnp.logical_not(first)
            nr = jnp.where(second, t2r, pltpu.roll(t2r, 4, axis=0))
            ni = jnp.where(second, t2i, pltpu.roll(t2i, 4, axis=0))
            return nr, ni

        carry = (jnp.zeros((8, half), F32), jnp.zeros((8, half), F32))
        for (start, count, reverse) in tiles_in_order:
            if reverse:
                carry = lax.fori_loop(0, count, lambda i, c, s=start, n=count: tile_step(s + n - 1 - i, c), carry)
            else:
                carry = lax.fori_loop(0, count, lambda i, c, s=start: tile_step(s + i, c), carry)

    n_lat = n_tiles - n_ctx_tiles
    a = a_ref[...]
    run(0, a[0:1], a[1:2], [(0, n_ctx_tiles, False), (n_ctx_tiles, n_lat, False)], False)
    run(2 * half, a[2:3], a[3:4], [(0, n_ctx_tiles, True), (n_ctx_tiles, n_lat, True)], True)


def _ssm_scan(s, a_pow, *, n_ctx_tiles):
    nj, m, w = s.shape
    half = w // 4
    return pl.pallas_call(
        functools.partial(_ssm_scan_kernel, n_ctx_tiles=n_ctx_tiles, n_tiles=m // 8, half=half),
        grid=(nj,),
        in_specs=[
            pl.BlockSpec((None, m, w), lambda j: (j, 0, 0)),
            pl.BlockSpec((None, 8, half), lambda j: (j, 0, 0)),
        ],
        out_specs=pl.BlockSpec((None, m, w), lambda j: (j, 0, 0)),
        out_shape=jax.ShapeDtypeStruct((nj, m, w), F32),
        compiler_params=_params("parallel"),
        name="ssm_scan",
    )(s, a_pow)


def _ssm_gelu_kernel(y_ref, u_ref, d_ref, o_ref):
    y = y_ref[...] + d_ref[...] * u_ref[...]
    o_ref[...] = jax.nn.gelu(y).astype(o_ref.dtype)


def _ssm_gelu(y, proj, d3, layer, *, rows, tr, u_col0):
    w = y.shape[1]
    tc = math.gcd(math.gcd(u_col0, w), 1024)
    return pl.pallas_call(
        _ssm_gelu_kernel,
        grid=(rows // tr, w // tc),
        in_specs=[
            pl.BlockSpec((tr, tc), lambda i, j: (i, j)),
            pl.BlockSpec((tr, tc), lambda i, j: (i, u_col0 // tc + j)),
            pl.BlockSpec((None, 1, tc), lambda i, j: (layer, 0, j)),
        ],
        out_specs=pl.BlockSpec((tr, tc), lambda i, j: (i, j)),
        out_shape=jax.ShapeDtypeStruct((rows, w), BF16),
        compiler_params=_params("parallel", "parallel"),
        name="ssm_gelu",
    )(y, proj, d3)


def _ssm_weights(lam_re, lam_im, log_dt, b_re, b_im, c_re, c_im):
    T, H, P = SSM_CHUNK, SSM_GROUP, SSM_STATE
    G = lam_re.shape[1]
    GL = GROUPS_PER_LANE_BLOCK
    J = G // GL
    lam_re, lam_im = lam_re.astype(F32), lam_im.astype(F32)
    dt = jnp.exp(log_dt.astype(F32))[..., None]
    mag = jnp.exp(lam_re * dt)
    ab_r, ab_i = mag * jnp.cos(lam_im * dt), mag * jnp.sin(lam_im * dt)
    nr, ni = ab_r - 1.0, ab_i
    den = lam_re * lam_re + lam_im * lam_im
    cr = (nr * lam_re + ni * lam_im) / den
    ci = (ni * lam_re - nr * lam_im) / den
    b_re, b_im = b_re.astype(F32), b_im.astype(F32)
    bb_r = cr[..., None] * b_re - ci[..., None] * b_im
    bb_i = cr[..., None] * b_im + ci[..., None] * b_re
    c_re, c_im = c_re.astype(F32), c_im.astype(F32)

    def powers(d, exps):
        e = jnp.asarray(exps, F32)[:, None, None]
        pmag = jnp.exp(e * (lam_re[d] * dt[d])[None])
        pang = e * (lam_im[d] * dt[d])[None]
        return pmag * jnp.cos(pang), pmag * jnp.sin(pang)

    def drive(d, exps):
        pr, pi = powers(d, exps)
        return (pr[..., None] * bb_r[d][None] - pi[..., None] * bb_i[d][None],
                pr[..., None] * bb_i[d][None] + pi[..., None] * bb_r[d][None])

    asc = list(range(T))
    desc = [T - 1 - s for s in range(T)]

    def group_mask(n_rows, n_cols, rows_per_group, cols_per_group):
        r = lax.broadcasted_iota(jnp.int32, (n_rows, n_cols), 0) // rows_per_group
        c = lax.broadcasted_iota(jnp.int32, (n_rows, n_cols), 1) // cols_per_group
        return (r == c).astype(F32)

    def tap_lhs(d, exps):
        zr, zi = drive(d, exps)
        z = jnp.concatenate([zr, zi], axis=2).reshape(T, J, GL, 2 * P, H)
        return z.transpose(1, 0, 2, 4, 3).reshape(J, T * LANES, 2 * P)
    z_all = jnp.stack([tap_lhs(0, asc), tap_lhs(1, desc)], axis=1)
    c_all = jnp.concatenate([c_re, -c_im], axis=-1).reshape(2, J, LANES, 2 * P).transpose(1, 0, 2, 3)
    raw = _ssm_taps(z_all, c_all).reshape(J, 2, T, LANES, LANES)
    fwd, bwd = raw[:, 0], raw[:, 1]
    tiles = jnp.concatenate([bwd[:, :T - 1], bwd[:, T - 1:] + fwd[:, :1], fwd[:, 1:]], axis=1)
    tiles = (tiles * group_mask(LANES, LANES, H, H)).astype(BF16)

    mask_s = group_mask(GL, GL * P, 1, P)[None, None, :, None, :]
    def summary(d, exps):
        pr, pi = powers(d, exps)
        pr, pi = (a.reshape(T, J, 1, GL * P).transpose(1, 0, 2, 3) for a in (pr, pi))
        br, bi = (a.reshape(J, GL, P, H).transpose(0, 3, 1, 2).reshape(J, 1, H, GL * P)
                  for a in (bb_r[d], bb_i[d]))
        zr = (pr * br - pi * bi)[:, :, None] * mask_s
        zi = (pr * bi + pi * br)[:, :, None] * mask_s
        return jnp.concatenate([zr, zi], axis=-1).reshape(J, T * LANES, 2 * GL * P)
    w_sum = jnp.concatenate([summary(0, desc), summary(1, asc)], axis=-1).astype(BF16)

    mask_v = group_mask(GL, LANES, 1, H)[None, :, None, None, :]
    def readout(d, exps):
        pr, pi = powers(d, exps)
        pr, pi = (a.reshape(T, J, GL, P).transpose(1, 2, 3, 0)[..., None] for a in (pr, pi))
        cr_l, ci_l = (a.reshape(J, GL, H, P).transpose(0, 3, 1, 2).reshape(J, 1, P, 1, LANES)
                      for a in (c_re[d], c_im[d]))
        vr = (cr_l * pr - ci_l * pi) * mask_v
        vi = (cr_l * pi + ci_l * pr) * mask_v
        return jnp.concatenate([vr.reshape(J, GL * P, T * LANES), -vi.reshape(J, GL * P, T * LANES)], axis=1)
    v_cat = jnp.concatenate([readout(0, [t + 1 for t in range(T)]), readout(1, [T - t for t in range(T)])],
                            axis=1).astype(BF16)

    def lanes(x):
        return x.reshape(J, GL * P)
    a_pow = jnp.stack([lanes(a) for d in range(2) for a in powers(d, [T])], axis=1)
    a_pow = jnp.concatenate([a_pow, jnp.zeros_like(a_pow)], axis=1)
    return tiles, w_sum, v_cat, a_pow


def _s5_mixer(proj, u_col0, n_batch, seq, n_ctx, weights):
    tiles, w_sum, v_cat, a_pow = weights
    T = SSM_CHUNK
    J = tiles.shape[0]
    n_lat_rows = n_batch * seq
    u = proj[:, u_col0:].astype(BF16)

    def to_chunks(rows, length):
        r = rows.reshape(n_batch, length // T, T, J, LANES).transpose(3, 1, 0, 2, 4)
        return r.reshape(J, (length // T) * n_batch, T * LANES)

    lhs = jnp.concatenate([to_chunks(u[n_lat_rows:], n_ctx), to_chunks(u[:n_lat_rows], seq)], axis=1)
    y_intra = _ssm_intra(lhs, tiles)
    summaries = _bmm(lhs, w_sum, tn=_tile(w_sum.shape[2], 1024))
    h_in = _ssm_scan(summaries, a_pow, n_ctx_tiles=(n_ctx // T) * n_batch // 8)
    y = _bmm(h_in, v_cat, add=y_intra, tn=_tile(T * LANES, 512))

    def from_chunks(part, length):
        r = part.reshape(J, length // T, n_batch, T, LANES).transpose(2, 1, 3, 0, 4)
        return r.reshape(n_batch * length, J * LANES)

    n_ctx_chunk_rows = (n_ctx // T) * n_batch
    return jnp.concatenate([from_chunks(y[:, n_ctx_chunk_rows:], seq), from_chunks(y[:, :n_ctx_chunk_rows], n_ctx)],
                           axis=0)


def _rope_tables(seq, n_batch, n_ctx_rows):
    rows_n = seq // GRID_W
    pos = jnp.arange(seq)
    rows = (pos // GRID_W).astype(F32)
    cols = (pos % GRID_W).astype(F32)
    del rows_n
    axis_dim = HEAD_DIM // 2
    inv_freq = ROPE_THETA ** (-jnp.arange(0, axis_dim, 2, dtype=F32) / axis_dim)
    ang_r, ang_c = rows[:, None] * inv_freq, cols[:, None] * inv_freq
    cosf = jnp.concatenate([jnp.cos(ang_r)] * 2 + [jnp.cos(ang_c)] * 2, axis=1)
    sins = jnp.concatenate([-jnp.sin(ang_r), jnp.sin(ang_r), -jnp.sin(ang_c), jnp.sin(ang_c)], axis=1)
    cosf = jnp.concatenate([jnp.tile(cosf, (n_batch, 1)), jnp.ones((n_ctx_rows, HEAD_DIM), F32)], axis=0)
    sins = jnp.concatenate([jnp.tile(sins, (n_batch, 1)), jnp.zeros((n_ctx_rows, HEAD_DIM), F32)], axis=0)
    return cosf, sins


def kernel(x, c, ctx, c_ctx, w_ada, b_ada, norm_ffn1, ffn1_w_in, ffn1_w_out, norm_mix, w_mix_in, q_norm, k_norm, ssm_lam_re, ssm_lam_im, ssm_log_dt, ssm_b_re, ssm_b_im, ssm_c_re, ssm_c_im, ssm_d, w_glu, b_glu, w_mix_out, norm_ffn2, ffn2_w_in, ffn2_w_out, norm_final):
    n_batch, seq, d = x.shape
    n_ctx = ctx.shape[1]
    depth = w_ada.shape[0]
    ssm_width = w_glu.shape[1]
    attn_width = w_mix_out.shape[1] - ssm_width
    n_q = attn_width // HEAD_DIM
    n_kv = n_q // GQA_GROUP
    u_col0 = attn_width + 2 * n_kv * HEAD_DIM
    n_lat = n_batch * seq
    n_ctx_rows = n_batch * n_ctx
    n_all = n_lat + n_ctx_rows
    tm = n_ctx_rows
    tr = _tile(tm, 256, 8)
    assert seq % tm == 0 and n_batch == 4

    h = jnp.concatenate([x.reshape(n_lat, d), ctx.reshape(n_ctx_rows, d)], axis=0)
    cc = jnp.concatenate([c, c_ctx[None], jnp.zeros((8 - n_batch - 1, d), F32)], axis=0)
    cosf, sins = _rope_tables(seq, n_batch, n_ctx_rows)

    def row3(a):
        return a.reshape(a.shape[0], 1, a.shape[1])

    b_ada3 = row3(b_ada)
    g_ffn1, g_mix, g_ffn2 = row3(norm_ffn1), row3(norm_mix), row3(norm_ffn2)
    qg3, kg3, d3, bglu3 = row3(q_norm), row3(k_norm), row3(ssm_d), row3(b_glu)
    ffn1_in, ffn1_out = ffn1_w_in.astype(BF16), ffn1_w_out.astype(BF16)
    ffn2_in, ffn2_out = ffn2_w_in.astype(BF16), ffn2_w_out.astype(BF16)
    mix_in, mix_out, glu = w_mix_in.astype(BF16), w_mix_out.astype(BF16), w_glu.astype(BF16)

    def ffn(h, mod4, k, gain3, w_in, w_out, layer, rows):
        hn = _norm(h, gain3, layer, rows=rows, tr=tr, out_dtype=BF16, mod4=mod4, k=k, seq=seq, n_batch=n_batch)
        hid =_mm_swiglu(hn, w_in, layer, rows=rows, tm=tm, tn=_tile(w_in.shape[2] // 2, 512))
        return _mm_resid(hid, w_out, layer, h, mod4, k, 0.5, rows=rows, tm=tm, tn=_tile(d, 1024),
                         tk=_tile(w_out.shape[1], 2048), seq=seq, n_batch=n_batch)

    for layer in range(depth):
        last = layer == depth - 1
        mod4 = _mod_table(cc, w_ada, b_ada3, layer).reshape(8, 9, 1, d)

        h = ffn(h, mod4, 0, g_ffn1, ffn1_in, ffn1_out, layer, n_all)

        hn = _norm(h, g_mix, layer, rows=n_all, tr=tr, out_dtype=BF16, mod4=mod4, k=1, seq=seq, n_batch=n_batch)
        proj = _mm_plain(hn, mix_in, layer, rows=n_all, tm=tm, tn=_tile(mix_in.shape[2], 512), tk=d, out_dtype=F32)
        qkv = _qkv_prep(proj, cosf, sins, qg3, kg3, layer, n_q=n_q, n_kv=n_kv)
        attn = _flash(qkv, n_q=n_q, n_kv=n_kv, n_batch=n_batch, q_row0=0, q_len=seq,
                      kv_srcs=[(0, seq), (n_lat, n_ctx)], tq=min(128, seq))
        out_rows = n_lat if last else n_all
        attn_c = None
        if not last:
            attn_c = _flash(qkv, n_q=n_q, n_kv=n_kv, n_batch=n_batch, q_row0=n_lat, q_len=n_ctx,
                            kv_srcs=[(n_lat, n_ctx)], tq=min(128, n_ctx))

        weights = _ssm_weights(ssm_lam_re[layer], ssm_lam_im[layer], ssm_log_dt[layer], ssm_b_re[layer],
                               ssm_b_im[layer], ssm_c_re[layer], ssm_c_im[layer])
        y = _s5_mixer(proj, u_col0, n_batch, seq, n_ctx, weights)
        yg = _ssm_gelu(y, proj, d3, layer, rows=out_rows, tr=tr, u_col0=u_col0)
        ssm_out = _mm_glu(yg, glu, bglu3, layer, rows=out_rows, tm=tm, tn=_tile(ssm_width, 512))

        h = _mm_mix_out(attn, attn_c, ssm_out, mix_out, layer, h, mod4, 1, rows=out_rows, tm=tm,
                        tn=_tile(d, 512), seq=seq, n_batch=n_batch)
        h = ffn(h, mod4, 2, g_ffn2, ffn2_in, ffn2_out, layer, out_rows)

    out = _norm(h, row3(norm_final[None]), 0, rows=n_lat, tr=tr, out_dtype=F32)
    return out.reshape(n_batch, seq, d)
```

```python
import functools
import math

import jax
import jax.numpy as jnp
from jax import lax
from jax.experimental import pallas as pl
from jax.experimental.pallas import tpu as pltpu

F32 = jnp.float32
BF16 = jnp.bfloat16

HEAD_DIM = 128
GQA_GROUP = 4
GRID_W = 64
ROPE_THETA = 10000.0
SSM_GROUP = 16
SSM_STATE = 64
SSM_CHUNK = 16
LANES = 128
GROUPS_PER_LANE_BLOCK = LANES // SSM_GROUP
EPS = 1e-6
VMEM_LIMIT_BYTES = 56 * 1024 * 1024


def _params(*sem):
    return pltpu.CompilerParams(dimension_semantics=sem, vmem_limit_bytes=VMEM_LIMIT_BYTES)


def _tile(n, pref, mult=LANES):
    if n <= pref:
        return n
    t = (pref // mult) * mult
    while t > mult and n % t:
        t -= mult
    assert n % t == 0, (n, pref, mult)
    return t


def _mod_kernel(c_ref, w_ref, b_ref, o_ref):
    c = c_ref[...]
    s = (c * jax.nn.sigmoid(c)).astype(BF16)
    o_ref[...] = jnp.dot(s, w_ref[...].astype(BF16), preferred_element_type=F32) + b_ref[...]


def _mod_table(cc, w_ada, b_ada3, layer):
    rows, d = cc.shape
    n = w_ada.shape[2]
    tn = _tile(n, 512)
    return pl.pallas_call(
        _mod_kernel,
        grid=(n // tn,),
        in_specs=[
            pl.BlockSpec((rows, d), lambda j: (0, 0)),
            pl.BlockSpec((None, d, tn), lambda j: (layer, 0, j)),
            pl.BlockSpec((None, 1, tn), lambda j: (layer, 0, j)),
        ],
        out_specs=pl.BlockSpec((rows, tn), lambda j: (0, j)),
        out_shape=jax.ShapeDtypeStruct((rows, n), F32),
        compiler_params=_params("parallel"),
        name="mod_table",
    )(cc, w_ada, b_ada3)


def _norm_kernel(x_ref, g_ref, *rest, modulate):
    o_ref = rest[-1]
    x = x_ref[...]
    r = lax.rsqrt(jnp.mean(x * x, axis=-1, keepdims=True) + EPS)
    h = (x * r) * g_ref[...]
    if modulate:
        sh_ref, sc_ref = rest[0], rest[1]
        h = h * (1.0 + sc_ref[...]) + sh_ref[...]
    o_ref[...] = h.astype(o_ref.dtype)


def _norm(x, gain3, layer, *, rows, tr, out_dtype, mod4=None, k=None, seq=None, n_batch=None):
    d = x.shape[1]
    modulate = mod4 is not None
    in_specs = [
        pl.BlockSpec((tr, d), lambda i: (i, 0)),
        pl.BlockSpec((None, 1, d), lambda i: (layer, 0, 0)),
    ]
    args = [x, gain3]
    if modulate:
        def mrow(i):
            return jnp.minimum((i * tr) // seq, n_batch)
        in_specs += [
            pl.BlockSpec((None, None, 1, d), lambda i: (mrow(i), 3 * k, 0, 0)),
            pl.BlockSpec((None, None, 1, d), lambda i: (mrow(i), 3 * k + 1, 0, 0)),
        ]
        args += [mod4, mod4]
    return pl.pallas_call(
        functools.partial(_norm_kernel, modulate=modulate),
        grid=(rows // tr,),
        in_specs=in_specs,
        out_specs=pl.BlockSpec((tr, d), lambda i: (i, 0)),
        out_shape=jax.ShapeDtypeStruct((rows, d), out_dtype),
        compiler_params=_params("parallel"),
        name="rmsnorm_mod" if modulate else "rmsnorm",
    )(*args)


def _mm_swiglu_kernel(x_ref, wg_ref, wu_ref, o_ref):
    x = x_ref[...]
    g = jnp.dot(x, wg_ref[...], preferred_element_type=F32)
    u = jnp.dot(x, wu_ref[...], preferred_element_type=F32)
    o_ref[...] = (g * jax.nn.sigmoid(g) * u).astype(o_ref.dtype)


def _mm_swiglu(x, w3, layer, *, rows, tm, tn):
    kdim, f = w3.shape[1], w3.shape[2] // 2
    nj = f // tn
    return pl.pallas_call(
        _mm_swiglu_kernel,
        grid=(rows // tm, nj),
        in_specs=[
            pl.BlockSpec((tm, kdim), lambda i, j: (i, 0)),
            pl.BlockSpec((None, kdim, tn), lambda i, j: (layer, 0, j)),
            pl.BlockSpec((None, kdim, tn), lambda i, j: (layer, 0, j + nj)),
        ],
        out_specs=pl.BlockSpec((tm, tn), lambda i, j: (i, j)),
        out_shape=jax.ShapeDtypeStruct((rows, f), BF16),
        compiler_params=_params("parallel", "parallel"),
        name="mm_swiglu",
    )(x, w3, w3)


def _mm_resid_kernel(x_ref, w_ref, res_ref, gate_ref, o_ref, *, nk, coef):
    def part():
        return (coef * gate_ref[...]) * jnp.dot(x_ref[...], w_ref[...], preferred_element_type=F32)

    if nk == 1:
        o_ref[...] = res_ref[...] + part()
    else:
        @pl.when(pl.program_id(2) == 0)
        def _():
            o_ref[...] = res_ref[...] + part()

        @pl.when(pl.program_id(2) != 0)
        def _():
            o_ref[...] += part()


def _mm_resid(x, w3, layer, res, mod4, k, coef, *, rows, tm, tn, tk, seq, n_batch):
    kdim, n = w3.shape[1], w3.shape[2]
    nk = kdim // tk

    def mrow(i):
        return jnp.minimum((i * tm) // seq, n_batch)

    return pl.pallas_call(
        functools.partial(_mm_resid_kernel, nk=nk, coef=coef),
        grid=(rows // tm, n // tn, nk),
        in_specs=[
            pl.BlockSpec((tm, tk), lambda i, j, kk: (i, kk)),
            pl.BlockSpec((None, tk, tn), lambda i, j, kk: (layer, kk, j)),
            pl.BlockSpec((tm, tn), lambda i, j, kk: (i, j)),
            pl.BlockSpec((None, None, 1, tn), lambda i, j, kk: (mrow(i), 3 * k + 2, 0, j)),
        ],
        out_specs=pl.BlockSpec((tm, tn), lambda i, j, kk: (i, j)),
        out_shape=jax.ShapeDtypeStruct((rows, n), F32),
        compiler_params=_params("parallel", "parallel", "arbitrary"),
        name="mm_resid",
    )(x, w3, res, mod4)


def _mm_glu_kernel(y_ref, w_ref, b_ref, yj_ref, o_ref):
    z = jnp.dot(y_ref[...], w_ref[...], preferred_element_type=F32) + b_ref[...]
    o_ref[...] = (yj_ref[...].astype(F32) * jax.nn.sigmoid(z)).astype(o_ref.dtype)


def _mm_glu(y, w3, b3, layer, *, rows, tm, tn):
    kdim, n = w3.shape[1], w3.shape[2]
    return pl.pallas_call(
        _mm_glu_kernel,
        grid=(rows // tm, n // tn),
        in_specs=[
            pl.BlockSpec((tm, kdim), lambda i, j: (i, 0)),
            pl.BlockSpec((None, kdim, tn), lambda i, j: (layer, 0, j)),
            pl.BlockSpec((None, 1, tn), lambda i, j: (layer, 0, j)),
            pl.BlockSpec((tm, tn), lambda i, j: (i, j)),
        ],
        out_specs=pl.BlockSpec((tm, tn), lambda i, j: (i, j)),
        out_shape=jax.ShapeDtypeStruct((rows, n), BF16),
        compiler_params=_params("parallel", "parallel"),
        name="mm_glu",
    )(y, w3, b3, y)


def _mm_mix_out_kernel(a_ref, ac_ref, s_ref, w_ref, res_ref, gate_ref, o_ref, *, n_lat_blocks, ka):
    a = jnp.where(pl.program_id(0) < n_lat_blocks, a_ref[...], ac_ref[...])
    total = jnp.dot(a, w_ref[:ka, :], preferred_element_type=F32)
    total += jnp.dot(s_ref[...], w_ref[ka:, :], preferred_element_type=F32)
    o_ref[...] = res_ref[...] + gate_ref[...] * total


def _mm_mix_out(attn, attn_c, ssm, w3, layer, res, mod4, k, *, rows, tm, tn, seq, n_batch):
    ka, ks = attn.shape[1], ssm.shape[1]
    n = w3.shape[2]
    n_lat_blocks = attn.shape[0] // tm
    if attn_c is None:
        attn_c = attn
    n_ctx_blocks = attn_c.shape[0] // tm

    def mrow(i):
        return jnp.minimum((i * tm) // seq, n_batch)

    return pl.pallas_call(
        functools.partial(_mm_mix_out_kernel, n_lat_blocks=n_lat_blocks, ka=ka),
        grid=(rows // tm, n // tn),
        in_specs=[
            pl.BlockSpec((tm, ka), lambda i, j: (jnp.minimum(i, n_lat_blocks - 1), 0)),
            pl.BlockSpec((tm, ka), lambda i, j: (jnp.clip(i - n_lat_blocks, 0, n_ctx_blocks - 1), 0)),
            pl.BlockSpec((tm, ks), lambda i, j: (i, 0)),
            pl.BlockSpec((None, ka + ks, tn), lambda i, j: (layer, 0, j)),
            pl.BlockSpec((tm, tn), lambda i, j: (i, j)),
            pl.BlockSpec((None, None, 1, tn), lambda i, j: (mrow(i), 3 * k + 2, 0, j)),
        ],
        out_specs=pl.BlockSpec((tm, tn), lambda i, j: (i, j)),
        out_shape=jax.ShapeDtypeStruct((rows, n), F32),
        compiler_params=_params("parallel", "parallel"),
        name="mm_mix_out",
    )(attn, attn_c, ssm, w3, res, mod4)


def _lane_group(shape, period, width):
    lane = lax.broadcasted_iota(jnp.int32, shape, 1)
    return (lane % period) // width


def _ssm_summary_kernel(x_ref, z_ref, o_ref, w_ref):
    T, H, GL, P = SSM_CHUNK, SSM_GROUP, GROUPS_PER_LANE_BLOCK, SSM_STATE
    grp = _lane_group((H, w_ref.shape[1]), GL * P, P)
    for gl in range(GL):
        keep = grp == gl
        for s in range(T):
            r0 = s * LANES + gl * H
            w_ref[r0:r0 + H, :] = jnp.where(keep, z_ref[s * H:(s + 1) * H, :], 0.0).astype(BF16)
    o_ref[...] = jnp.dot(x_ref[...], w_ref[...], preferred_element_type=F32)


def _ssm_summary(x, z):
    nj, m, kdim = x.shape
    n = z.shape[2]
    tn = n // 2
    return pl.pallas_call(
        _ssm_summary_kernel,
        grid=(nj, n // tn),
        in_specs=[
            pl.BlockSpec((None, m, kdim), lambda j, c: (j, 0, 0)),
            pl.BlockSpec((None, z.shape[1], tn), lambda j, c: (j, 0, c)),
        ],
        out_specs=pl.BlockSpec((None, m, tn), lambda j, c: (j, 0, c)),
        out_shape=jax.ShapeDtypeStruct((nj, m, n), F32),
        scratch_shapes=[pltpu.VMEM((kdim, tn), BF16)],
        compiler_params=_params("parallel", "parallel"),
        name="ssm_summary",
    )(x, z)


def _ssm_readout_kernel(h_ref, v_ref, add_ref, o_ref, w_ref):
    GL, P = GROUPS_PER_LANE_BLOCK, SSM_STATE
    grp = _lane_group((P, w_ref.shape[1]), LANES, SSM_GROUP)
    for gl in range(GL):
        keep = grp == gl
        for x in range(v_ref.shape[0]):
            r0 = (x * GL + gl) * P
            w_ref[r0:r0 + P, :] = jnp.where(keep, v_ref[x], 0.0).astype(BF16)
    o_ref[...] = add_ref[...] + jnp.dot(h_ref[...].astype(BF16), w_ref[...], preferred_element_type=F32)


def _ssm_readout(h, v, add):
    nj, m, kdim = h.shape
    n = v.shape[3]
    tn = n // 4
    return pl.pallas_call(
        _ssm_readout_kernel,
        grid=(nj, n // tn),
        in_specs=[
            pl.BlockSpec((None, m, kdim), lambda j, c: (j, 0, 0)),
            pl.BlockSpec((None, v.shape[1], v.shape[2], tn), lambda j, c: (j, 0, 0, c)),
            pl.BlockSpec((None, m, tn), lambda j, c: (j, 0, c)),
        ],
        out_specs=pl.BlockSpec((None, m, tn), lambda j, c: (j, 0, c)),
        out_shape=jax.ShapeDtypeStruct((nj, m, n), F32),
        scratch_shapes=[pltpu.VMEM((kdim, tn), BF16)],
        compiler_params=_params("parallel", "parallel"),
        name="ssm_readout",
    )(h, v, add)


def _ssm_intra_kernel(x_ref, t_ref, o_ref, w_ref, *, n_t):
    T = SSM_CHUNK
    for half in range(T // n_t):
        @pl.when(pl.program_id(1) == half)
        def _(half=half):
            for s in range(T):
                for tt in range(n_t):
                    lag = half * n_t + tt - s + T - 1
                    w_ref[s * LANES:(s + 1) * LANES, tt * LANES:(tt + 1) * LANES] = t_ref[lag]
    o_ref[...] = jnp.dot(x_ref[...], w_ref[...], preferred_element_type=F32)


def _ssm_intra(x, tiles):
    nj, m, kdim = x.shape
    n_t = SSM_CHUNK // 2
    tn = n_t * LANES
    return pl.pallas_call(
        functools.partial(_ssm_intra_kernel, n_t=n_t),
        grid=(nj, kdim // tn),
        in_specs=[
            pl.BlockSpec((None, m, kdim), lambda j, c: (j, 0, 0)),
            pl.BlockSpec((None, 2 * SSM_CHUNK - 1, LANES, LANES), lambda j, c: (j, 0, 0, 0)),
        ],
        out_specs=pl.BlockSpec((None, m, tn), lambda j, c: (j, 0, c)),
        out_shape=jax.ShapeDtypeStruct((nj, m, kdim), F32),
        scratch_shapes=[pltpu.VMEM((kdim, tn), BF16)],
        compiler_params=_params("parallel", "arbitrary"),
        name="ssm_intra",
    )(x, tiles)


def _mix_in_kernel(x_ref, w_ref, cos_ref, sin_ref, qg_ref, kg_ref, qkv_ref, u_ref, *, nq_tiles, heads):
    j = pl.program_id(1)

    def proj():
        return jnp.dot(x_ref[...], w_ref[...], preferred_element_type=F32)

    @pl.when(j <= nq_tiles)
    def _():
        is_q = j < nq_tiles
        gain = jnp.where(is_q, qg_ref[...], kg_ref[...])
        scale = jnp.where(is_q, HEAD_DIM ** -0.5 * math.log2(math.e), 1.0)
        rows = x_ref.shape[0]
        rc = min(rows, 256)
        for r0 in range(0, rows, rc):
            p = jnp.dot(x_ref[r0:r0 + rc, :], w_ref[...], preferred_element_type=F32)
            cosf = cos_ref[r0:r0 + rc, :]
            sins = sin_ref[r0:r0 + rc, :]
            lane = lax.broadcasted_iota(jnp.int32, cosf.shape, 1)
            first_half = (lane % (HEAD_DIM // 2)) < (HEAD_DIM // 4)
            for h in range(heads):
                sl = slice(h * HEAD_DIM, (h + 1) * HEAD_DIM)
                t = p[:, sl]
                r = lax.rsqrt(jnp.mean(t * t, axis=-1, keepdims=True) + EPS)
                t = (t * r) * gain
                partner = jnp.where(first_half,
                                    pltpu.roll(t, HEAD_DIM - HEAD_DIM // 4, axis=1),
                                    pltpu.roll(t, HEAD_DIM // 4, axis=1))
                t = (t * cosf + partner * sins) * scale
                qkv_ref[r0:r0 + rc, sl] = t.astype(qkv_ref.dtype)

    @pl.when(j == nq_tiles + 1)
    def _():
        qkv_ref[...] = proj().astype(qkv_ref.dtype)

    @pl.when(j > nq_tiles + 1)
    def _():
        u_ref[...] = proj().astype(u_ref.dtype)


def _mix_in(x, w3, layer, cosf, sins, qg3, kg3, *, rows, tm, n_q, n_kv):
    kdim, n = w3.shape[1], w3.shape[2]
    tn = n_kv * HEAD_DIM
    nq_tiles = n_q // n_kv
    n_qkv_tiles = nq_tiles + 2
    n_tiles = n // tn
    return pl.pallas_call(
        functools.partial(_mix_in_kernel, nq_tiles=nq_tiles, heads=n_kv),
        grid=(rows // tm, n_tiles),
        in_specs=[
            pl.BlockSpec((tm, kdim), lambda i, j: (i, 0)),
            pl.BlockSpec((None, kdim, tn), lambda i, j: (layer, 0, j)),
            pl.BlockSpec((tm, HEAD_DIM), lambda i, j: (i, 0)),
            pl.BlockSpec((tm, HEAD_DIM), lambda i, j: (i, 0)),
            pl.BlockSpec((None, 1, HEAD_DIM), lambda i, j: (layer, 0, 0)),
            pl.BlockSpec((None, 1, HEAD_DIM), lambda i, j: (layer, 0, 0)),
        ],
        out_specs=[
            pl.BlockSpec((tm, tn), lambda i, j: (i, jnp.minimum(j, n_qkv_tiles - 1))),
            pl.BlockSpec((tm, tn), lambda i, j: (i, jnp.maximum(j - n_qkv_tiles, 0))),
        ],
        out_shape=[
            jax.ShapeDtypeStruct((rows, n_qkv_tiles * tn), BF16),
            jax.ShapeDtypeStruct((rows, n - n_qkv_tiles * tn), BF16),
        ],
        compiler_params=_params("parallel", "arbitrary"),
        name="mix_in",
    )(x, w3, cosf, sins, qg3, kg3)


def _flash_kernel(q_ref, *rest, tq, kv_lens, tkv):
    o_ref = rest[-1]
    n_src = len(kv_lens)
    k_refs, v_refs = rest[:n_src], rest[n_src:2 * n_src]
    q = jnp.concatenate([q_ref[:, h * HEAD_DIM:(h + 1) * HEAD_DIM] for h in range(GQA_GROUP)], axis=0)
    nrow = GQA_GROUP * tq

    def step(k, v, carry):
        m, l, acc = carry
        s = lax.dot_general(q, k, (((1,), (1,)), ((), ())), preferred_element_type=F32)
        m_new = jnp.maximum(m, jnp.max(s, axis=-1, keepdims=True))
        alpha = jnp.exp2(m - m_new)
        p = jnp.exp2(s - m_new)
        l = alpha * l + jnp.sum(p, axis=-1, keepdims=True)
        acc = alpha * acc + jnp.dot(p.astype(BF16), v, preferred_element_type=F32)
        return m_new, l, acc

    carry = (jnp.full((nrow, 1), -jnp.inf, F32), jnp.zeros((nrow, 1), F32), jnp.zeros((nrow, HEAD_DIM), F32))
    for k_ref, v_ref, length in zip(k_refs, v_refs, kv_lens):
        t = min(tkv, length)
        for c in range(length // t):
            carry = step(k_ref[c * t:(c + 1) * t, :], v_ref[c * t:(c + 1) * t, :], carry)
    _, l, acc = carry
    out = acc / l
    for h in range(GQA_GROUP):
        o_ref[:, h * HEAD_DIM:(h + 1) * HEAD_DIM] = out[h * tq:(h + 1) * tq].astype(o_ref.dtype)


def _flash(qkv, *, n_q, n_kv, n_batch, q_row0, q_len, kv_srcs, tq):
    qw = GQA_GROUP * HEAD_DIM
    nq_t = q_len // tq
    assert q_row0 % tq == 0
    in_specs = [pl.BlockSpec((tq, qw), lambda b, g, i: (q_row0 // tq + b * nq_t + i, g))]
    args = [qkv]
    for col0 in (n_q, n_q + n_kv):
        for (row0, length) in kv_srcs:
            assert row0 % length == 0
            in_specs.append(pl.BlockSpec(
                (length, HEAD_DIM),
                lambda b, g, i, row0=row0, length=length, col0=col0: (row0 // length + b, col0 + g)))
            args.append(qkv)
    return pl.pallas_call(
        functools.partial(_flash_kernel, tq=tq, kv_lens=tuple(l for _, l in kv_srcs), tkv=256),
        grid=(n_batch, n_kv, nq_t),
        in_specs=in_specs,
        out_specs=pl.BlockSpec((tq, qw), lambda b, g, i: (b * nq_t + i, g)),
        out_shape=jax.ShapeDtypeStruct((n_batch * q_len, n_q * HEAD_DIM), BF16),
        compiler_params=_params("parallel", "parallel", "arbitrary"),
        name="gqa_attention",
    )(*args)


def _ssm_taps_kernel(z_ref, c_ref, o_ref):
    o_ref[...] = lax.dot_general(z_ref[...], c_ref[...], (((1,), (1,)), ((), ())),
                                 precision=lax.Precision.HIGHEST, preferred_element_type=F32)


def _ssm_taps(z, c):
    nj, nd, rows, p2 = z.shape
    return pl.pallas_call(
        _ssm_taps_kernel,
        grid=(nj, nd),
        in_specs=[
            pl.BlockSpec((None, None, rows, p2), lambda j, d: (j, d, 0, 0)),
            pl.BlockSpec((None, None, LANES, p2), lambda j, d: (j, d, 0, 0)),
        ],
        out_specs=pl.BlockSpec((None, None, rows, LANES), lambda j, d: (j, d, 0, 0)),
        out_shape=jax.ShapeDtypeStruct((nj, nd, rows, LANES), F32),
        compiler_params=_params("parallel", "parallel"),
        name="ssm_taps",
    )(z, c)


def _ssm_scan_kernel(s_ref, a_ref, o_ref, *, n_ctx_tiles, n_tiles, half):
    row = lax.broadcasted_iota(jnp.int32, (8, half), 0)
    top = row < 4

    def run(col0, a_re, a_im, tiles_in_order, second_first):
        first = jnp.logical_not(top) if second_first else top

        def dup(t):
            return jnp.where(first, t, pltpu.roll(t, 4, axis=0))

        def tile_step(ti, carry):
            hr, hi = carry
            r0 = pl.multiple_of(ti * 8, 8)
            sr = s_ref[pl.ds(r0, 8), col0:col0 + half]
            si = s_ref[pl.ds(r0, 8), col0 + half:col0 + 2 * half]
            t1r = a_re * hr - a_im * hi + sr
            t1i = a_re * hi + a_im * hr + si
            h1r, h1i = dup(t1r), dup(t1i)
            o_ref[pl.ds(r0, 8), col0:col0 + half] = jnp.where(first, hr, h1r)
            o_ref[pl.ds(r0, 8), col0 + half:col0 + 2 * half] = jnp.where(first, hi, h1i)
            t2r = a_re * h1r - a_im * h1i + sr
            t2i = a_re * h1i + a_im * h1r + si
            second = jnp.logical_not(first)
            nr = jnp.where(second, t2r, pltpu.roll(t2r, 4, axis=0))
            ni = jnp.where(second, t2i, pltpu.roll(t2i, 4, axis=0))
            return nr, ni

        carry = (jnp.zeros((8, half), F32), jnp.zeros((8, half), F32))
        for (start, count, reverse) in tiles_in_order:
            if reverse:
                carry = lax.fori_loop(0, count, lambda i, c, s=start, n=count: tile_step(s + n - 1 - i, c), carry)
            else:
                carry = lax.fori_loop(0, count, lambda i, c, s=start: tile_step(s + i, c), carry)

    n_lat = n_tiles - n_ctx_tiles
    a = a_ref[...]
    run(0, a[0:1], a[1:2], [(0, n_ctx_tiles, False), (n_ctx_tiles, n_lat, False)], False)
    run(2 * half, a[2:3], a[3:4], [(0, n_ctx_tiles, True), (n_ctx_tiles, n_lat, True)], True)


def _ssm_scan(s, a_pow, *, n_ctx_tiles):
    nj, m, w = s.shape
    half = w // 4
    return pl.pallas_call(
        functools.partial(_ssm_scan_kernel, n_ctx_tiles=n_ctx_tiles, n_tiles=m // 8, half=half),
        grid=(nj,),
        in_specs=[
            pl.BlockSpec((None, m, w), lambda j: (j, 0, 0)),
            pl.BlockSpec((None, 8, half), lambda j: (j, 0, 0)),
        ],
        out_specs=pl.BlockSpec((None, m, w), lambda j: (j, 0, 0)),
        out_shape=jax.ShapeDtypeStruct((nj, m, w), F32),
        compiler_params=_params("parallel"),
        name="ssm_scan",
    )(s, a_pow)


def _ssm_gelu_kernel(y_ref, u_ref, d_ref, o_ref):
    y = y_ref[...] + d_ref[...] * u_ref[...].astype(F32)
    o_ref[...] = jax.nn.gelu(y).astype(o_ref.dtype)


def _ssm_gelu(y, u, d3, layer, *, rows, tr):
    w = y.shape[1]
    return pl.pallas_call(
        _ssm_gelu_kernel,
        grid=(rows // tr,),
        in_specs=[
            pl.BlockSpec((tr, w), lambda i: (i, 0)),
            pl.BlockSpec((tr, w), lambda i: (i, 0)),
            pl.BlockSpec((None, 1, w), lambda i: (layer, 0, 0)),
        ],
        out_specs=pl.BlockSpec((tr, w), lambda i: (i, 0)),
        out_shape=jax.ShapeDtypeStruct((rows, w), BF16),
        compiler_params=_params("parallel"),
        name="ssm_gelu",
    )(y, u, d3)


def _ssm_weights(lam_re, lam_im, log_dt, b_re, b_im, c_re, c_im):
    T, H, P = SSM_CHUNK, SSM_GROUP, SSM_STATE
    G = lam_re.shape[1]
    GL = GROUPS_PER_LANE_BLOCK
    J = G // GL
    lam_re, lam_im = lam_re.astype(F32), lam_im.astype(F32)
    dt = jnp.exp(log_dt.astype(F32))[..., None]
    mag = jnp.exp(lam_re * dt)
    ab_r, ab_i = mag * jnp.cos(lam_im * dt), mag * jnp.sin(lam_im * dt)
    nr, ni = ab_r - 1.0, ab_i
    den = lam_re * lam_re + lam_im * lam_im
    cr = (nr * lam_re + ni * lam_im) / den
    ci = (ni * lam_re - nr * lam_im) / den
    b_re, b_im = b_re.astype(F32), b_im.astype(F32)
    bb_r = cr[..., None] * b_re - ci[..., None] * b_im
    bb_i = cr[..., None] * b_im + ci[..., None] * b_re
    c_re, c_im = c_re.astype(F32), c_im.astype(F32)

    def powers(d, exps):
        e = jnp.asarray(exps, F32)[:, None, None]
        pmag = jnp.exp(e * (lam_re[d] * dt[d])[None])
        pang = e * (lam_im[d] * dt[d])[None]
        return pmag * jnp.cos(pang), pmag * jnp.sin(pang)

    def drive(d, exps):
        pr, pi = powers(d, exps)
        return (pr[..., None] * bb_r[d][None] - pi[..., None] * bb_i[d][None],
                pr[..., None] * bb_i[d][None] + pi[..., None] * bb_r[d][None])

    asc = list(range(T))
    desc = [T - 1 - s for s in range(T)]

    def group_mask(n_rows, n_cols, rows_per_group, cols_per_group):
        r = lax.broadcasted_iota(jnp.int32, (n_rows, n_cols), 0) // rows_per_group
        c = lax.broadcasted_iota(jnp.int32, (n_rows, n_cols), 1) // cols_per_group
        return (r == c).astype(F32)

    def tap_lhs(d, exps):
        zr, zi = drive(d, exps)
        z = jnp.concatenate([zr, zi], axis=2).reshape(T, J, GL, 2 * P, H)
        return z.transpose(1, 0, 2, 4, 3).reshape(J, T * LANES, 2 * P)
    z_all = jnp.stack([tap_lhs(0, asc), tap_lhs(1, desc)], axis=1)
    c_all = jnp.concatenate([c_re, -c_im], axis=-1).reshape(2, J, LANES, 2 * P).transpose(1, 0, 2, 3)
    raw = _ssm_taps(z_all, c_all).reshape(J, 2, T, LANES, LANES)
    fwd, bwd = raw[:, 0], raw[:, 1]
    tiles = jnp.concatenate([bwd[:, :T - 1], bwd[:, T - 1:] + fwd[:, :1], fwd[:, 1:]], axis=1)
    tiles = (tiles * group_mask(LANES, LANES, H, H)).astype(BF16)

    def summary(d, exps):
        pr, pi = powers(d, exps)
        pr, pi = (a.reshape(T, J, 1, GL * P).transpose(1, 0, 2, 3) for a in (pr, pi))
        br, bi = (a.reshape(J, GL, P, H).transpose(0, 3, 1, 2).reshape(J, 1, H, GL * P)
                  for a in (bb_r[d], bb_i[d]))
        return jnp.concatenate([pr * br - pi * bi, pr * bi + pi * br], axis=-1).reshape(J, T * H, 2 * GL * P)
    z_sum = jnp.concatenate([summary(0, desc), summary(1, asc)], axis=-1)

    def readout(d, exps):
        pr, pi = powers(d, exps)
        pr, pi = (jnp.repeat(a.reshape(T, J, GL, P).transpose(1, 3, 0, 2), H, axis=-1)
                  for a in (pr, pi))
        cr_l, ci_l = (a.reshape(J, GL, H, P).transpose(0, 3, 1, 2).reshape(J, P, 1, LANES)
                      for a in (c_re[d], c_im[d]))
        vr = (cr_l * pr - ci_l * pi).reshape(J, 1, P, T * LANES)
        vi = (cr_l * pi + ci_l * pr).reshape(J, 1, P, T * LANES)
        return jnp.concatenate([vr, -vi], axis=1)
    v_out = jnp.concatenate([readout(0, [t + 1 for t in range(T)]), readout(1, [T - t for t in range(T)])],
                            axis=1)

    def lanes(x):
        return x.reshape(J, GL * P)
    a_pow = jnp.stack([lanes(a) for d in range(2) for a in powers(d, [T])], axis=1)
    a_pow = jnp.concatenate([a_pow, jnp.zeros_like(a_pow)], axis=1)
    return tiles, z_sum, v_out, a_pow


def _s5_mixer(u, n_batch, seq, n_ctx, weights):
    tiles, z_sum, v_out, a_pow = weights
    T = SSM_CHUNK
    J = tiles.shape[0]
    n_lat_rows = n_batch * seq

    def to_chunks(rows, length):
        r = rows.reshape(n_batch, length // T, T, J, LANES).transpose(3, 1, 0, 2, 4)
        return r.reshape(J, (length // T) * n_batch, T * LANES)

    lhs = jnp.concatenate([to_chunks(u[n_lat_rows:], n_ctx), to_chunks(u[:n_lat_rows], seq)], axis=1)
    y_intra = _ssm_intra(lhs, tiles)
    summaries = _ssm_summary(lhs, z_sum)
    h_in = _ssm_scan(summaries, a_pow, n_ctx_tiles=(n_ctx // T) * n_batch // 8)
    y = _ssm_readout(h_in, v_out, y_intra)

    def from_chunks(part, length):
        r = part.reshape(J, length // T, n_batch, T, LANES).transpose(2, 1, 3, 0, 4)
        return r.reshape(n_batch * length, J * LANES)

    n_ctx_chunk_rows = (n_ctx // T) * n_batch
    return jnp.concatenate([from_chunks(y[:, n_ctx_chunk_rows:], seq), from_chunks(y[:, :n_ctx_chunk_rows], n_ctx)],
                           axis=0)


def _rope_tables(seq, n_batch, n_ctx_rows):
    rows_n = seq // GRID_W
    pos = jnp.arange(seq)
    rows = (pos // GRID_W).astype(F32)
    cols = (pos % GRID_W).astype(F32)
    del rows_n
    axis_dim = HEAD_DIM // 2
    inv_freq = ROPE_THETA ** (-jnp.arange(0, axis_dim, 2, dtype=F32) / axis_dim)
    ang_r, ang_c = rows[:, None] * inv_freq, cols[:, None] * inv_freq
    cosf = jnp.concatenate([jnp.cos(ang_r)] * 2 + [jnp.cos(ang_c)] * 2, axis=1)
    sins = jnp.concatenate([-jnp.sin(ang_r), jnp.sin(ang_r), -jnp.sin(ang_c), jnp.sin(ang_c)], axis=1)
    cosf = jnp.concatenate([jnp.tile(cosf, (n_batch, 1)), jnp.ones((n_ctx_rows, HEAD_DIM), F32)], axis=0)
    sins = jnp.concatenate([jnp.tile(sins, (n_batch, 1)), jnp.zeros((n_ctx_rows, HEAD_DIM), F32)], axis=0)
    return cosf, sins


def kernel(x, c, ctx, c_ctx, w_ada, b_ada, norm_ffn1, ffn1_w_in, ffn1_w_out, norm_mix, w_mix_in, q_norm, k_norm, ssm_lam_re, ssm_lam_im, ssm_log_dt, ssm_b_re, ssm_b_im, ssm_c_re, ssm_c_im, ssm_d, w_glu, b_glu, w_mix_out, norm_ffn2, ffn2_w_in, ffn2_w_out, norm_final):
    n_batch, seq, d = x.shape
    n_ctx = ctx.shape[1]
    depth = w_ada.shape[0]
    ssm_width = w_glu.shape[1]
    attn_width = w_mix_out.shape[1] - ssm_width
    n_q = attn_width // HEAD_DIM
    n_kv = n_q // GQA_GROUP
    n_lat = n_batch * seq
    n_ctx_rows = n_batch * n_ctx
    n_all = n_lat + n_ctx_rows
    tm = n_ctx_rows
    tr = _tile(tm, 256, 8)
    assert seq % tm == 0 and n_batch == 4

    h = jnp.concatenate([x.reshape(n_lat, d), ctx.reshape(n_ctx_rows, d)], axis=0)
    cc = jnp.concatenate([c, c_ctx[None], jnp.zeros((8 - n_batch - 1, d), F32)], axis=0)
    cosf, sins = _rope_tables(seq, n_batch, n_ctx_rows)

    def row3(a):
        return a.reshape(a.shape[0], 1, a.shape[1])

    b_ada3 = row3(b_ada)
    g_ffn1, g_mix, g_ffn2 = row3(norm_ffn1), row3(norm_mix), row3(norm_ffn2)
    qg3, kg3, d3, bglu3 = row3(q_norm), row3(k_norm), row3(ssm_d), row3(b_glu)
    ffn1_in, ffn1_out = ffn1_w_in.astype(BF16), ffn1_w_out.astype(BF16)
    ffn2_in, ffn2_out = ffn2_w_in.astype(BF16), ffn2_w_out.astype(BF16)
    mix_in, mix_out, glu = w_mix_in.astype(BF16), w_mix_out.astype(BF16), w_glu.astype(BF16)

    def ffn(h, mod4, k, gain3, w_in, w_out, layer, rows):
        hn = _norm(h, gain3, layer, rows=rows, tr=tr, out_dtype=BF16, mod4=mod4, k=k, seq=seq, n_batch=n_batch)
        hid =_mm_swiglu(hn, w_in, layer, rows=rows, tm=tm, tn=_tile(w_in.shape[2] // 2, 512))
        return _mm_resid(hid, w_out, layer, h, mod4, k, 0.5, rows=rows, tm=tm, tn=_tile(d, 1024),
                         tk=_tile(w_out.shape[1], 2048), seq=seq, n_batch=n_batch)

    for layer in range(depth):
        last = layer == depth - 1
        mod4 = _mod_table(cc, w_ada, b_ada3, layer).reshape(8, 9, 1, d)

        h = ffn(h, mod4, 0, g_ffn1, ffn1_in, ffn1_out, layer, n_all)

        hn = _norm(h, g_mix, layer, rows=n_all, tr=tr, out_dtype=BF16, mod4=mod4, k=1, seq=seq, n_batch=n_batch)
        qkv, u = _mix_in(hn, mix_in, layer, cosf, sins, qg3, kg3, rows=n_all, tm=tm, n_q=n_q, n_kv=n_kv)
        attn = _flash(qkv, n_q=n_q, n_kv=n_kv, n_batch=n_batch, q_row0=0, q_len=seq,
                      kv_srcs=[(0, seq), (n_lat, n_ctx)], tq=min(256, seq))
        out_rows = n_lat if last else n_all
        attn_c = None
        if not last:
            attn_c = _flash(qkv, n_q=n_q, n_kv=n_kv, n_batch=n_batch, q_row0=n_lat, q_len=n_ctx,
                            kv_srcs=[(n_lat, n_ctx)], tq=min(128, n_ctx))

        weights = _ssm_weights(ssm_lam_re[layer], ssm_lam_im[layer], ssm_log_dt[layer], ssm_b_re[layer],
                               ssm_b_im[layer], ssm_c_re[layer], ssm_c_im[layer])
        y = _s5_mixer(u, n_batch, seq, n_ctx, weights)
        yg = _ssm_gelu(y, u, d3, layer, rows=out_rows, tr=tr)
        ssm_out = _mm_glu(yg, glu, bglu3, layer, rows=out_rows, tm=tm, tn=_tile(ssm_width, 512))

        h = _mm_mix_out(attn, attn_c, ssm_out, mix_out, layer, h, mod4, 1, rows=out_rows, tm=tm,
                        tn=_tile(d, 512), seq=seq, n_batch=n_batch)
        h = ffn(h, mod4, 2, g_ffn2, ffn2_in, ffn2_out, layer, out_rows)

    out = _norm(h, row3(norm_final[None]), 0, rows=n_lat, tr=tr, out_dtype=F32)
    return out.reshape(n_batch, seq, d)
```

```python
import functools
import math

import jax
import jax.numpy as jnp
from jax import lax
from jax.experimental import pallas as pl
from jax.experimental.pallas import tpu as pltpu

F32 = jnp.float32
BF16 = jnp.bfloat16

HEAD_DIM = 128
GQA_GROUP = 4
GRID_W = 64
ROPE_THETA = 10000.0
SSM_GROUP = 16
SSM_STATE = 64
SSM_CHUNK = 16
LANES = 128
GROUPS_PER_LANE_BLOCK = LANES // SSM_GROUP
EPS = 1e-6
VMEM_LIMIT_BYTES = 56 * 1024 * 1024


def _params(*sem):
    return pltpu.CompilerParams(dimension_semantics=sem, vmem_limit_bytes=VMEM_LIMIT_BYTES)


def _tile(n, pref, mult=LANES):
    if n <= pref:
        return n
    t = (pref // mult) * mult
    while t > mult and n % t:
        t -= mult
    assert n % t == 0, (n, pref, mult)
    return t


def _mod_kernel(c_ref, w_ref, b_ref, o_ref):
    c = c_ref[...]
    s = (c * jax.nn.sigmoid(c)).astype(BF16)
    o_ref[...] = jnp.dot(s, w_ref[...].astype(BF16), preferred_element_type=F32) + b_ref[...]


def _mod_table(cc, w_ada, b_ada3, layer):
    rows, d = cc.shape
    n = w_ada.shape[2]
    tn = _tile(n, 512)
    return pl.pallas_call(
        _mod_kernel,
        grid=(n // tn,),
        in_specs=[
            pl.BlockSpec((rows, d), lambda j: (0, 0)),
            pl.BlockSpec((None, d, tn), lambda j: (layer, 0, j)),
            pl.BlockSpec((None, 1, tn), lambda j: (layer, 0, j)),
        ],
        out_specs=pl.BlockSpec((rows, tn), lambda j: (0, j)),
        out_shape=jax.ShapeDtypeStruct((rows, n), F32),
        compiler_params=_params("parallel"),
        name="mod_table",
    )(cc, w_ada, b_ada3)


def _norm_kernel(x_ref, g_ref, *rest, modulate):
    o_ref = rest[-1]
    gain = g_ref[...]
    if modulate:
        sh_ref, sc_ref = rest[0], rest[1]
        gain = gain * (1.0 + sc_ref[...])
        shift = sh_ref[...]

    rc = 16

    def chunk(i, carry):
        rows = pl.ds(pl.multiple_of(i * rc, rc), rc)
        x = x_ref[rows, :]
        r = lax.rsqrt(jnp.mean(x * x, axis=-1, keepdims=True) + EPS)
        h = (x * r) * gain
        if modulate:
            h = h + shift
        o_ref[rows, :] = h.astype(o_ref.dtype)
        return carry

    lax.fori_loop(0, x_ref.shape[0] // rc, chunk, 0, unroll=True)


def _norm(x, gain3, layer, *, rows, tr, out_dtype, mod4=None, k=None, seq=None, n_batch=None):
    d = x.shape[1]
    modulate = mod4 is not None
    in_specs = [
        pl.BlockSpec((tr, d), lambda i: (i, 0)),
        pl.BlockSpec((None, 1, d), lambda i: (layer, 0, 0)),
    ]
    args = [x, gain3]
    if modulate:
        def mrow(i):
            return jnp.minimum((i * tr) // seq, n_batch)
        in_specs += [
            pl.BlockSpec((None, None, 1, d), lambda i: (mrow(i), 3 * k, 0, 0)),
            pl.BlockSpec((None, None, 1, d), lambda i: (mrow(i), 3 * k + 1, 0, 0)),
        ]
        args += [mod4, mod4]
    return pl.pallas_call(
        functools.partial(_norm_kernel, modulate=modulate),
        grid=(rows // tr,),
        in_specs=in_specs,
        out_specs=pl.BlockSpec((tr, d), lambda i: (i, 0)),
        out_shape=jax.ShapeDtypeStruct((rows, d), out_dtype),
        compiler_params=_params("parallel"),
        name="rmsnorm_mod" if modulate else "rmsnorm",
    )(*args)


def _mm_swiglu_kernel(x_ref, wg_ref, wu_ref, o_ref):
    x = x_ref[...]
    g = jnp.dot(x, wg_ref[...], preferred_element_type=F32)
    u = jnp.dot(x, wu_ref[...], preferred_element_type=F32)
    o_ref[...] = (g * jax.nn.sigmoid(g) * u).astype(o_ref.dtype)


def _mm_swiglu(x, w3, layer, *, rows, tm, tn):
    kdim, f = w3.shape[1], w3.shape[2] // 2
    nj = f // tn
    return pl.pallas_call(
        _mm_swiglu_kernel,
        grid=(rows // tm, nj),
        in_specs=[
            pl.BlockSpec((tm, kdim), lambda i, j: (i, 0)),
            pl.BlockSpec((None, kdim, tn), lambda i, j: (layer, 0, j)),
            pl.BlockSpec((None, kdim, tn), lambda i, j: (layer, 0, j + nj)),
        ],
        out_specs=pl.BlockSpec((tm, tn), lambda i, j: (i, j)),
        out_shape=jax.ShapeDtypeStruct((rows, f), BF16),
        compiler_params=_params("parallel", "parallel"),
        name="mm_swiglu",
    )(x, w3, w3)


def _mm_resid_kernel(x_ref, w_ref, res_ref, gate_ref, o_ref, *, nk, coef):
    def part():
        return (coef * gate_ref[...]) * jnp.dot(x_ref[...], w_ref[...], preferred_element_type=F32)

    if nk == 1:
        o_ref[...] = res_ref[...] + part()
    else:
        @pl.when(pl.program_id(2) == 0)
        def _():
            o_ref[...] = res_ref[...] + part()

        @pl.when(pl.program_id(2) != 0)
        def _():
            o_ref[...] += part()


def _mm_resid(x, w3, layer, res, mod4, k, coef, *, rows, tm, tn, tk, seq, n_batch):
    kdim, n = w3.shape[1], w3.shape[2]
    nk = kdim // tk

    def mrow(i):
        return jnp.minimum((i * tm) // seq, n_batch)

    return pl.pallas_call(
        functools.partial(_mm_resid_kernel, nk=nk, coef=coef),
        grid=(rows // tm, n // tn, nk),
        in_specs=[
            pl.BlockSpec((tm, tk), lambda i, j, kk: (i, kk)),
            pl.BlockSpec((None, tk, tn), lambda i, j, kk: (layer, kk, j)),
            pl.BlockSpec((tm, tn), lambda i, j, kk: (i, j)),
            pl.BlockSpec((None, None, 1, tn), lambda i, j, kk: (mrow(i), 3 * k + 2, 0, j)),
        ],
        out_specs=pl.BlockSpec((tm, tn), lambda i, j, kk: (i, j)),
        out_shape=jax.ShapeDtypeStruct((rows, n), F32),
        compiler_params=_params("parallel", "parallel", "arbitrary"),
        name="mm_resid",
    )(x, w3, res, mod4)


def _mm_glu_kernel(y_ref, w_ref, b_ref, yj_ref, o_ref):
    z = jnp.dot(y_ref[...], w_ref[...], preferred_element_type=F32) + b_ref[...]
    o_ref[...] = (yj_ref[...].astype(F32) * jax.nn.sigmoid(z)).astype(o_ref.dtype)


def _mm_glu(y, w3, b3, layer, *, rows, tm, tn):
    kdim, n = w3.shape[1], w3.shape[2]
    return pl.pallas_call(
        _mm_glu_kernel,
        grid=(rows // tm, n // tn),
        in_specs=[
            pl.BlockSpec((tm, kdim), lambda i, j: (i, 0)),
            pl.BlockSpec((None, kdim, tn), lambda i, j: (layer, 0, j)),
            pl.BlockSpec((None, 1, tn), lambda i, j: (layer, 0, j)),
            pl.BlockSpec((tm, tn), lambda i, j: (i, j)),
        ],
        out_specs=pl.BlockSpec((tm, tn), lambda i, j: (i, j)),
        out_shape=jax.ShapeDtypeStruct((rows, n), BF16),
        compiler_params=_params("parallel", "parallel"),
        name="mm_glu",
    )(y, w3, b3, y)


def _mm_mix_out_kernel(a_ref, ac_ref, s_ref, w_ref, res_ref, gate_ref, o_ref, *, n_lat_blocks, ka):
    a = jnp.where(pl.program_id(0) < n_lat_blocks, a_ref[...], ac_ref[...])
    total = jnp.dot(a, w_ref[:ka, :], preferred_element_type=F32)
    total += jnp.dot(s_ref[...], w_ref[ka:, :], preferred_element_type=F32)
    o_ref[...] = res_ref[...] + gate_ref[...] * total


def _mm_mix_out(attn, attn_c, ssm, w3, layer, res, mod4, k, *, rows, tm, tn, seq, n_batch):
    ka, ks = attn.shape[1], ssm.shape[1]
    n = w3.shape[2]
    n_lat_blocks = attn.shape[0] // tm
    if attn_c is None:
        attn_c = attn
    n_ctx_blocks = attn_c.shape[0] // tm

    def mrow(i):
        return jnp.minimum((i * tm) // seq, n_batch)

    return pl.pallas_call(
        functools.partial(_mm_mix_out_kernel, n_lat_blocks=n_lat_blocks, ka=ka),
        grid=(rows // tm, n // tn),
        in_specs=[
            pl.BlockSpec((tm, ka), lambda i, j: (jnp.minimum(i, n_lat_blocks - 1), 0)),
            pl.BlockSpec((tm, ka), lambda i, j: (jnp.clip(i - n_lat_blocks, 0, n_ctx_blocks - 1), 0)),
            pl.BlockSpec((tm, ks), lambda i, j: (i, 0)),
            pl.BlockSpec((None, ka + ks, tn), lambda i, j: (layer, 0, j)),
            pl.BlockSpec((tm, tn), lambda i, j: (i, j)),
            pl.BlockSpec((None, None, 1, tn), lambda i, j: (mrow(i), 3 * k + 2, 0, j)),
        ],
        out_specs=pl.BlockSpec((tm, tn), lambda i, j: (i, j)),
        out_shape=jax.ShapeDtypeStruct((rows, n), F32),
        compiler_params=_params("parallel", "parallel"),
        name="mm_mix_out",
    )(attn, attn_c, ssm, w3, res, mod4)


def _lane_group(shape, period, width):
    lane = lax.broadcasted_iota(jnp.int32, shape, 1)
    return (lane % period) // width


def _ssm_summary_kernel(x_ref, z_ref, o_ref, w_ref):
    T, H, GL, P = SSM_CHUNK, SSM_GROUP, GROUPS_PER_LANE_BLOCK, SSM_STATE
    grp = _lane_group((H, w_ref.shape[1]), GL * P, P)
    for gl in range(GL):
        keep = grp == gl
        for s in range(T):
            r0 = s * LANES + gl * H
            w_ref[r0:r0 + H, :] = jnp.where(keep, z_ref[s * H:(s + 1) * H, :], 0.0).astype(BF16)
    o_ref[...] = jnp.dot(x_ref[...], w_ref[...], preferred_element_type=F32)


def _ssm_summary(x, z):
    nj, m, kdim = x.shape
    n = z.shape[2]
    tn = n // 2
    return pl.pallas_call(
        _ssm_summary_kernel,
        grid=(nj, n // tn),
        in_specs=[
            pl.BlockSpec((None, m, kdim), lambda j, c: (j, 0, 0)),
            pl.BlockSpec((None, z.shape[1], tn), lambda j, c: (j, 0, c)),
        ],
        out_specs=pl.BlockSpec((None, m, tn), lambda j, c: (j, 0, c)),
        out_shape=jax.ShapeDtypeStruct((nj, m, n), F32),
        scratch_shapes=[pltpu.VMEM((kdim, tn), BF16)],
        compiler_params=_params("parallel", "parallel"),
        name="ssm_summary",
    )(x, z)


def _ssm_readout_kernel(h_ref, v_ref, add_ref, u_ref, d_ref, o_ref, w_ref):
    GL, P = GROUPS_PER_LANE_BLOCK, SSM_STATE
    grp = _lane_group((P, w_ref.shape[1]), LANES, SSM_GROUP)
    for gl in range(GL):
        keep = grp == gl
        for x in range(v_ref.shape[0]):
            r0 = (x * GL + gl) * P
            w_ref[r0:r0 + P, :] = jnp.where(keep, v_ref[x], 0.0).astype(BF16)
    y = add_ref[...] + jnp.dot(h_ref[...].astype(BF16), w_ref[...], preferred_element_type=F32)
    y = y + d_ref[...] * u_ref[...].astype(F32)
    o_ref[...] = jax.nn.gelu(y).astype(o_ref.dtype)


def _ssm_readout(h, v, add, u, d):
    nj, m, kdim = h.shape
    n = v.shape[3]
    tn = n // 4
    return pl.pallas_call(
        _ssm_readout_kernel,
        grid=(nj, n // tn),
        in_specs=[
            pl.BlockSpec((None, m, kdim), lambda j, c: (j, 0, 0)),
            pl.BlockSpec((None, v.shape[1], v.shape[2], tn), lambda j, c: (j, 0, 0, c)),
            pl.BlockSpec((None, m, tn), lambda j, c: (j, 0, c)),
            pl.BlockSpec((None, m, tn), lambda j, c: (j, 0, c)),
            pl.BlockSpec((None, 1, tn), lambda j, c: (j, 0, c)),
        ],
        out_specs=pl.BlockSpec((None, m, tn), lambda j, c: (j, 0, c)),
        out_shape=jax.ShapeDtypeStruct((nj, m, n), BF16),
        scratch_shapes=[pltpu.VMEM((kdim, tn), BF16)],
        compiler_params=_params("parallel", "parallel"),
        name="ssm_readout",
    )(h, v, add, u, d)


def _ssm_intra_kernel(x_ref, t_ref, o_ref, w_ref, *, n_t):
    T = SSM_CHUNK
    for half in range(T // n_t):
        @pl.when(pl.program_id(1) == half)
        def _(half=half):
            for s in range(T):
                for tt in range(n_t):
                    lag = half * n_t + tt - s + T - 1
                    w_ref[s * LANES:(s + 1) * LANES, tt * LANES:(tt + 1) * LANES] = t_ref[lag]
    o_ref[...] = jnp.dot(x_ref[...], w_ref[...], preferred_element_type=F32)


def _ssm_intra(x, tiles):
    nj, m, kdim = x.shape
    n_t = SSM_CHUNK // 2
    tn = n_t * LANES
    return pl.pallas_call(
        functools.partial(_ssm_intra_kernel, n_t=n_t),
        grid=(nj, kdim // tn),
        in_specs=[
            pl.BlockSpec((None, m, kdim), lambda j, c: (j, 0, 0)),
            pl.BlockSpec((None, 2 * SSM_CHUNK - 1, LANES, LANES), lambda j, c: (j, 0, 0, 0)),
        ],
        out_specs=pl.BlockSpec((None, m, tn), lambda j, c: (j, 0, c)),
        out_shape=jax.ShapeDtypeStruct((nj, m, kdim), F32),
        scratch_shapes=[pltpu.VMEM((kdim, tn), BF16)],
        compiler_params=_params("parallel", "arbitrary"),
        name="ssm_intra",
    )(x, tiles)


def _mix_in_kernel(x_ref, w_ref, cos_ref, sin_ref, qg_ref, kg_ref, qkv_ref, u_ref, *, nq_tiles, heads):
    j = pl.program_id(1)

    def proj():
        return jnp.dot(x_ref[...], w_ref[...], preferred_element_type=F32)

    @pl.when(j <= nq_tiles)
    def _():
        is_q = j < nq_tiles
        gain = jnp.where(is_q, qg_ref[...], kg_ref[...])
        scale = jnp.where(is_q, HEAD_DIM ** -0.5 * math.log2(math.e), 1.0)
        rows = x_ref.shape[0]
        rc = min(rows, 256)
        for r0 in range(0, rows, rc):
            p = jnp.dot(x_ref[r0:r0 + rc, :], w_ref[...], preferred_element_type=F32)
            cosf = cos_ref[r0:r0 + rc, :]
            sins = sin_ref[r0:r0 + rc, :]
            lane = lax.broadcasted_iota(jnp.int32, cosf.shape, 1)
            first_half = (lane % (HEAD_DIM // 2)) < (HEAD_DIM // 4)
            for h in range(heads):
                sl = slice(h * HEAD_DIM, (h + 1) * HEAD_DIM)
                t = p[:, sl]
                r = lax.rsqrt(jnp.mean(t * t, axis=-1, keepdims=True) + EPS)
                t = (t * r) * gain
                partner = jnp.where(first_half,
                                    pltpu.roll(t, HEAD_DIM - HEAD_DIM // 4, axis=1),
                                    pltpu.roll(t, HEAD_DIM // 4, axis=1))
                t = (t * cosf + partner * sins) * scale
                qkv_ref[r0:r0 + rc, sl] = t.astype(qkv_ref.dtype)

    @pl.when(j == nq_tiles + 1)
    def _():
        qkv_ref[...] = proj().astype(qkv_ref.dtype)

    @pl.when(j > nq_tiles + 1)
    def _():
        u_ref[...] = proj().astype(u_ref.dtype)


def _mix_in(x, w3, layer, cosf, sins, qg3, kg3, *, rows, tm, n_q, n_kv):
    kdim, n = w3.shape[1], w3.shape[2]
    tn = n_kv * HEAD_DIM
    nq_tiles = n_q // n_kv
    n_qkv_tiles = nq_tiles + 2
    n_tiles = n // tn
    return pl.pallas_call(
        functools.partial(_mix_in_kernel, nq_tiles=nq_tiles, heads=n_kv),
        grid=(rows // tm, n_tiles),
        in_specs=[
            pl.BlockSpec((tm, kdim), lambda i, j: (i, 0)),
            pl.BlockSpec((None, kdim, tn), lambda i, j: (layer, 0, j)),
            pl.BlockSpec((tm, HEAD_DIM), lambda i, j: (i, 0)),
            pl.BlockSpec((tm, HEAD_DIM), lambda i, j: (i, 0)),
            pl.BlockSpec((None, 1, HEAD_DIM), lambda i, j: (layer, 0, 0)),
            pl.BlockSpec((None, 1, HEAD_DIM), lambda i, j: (layer, 0, 0)),
        ],
        out_specs=[
            pl.BlockSpec((tm, tn), lambda i, j: (i, jnp.minimum(j, n_qkv_tiles - 1))),
            pl.BlockSpec((tm, tn), lambda i, j: (i, jnp.maximum(j - n_qkv_tiles, 0))),
        ],
        out_shape=[
            jax.ShapeDtypeStruct((rows, n_qkv_tiles * tn), BF16),
            jax.ShapeDtypeStruct((rows, n - n_qkv_tiles * tn), BF16),
        ],
        compiler_params=_params("parallel", "arbitrary"),
        name="mix_in",
    )(x, w3, cosf, sins, qg3, kg3)


def _flash_kernel(q_ref, *rest, tq, kv_lens, tkv):
    o_ref = rest[-1]
    n_src = len(kv_lens)
    k_refs, v_refs = rest[:n_src], rest[n_src:2 * n_src]
    q = jnp.concatenate([q_ref[:, h * HEAD_DIM:(h + 1) * HEAD_DIM] for h in range(GQA_GROUP)], axis=0)
    nrow = GQA_GROUP * tq

    def step(k, v, carry):
        m, l, acc = carry
        s = lax.dot_general(q, k, (((1,), (1,)), ((), ())), preferred_element_type=F32)
        m_new = jnp.maximum(m, jnp.max(s, axis=-1, keepdims=True))
        alpha = jnp.exp2(m - m_new)
        p = jnp.exp2(s - m_new)
        l = alpha * l + jnp.sum(p, axis=-1, keepdims=True)
        acc = alpha * acc + jnp.dot(p.astype(BF16), v, preferred_element_type=F32)
        return m_new, l, acc

    carry = (jnp.full((nrow, 1), -jnp.inf, F32), jnp.zeros((nrow, 1), F32), jnp.zeros((nrow, HEAD_DIM), F32))
    for k_ref, v_ref, length in zip(k_refs, v_refs, kv_lens):
        t = min(tkv, length)
        for c in range(length // t):
            carry = step(k_ref[c * t:(c + 1) * t, :], v_ref[c * t:(c + 1) * t, :], carry)
    _, l, acc = carry
    out = acc / l
    for h in range(GQA_GROUP):
        o_ref[:, h * HEAD_DIM:(h + 1) * HEAD_DIM] = out[h * tq:(h + 1) * tq].astype(o_ref.dtype)


def _flash(qkv, *, n_q, n_kv, n_batch, q_row0, q_len, kv_srcs, tq):
    qw = GQA_GROUP * HEAD_DIM
    nq_t = q_len // tq
    assert q_row0 % tq == 0
    in_specs = [pl.BlockSpec((tq, qw), lambda b, g, i: (q_row0 // tq + b * nq_t + i, g))]
    args = [qkv]
    for col0 in (n_q, n_q + n_kv):
        for (row0, length) in kv_srcs:
            assert row0 % length == 0
            in_specs.append(pl.BlockSpec(
                (length, HEAD_DIM),
                lambda b, g, i, row0=row0, length=length, col0=col0: (row0 // length + b, col0 + g)))
            args.append(qkv)
    return pl.pallas_call(
        functools.partial(_flash_kernel, tq=tq, kv_lens=tuple(l for _, l in kv_srcs), tkv=256),
        grid=(n_batch, n_kv, nq_t),
        in_specs=in_specs,
        out_specs=pl.BlockSpec((tq, qw), lambda b, g, i: (b * nq_t + i, g)),
        out_shape=jax.ShapeDtypeStruct((n_batch * q_len, n_q * HEAD_DIM), BF16),
        compiler_params=_params("parallel", "parallel", "arbitrary"),
        name="gqa_attention",
    )(*args)


def _ssm_taps_kernel(z_ref, c_ref, o_ref):
    o_ref[...] = lax.dot_general(z_ref[...], c_ref[...], (((1,), (1,)), ((), ())),
                                 precision=lax.Precision.HIGHEST, preferred_element_type=F32)


def _ssm_taps(z, c):
    nj, nd, rows, p2 = z.shape
    return pl.pallas_call(
        _ssm_taps_kernel,
        grid=(nj, nd),
        in_specs=[
            pl.BlockSpec((None, None, rows, p2), lambda j, d: (j, d, 0, 0)),
            pl.BlockSpec((None, None, LANES, p2), lambda j, d: (j, d, 0, 0)),
        ],
        out_specs=pl.BlockSpec((None, None, rows, LANES), lambda j, d: (j, d, 0, 0)),
        out_shape=jax.ShapeDtypeStruct((nj, nd, rows, LANES), F32),
        compiler_params=_params("parallel", "parallel"),
        name="ssm_taps",
    )(z, c)


def _ssm_scan_kernel(s_ref, a_ref, o_ref, *, n_ctx_tiles, n_tiles, half):
    row = lax.broadcasted_iota(jnp.int32, (8, half), 0)
    top = row < 4

    def run(col0, a_re, a_im, tiles_in_order, second_first):
        first = jnp.logical_not(top) if second_first else top

        def dup(t):
            return jnp.where(first, t, pltpu.roll(t, 4, axis=0))

        def tile_step(ti, carry):
            hr, hi = carry
            r0 = pl.multiple_of(ti * 8, 8)
            sr = s_ref[pl.ds(r0, 8), col0:col0 + half]
            si = s_ref[pl.ds(r0, 8), col0 + half:col0 + 2 * half]
            t1r = a_re * hr - a_im * hi + sr
            t1i = a_re * hi + a_im * hr + si
            h1r, h1i = dup(t1r), dup(t1i)
            o_ref[pl.ds(r0, 8), col0:col0 + half] = jnp.where(first, hr, h1r)
            o_ref[pl.ds(r0, 8), col0 + half:col0 + 2 * half] = jnp.where(first, hi, h1i)
            t2r = a_re * h1r - a_im * h1i + sr
            t2i = a_re * h1i + a_im * h1r + si
            second = jnp.logical_not(first)
            nr = jnp.where(second, t2r, pltpu.roll(t2r, 4, axis=0))
            ni = jnp.where(second, t2i, pltpu.roll(t2i, 4, axis=0))
            return nr, ni

        carry = (jnp.zeros((8, half), F32), jnp.zeros((8, half), F32))
        for (start, count, reverse) in tiles_in_order:
            if reverse:
                carry = lax.fori_loop(0, count, lambda i, c, s=start, n=count: tile_step(s + n - 1 - i, c), carry)
            else:
                carry = lax.fori_loop(0, count, lambda i, c, s=start: tile_step(s + i, c), carry)

    n_lat = n_tiles - n_ctx_tiles
    a = a_ref[...]
    run(0, a[0:1], a[1:2], [(0, n_ctx_tiles, False), (n_ctx_tiles, n_lat, False)], False)
    run(2 * half, a[2:3], a[3:4], [(0, n_ctx_tiles, True), (n_ctx_tiles, n_lat, True)], True)


def _ssm_scan(s, a_pow, *, n_ctx_tiles):
    nj, m, w = s.shape
    half = w // 4
    return pl.pallas_call(
        functools.partial(_ssm_scan_kernel, n_ctx_tiles=n_ctx_tiles, n_tiles=m // 8, half=half),
        grid=(nj,),
        in_specs=[
            pl.BlockSpec((None, m, w), lambda j: (j, 0, 0)),
            pl.BlockSpec((None, 8, half), lambda j: (j, 0, 0)),
        ],
        out_specs=pl.BlockSpec((None, m, w), lambda j: (j, 0, 0)),
        out_shape=jax.ShapeDtypeStruct((nj, m, w), F32),
        compiler_params=_params("parallel"),
        name="ssm_scan",
    )(s, a_pow)


def _ssm_weights(lam_re, lam_im, log_dt, b_re, b_im, c_re, c_im):
    T, H, P = SSM_CHUNK, SSM_GROUP, SSM_STATE
    G = lam_re.shape[1]
    GL = GROUPS_PER_LANE_BLOCK
    J = G // GL
    lam_re, lam_im = lam_re.astype(F32), lam_im.astype(F32)
    dt = jnp.exp(log_dt.astype(F32))[..., None]
    mag = jnp.exp(lam_re * dt)
    ab_r, ab_i = mag * jnp.cos(lam_im * dt), mag * jnp.sin(lam_im * dt)
    nr, ni = ab_r - 1.0, ab_i
    den = lam_re * lam_re + lam_im * lam_im
    cr = (nr * lam_re + ni * lam_im) / den
    ci = (ni * lam_re - nr * lam_im) / den
    b_re, b_im = b_re.astype(F32), b_im.astype(F32)
    bb_r = cr[..., None] * b_re - ci[..., None] * b_im
    bb_i = cr[..., None] * b_im + ci[..., None] * b_re
    c_re, c_im = c_re.astype(F32), c_im.astype(F32)

    def powers(d, exps):
        e = jnp.asarray(exps, F32)[:, None, None]
        pmag = jnp.exp(e * (lam_re[d] * dt[d])[None])
        pang = e * (lam_im[d] * dt[d])[None]
        return pmag * jnp.cos(pang), pmag * jnp.sin(pang)

    def drive(d, exps):
        pr, pi = powers(d, exps)
        return (pr[..., None] * bb_r[d][None] - pi[..., None] * bb_i[d][None],
                pr[..., None] * bb_i[d][None] + pi[..., None] * bb_r[d][None])

    asc = list(range(T))
    desc = [T - 1 - s for s in range(T)]

    def group_mask(n_rows, n_cols, rows_per_group, cols_per_group):
        r = lax.broadcasted_iota(jnp.int32, (n_rows, n_cols), 0) // rows_per_group
        c = lax.broadcasted_iota(jnp.int32, (n_rows, n_cols), 1) // cols_per_group
        return (r == c).astype(F32)

    def tap_lhs(d, exps):
        zr, zi = drive(d, exps)
        z = jnp.concatenate([zr, zi], axis=2).reshape(T, J, GL, 2 * P, H)
        return z.transpose(1, 0, 2, 4, 3).reshape(J, T * LANES, 2 * P)
    z_all = jnp.stack([tap_lhs(0, asc), tap_lhs(1, desc)], axis=1)
    c_all = jnp.concatenate([c_re, -c_im], axis=-1).reshape(2, J, LANES, 2 * P).transpose(1, 0, 2, 3)
    raw = _ssm_taps(z_all, c_all).reshape(J, 2, T, LANES, LANES)
    fwd, bwd = raw[:, 0], raw[:, 1]
    tiles = jnp.concatenate([bwd[:, :T - 1], bwd[:, T - 1:] + fwd[:, :1], fwd[:, 1:]], axis=1)
    tiles = (tiles * group_mask(LANES, LANES, H, H)).astype(BF16)

    def summary(d, exps):
        pr, pi = powers(d, exps)
        pr, pi = (a.reshape(T, J, 1, GL * P).transpose(1, 0, 2, 3) for a in (pr, pi))
        br, bi = (a.reshape(J, GL, P, H).transpose(0, 3, 1, 2).reshape(J, 1, H, GL * P)
                  for a in (bb_r[d], bb_i[d]))
        return jnp.concatenate([pr * br - pi * bi, pr * bi + pi * br], axis=-1).reshape(J, T * H, 2 * GL * P)
    z_sum = jnp.concatenate([summary(0, desc), summary(1, asc)], axis=-1)

    def readout(d, exps):
        pr, pi = powers(d, exps)
        pr, pi = (jnp.repeat(a.reshape(T, J, GL, P).transpose(1, 3, 0, 2), H, axis=-1)
                  for a in (pr, pi))
        cr_l, ci_l = (a.reshape(J, GL, H, P).transpose(0, 3, 1, 2).reshape(J, P, 1, LANES)
                      for a in (c_re[d], c_im[d]))
        vr = (cr_l * pr - ci_l * pi).reshape(J, 1, P, T * LANES)
        vi = (cr_l * pi + ci_l * pr).reshape(J, 1, P, T * LANES)
        return jnp.concatenate([vr, -vi], axis=1)
    v_out = jnp.concatenate([readout(0, [t + 1 for t in range(T)]), readout(1, [T - t for t in range(T)])],
                            axis=1)

    def lanes(x):
        return x.reshape(J, GL * P)
    a_pow = jnp.stack([lanes(a) for d in range(2) for a in powers(d, [T])], axis=1)
    a_pow = jnp.concatenate([a_pow, jnp.zeros_like(a_pow)], axis=1)
    return tiles, z_sum, v_out, a_pow


def _s5_mixer(u, d_skip, n_batch, seq, n_ctx, weights):
    tiles, z_sum, v_out, a_pow = weights
    T = SSM_CHUNK
    J = tiles.shape[0]
    n_lat_rows = n_batch * seq
    d_chunk = jnp.tile(d_skip.astype(F32).reshape(J, 1, LANES), (1, 1, T))

    def to_chunks(rows, length):
        r = rows.reshape(n_batch, length // T, T, J, LANES).transpose(3, 1, 0, 2, 4)
        return r.reshape(J, (length // T) * n_batch, T * LANES)

    lhs = jnp.concatenate([to_chunks(u[n_lat_rows:], n_ctx), to_chunks(u[:n_lat_rows], seq)], axis=1)
    y_intra = _ssm_intra(lhs, tiles)
    summaries = _ssm_summary(lhs, z_sum)
    h_in = _ssm_scan(summaries, a_pow, n_ctx_tiles=(n_ctx // T) * n_batch // 8)
    y = _ssm_readout(h_in, v_out, y_intra, lhs, d_chunk)

    def from_chunks(part, length):
        r = part.reshape(J, length // T, n_batch, T, LANES).transpose(2, 1, 3, 0, 4)
        return r.reshape(n_batch * length, J * LANES)

    n_ctx_chunk_rows = (n_ctx // T) * n_batch
    return jnp.concatenate([from_chunks(y[:, n_ctx_chunk_rows:], seq), from_chunks(y[:, :n_ctx_chunk_rows], n_ctx)],
                           axis=0)


def _rope_tables(seq, n_batch, n_ctx_rows):
    rows_n = seq // GRID_W
    pos = jnp.arange(seq)
    rows = (pos // GRID_W).astype(F32)
    cols = (pos % GRID_W).astype(F32)
    del rows_n
    axis_dim = HEAD_DIM // 2
    inv_freq = ROPE_THETA ** (-jnp.arange(0, axis_dim, 2, dtype=F32) / axis_dim)
    ang_r, ang_c = rows[:, None] * inv_freq, cols[:, None] * inv_freq
    cosf = jnp.concatenate([jnp.cos(ang_r)] * 2 + [jnp.cos(ang_c)] * 2, axis=1)
    sins = jnp.concatenate([-jnp.sin(ang_r), jnp.sin(ang_r), -jnp.sin(ang_c), jnp.sin(ang_c)], axis=1)
    cosf = jnp.concatenate([jnp.tile(cosf, (n_batch, 1)), jnp.ones((n_ctx_rows, HEAD_DIM), F32)], axis=0)
    sins = jnp.concatenate([jnp.tile(sins, (n_batch, 1)), jnp.zeros((n_ctx_rows, HEAD_DIM), F32)], axis=0)
    return cosf, sins


def kernel(x, c, ctx, c_ctx, w_ada, b_ada, norm_ffn1, ffn1_w_in, ffn1_w_out, norm_mix, w_mix_in, q_norm, k_norm, ssm_lam_re, ssm_lam_im, ssm_log_dt, ssm_b_re, ssm_b_im, ssm_c_re, ssm_c_im, ssm_d, w_glu, b_glu, w_mix_out, norm_ffn2, ffn2_w_in, ffn2_w_out, norm_final):
    n_batch, seq, d = x.shape
    n_ctx = ctx.shape[1]
    depth = w_ada.shape[0]
    ssm_width = w_glu.shape[1]
    attn_width = w_mix_out.shape[1] - ssm_width
    n_q = attn_width // HEAD_DIM
    n_kv = n_q // GQA_GROUP
    n_lat = n_batch * seq
    n_ctx_rows = n_batch * n_ctx
    n_all = n_lat + n_ctx_rows
    tm = n_ctx_rows
    tr = _tile(tm, 256, 8)
    assert seq % tm == 0 and n_batch == 4

    h = jnp.concatenate([x.reshape(n_lat, d), ctx.reshape(n_ctx_rows, d)], axis=0)
    cc = jnp.concatenate([c, c_ctx[None], jnp.zeros((8 - n_batch - 1, d), F32)], axis=0)
    cosf, sins = _rope_tables(seq, n_batch, n_ctx_rows)

    def row3(a):
        return a.reshape(a.shape[0], 1, a.shape[1])

    b_ada3 = row3(b_ada)
    g_ffn1, g_mix, g_ffn2 = row3(norm_ffn1), row3(norm_mix), row3(norm_ffn2)
    qg3, kg3, bglu3 = row3(q_norm), row3(k_norm), row3(b_glu)
    ffn1_in, ffn1_out = ffn1_w_in.astype(BF16), ffn1_w_out.astype(BF16)
    ffn2_in, ffn2_out = ffn2_w_in.astype(BF16), ffn2_w_out.astype(BF16)
    mix_in, mix_out, glu = w_mix_in.astype(BF16), w_mix_out.astype(BF16), w_glu.astype(BF16)

    def ffn(h, mod4, k, gain3, w_in, w_out, layer, rows):
        hn = _norm(h, gain3, layer, rows=rows, tr=tr, out_dtype=BF16, mod4=mod4, k=k, seq=seq, n_batch=n_batch)
        hid =_mm_swiglu(hn, w_in, layer, rows=rows, tm=tm, tn=_tile(w_in.shape[2] // 2, 512))
        return _mm_resid(hid, w_out, layer, h, mod4, k, 0.5, rows=rows, tm=tm, tn=_tile(d, 1024),
                         tk=_tile(w_out.shape[1], 4096), seq=seq, n_batch=n_batch)

    for layer in range(depth):
        last = layer == depth - 1
        mod4 = _mod_table(cc, w_ada, b_ada3, layer).reshape(8, 9, 1, d)

        h = ffn(h, mod4, 0, g_ffn1, ffn1_in, ffn1_out, layer, n_all)

        hn = _norm(h, g_mix, layer, rows=n_all, tr=tr, out_dtype=BF16, mod4=mod4, k=1, seq=seq, n_batch=n_batch)
        qkv, u = _mix_in(hn, mix_in, layer, cosf, sins, qg3, kg3, rows=n_all, tm=tm, n_q=n_q, n_kv=n_kv)
        attn = _flash(qkv, n_q=n_q, n_kv=n_kv, n_batch=n_batch, q_row0=0, q_len=seq,
                      kv_srcs=[(0, seq), (n_lat, n_ctx)], tq=min(256, seq))
        out_rows = n_lat if last else n_all
        attn_c = None
        if not last:
            attn_c = _flash(qkv, n_q=n_q, n_kv=n_kv, n_batch=n_batch, q_row0=n_lat, q_len=n_ctx,
                            kv_srcs=[(n_lat, n_ctx)], tq=min(128, n_ctx))

        weights = _ssm_weights(ssm_lam_re[layer], ssm_lam_im[layer], ssm_log_dt[layer], ssm_b_re[layer],
                               ssm_b_im[layer], ssm_c_re[layer], ssm_c_im[layer])
        yg = _s5_mixer(u, ssm_d[layer], n_batch, seq, n_ctx, weights)
        ssm_out = _mm_glu(yg, glu, bglu3, layer, rows=out_rows, tm=tm, tn=_tile(ssm_width, 512))

        h = _mm_mix_out(attn, attn_c, ssm_out, mix_out, layer, h, mod4, 1, rows=out_rows, tm=tm,
                        tn=_tile(d, 512), seq=seq, n_batch=n_batch)
        h = ffn(h, mod4, 2, g_ffn2, ffn2_in, ffn2_out, layer, out_rows)

    out = _norm(h, row3(norm_final[None]), 0, rows=n_lat, tr=tr, out_dtype=F32)
    return out.reshape(n_batch, seq, d)
```

```python
import functools
import math

import jax
import jax.numpy as jnp
from jax import lax
from jax.experimental import pallas as pl
from jax.experimental.pallas import tpu as pltpu

F32 = jnp.float32
BF16 = jnp.bfloat16

HEAD_DIM = 128
GQA_GROUP = 4
GRID_W = 64
ROPE_THETA = 10000.0
SSM_GROUP = 16
SSM_STATE = 64
SSM_CHUNK = 16
LANES = 128
GROUPS_PER_LANE_BLOCK = LANES // SSM_GROUP
EPS = 1e-6
VMEM_LIMIT_BYTES = 56 * 1024 * 1024


def _params(*sem):
    return pltpu.CompilerParams(dimension_semantics=sem, vmem_limit_bytes=VMEM_LIMIT_BYTES)


def _tile(n, pref, mult=LANES):
    if n <= pref:
        return n
    t = (pref // mult) * mult
    while t > mult and n % t:
        t -= mult
    assert n % t == 0, (n, pref, mult)
    return t


def _mod_kernel(c_ref, w_ref, b_ref, o_ref):
    c = c_ref[...]
    s = (c * jax.nn.sigmoid(c)).astype(BF16)
    o_ref[...] = jnp.dot(s, w_ref[...].astype(BF16), preferred_element_type=F32) + b_ref[...]


def _mod_table(cc, w_ada, b_ada3, layer):
    rows, d = cc.shape
    n = w_ada.shape[2]
    tn = _tile(n, 512)
    return pl.pallas_call(
        _mod_kernel,
        grid=(n // tn,),
        in_specs=[
            pl.BlockSpec((rows, d), lambda j: (0, 0)),
            pl.BlockSpec((None, d, tn), lambda j: (layer, 0, j)),
            pl.BlockSpec((None, 1, tn), lambda j: (layer, 0, j)),
        ],
        out_specs=pl.BlockSpec((rows, tn), lambda j: (0, j)),
        out_shape=jax.ShapeDtypeStruct((rows, n), F32),
        compiler_params=_params("parallel"),
        name="mod_table",
    )(cc, w_ada, b_ada3)


def _norm_kernel(x_ref, *rest, modulate, n_head_blocks):
    o_ref = rest[-1]
    if n_head_blocks is not None:
        tail_ref, rest = rest[0], rest[1:]
        from_head = pl.program_id(0) < n_head_blocks
    g_ref = rest[0]
    gain = g_ref[...]
    if modulate:
        sh_ref, sc_ref = rest[1], rest[2]
        gain = gain * (1.0 + sc_ref[...])
        shift = sh_ref[...]

    rc = 16

    def chunk(i, carry):
        rows = pl.ds(pl.multiple_of(i * rc, rc), rc)
        x = x_ref[rows, :]
        if n_head_blocks is not None:
            x = jnp.where(from_head, x, tail_ref[rows, :])
        r = lax.rsqrt(jnp.mean(x * x, axis=-1, keepdims=True) + EPS)
        h = (x * r) * gain
        if modulate:
            h = h + shift
        o_ref[rows, :] = h.astype(o_ref.dtype)
        return carry

    lax.fori_loop(0, x_ref.shape[0] // rc, chunk, 0, unroll=True)


def _two_source_specs(head, tail, block, col_of):
    nh, nt = head.shape[0] // block[0], tail.shape[0] // block[0]
    return [
        pl.BlockSpec(block, lambda i, *g: (jnp.minimum(i, nh - 1), col_of(*g))),
        pl.BlockSpec(block, lambda i, *g: (jnp.clip(i - nh, 0, nt - 1), col_of(*g))),
    ], nh


def _norm(x, gain3, layer, *, rows, tr, out_dtype, mod4=None, k=None, seq=None, n_batch=None, x_tail=None):
    d = x.shape[1]
    modulate = mod4 is not None
    n_head_blocks = None
    if x_tail is None:
        in_specs, args = [pl.BlockSpec((tr, d), lambda i: (i, 0))], [x]
    else:
        in_specs, n_head_blocks = _two_source_specs(x, x_tail, (tr, d), lambda: 0)
        args = [x, x_tail]
    in_specs.append(pl.BlockSpec((None, 1, d), lambda i: (layer, 0, 0)))
    args.append(gain3)
    if modulate:
        def mrow(i):
            return jnp.minimum((i * tr) // seq, n_batch)
        in_specs += [
            pl.BlockSpec((None, None, 1, d), lambda i: (mrow(i), 3 * k, 0, 0)),
            pl.BlockSpec((None, None, 1, d), lambda i: (mrow(i), 3 * k + 1, 0, 0)),
        ]
        args += [mod4, mod4]
    return pl.pallas_call(
        functools.partial(_norm_kernel, modulate=modulate, n_head_blocks=n_head_blocks),
        grid=(rows // tr,),
        in_specs=in_specs,
        out_specs=pl.BlockSpec((tr, d), lambda i: (i, 0)),
        out_shape=jax.ShapeDtypeStruct((rows, d), out_dtype),
        compiler_params=_params("parallel"),
        name="rmsnorm_mod" if modulate else "rmsnorm",
    )(*args)


def _mm_swiglu_kernel(x_ref, wg_ref, wu_ref, o_ref):
    x = x_ref[...]
    g = jnp.dot(x, wg_ref[...], preferred_element_type=F32)
    u = jnp.dot(x, wu_ref[...], preferred_element_type=F32)
    o_ref[...] = (g * jax.nn.sigmoid(g) * u).astype(o_ref.dtype)


def _mm_swiglu(x, w3, layer, *, rows, tm, tn):
    kdim, f = w3.shape[1], w3.shape[2] // 2
    nj = f // tn
    return pl.pallas_call(
        _mm_swiglu_kernel,
        grid=(rows // tm, nj),
        in_specs=[
            pl.BlockSpec((tm, kdim), lambda i, j: (i, 0)),
            pl.BlockSpec((None, kdim, tn), lambda i, j: (layer, 0, j)),
            pl.BlockSpec((None, kdim, tn), lambda i, j: (layer, 0, j + nj)),
        ],
        out_specs=pl.BlockSpec((tm, tn), lambda i, j: (i, j)),
        out_shape=jax.ShapeDtypeStruct((rows, f), BF16),
        compiler_params=_params("parallel", "parallel"),
        name="mm_swiglu",
    )(x, w3, w3)


def _mm_resid_kernel(x_ref, w_ref, gate_ref, res_ref, *rest, nk, coef, n_head_blocks):
    o_ref = rest[-1]

    def part():
        return (coef * gate_ref[...]) * jnp.dot(x_ref[...], w_ref[...], preferred_element_type=F32)

    def residual():
        if n_head_blocks is None:
            return res_ref[...]
        return jnp.where(pl.program_id(0) < n_head_blocks, res_ref[...], rest[0][...])

    if nk == 1:
        o_ref[...] = residual() + part()
    else:
        @pl.when(pl.program_id(2) == 0)
        def _():
            o_ref[...] = residual() + part()

        @pl.when(pl.program_id(2) != 0)
        def _():
            o_ref[...] += part()


def _mm_resid(x, w3, layer, res, mod4, k, coef, *, rows, tm, tn, tk, seq, n_batch, res_tail=None):
    kdim, n = w3.shape[1], w3.shape[2]
    nk = kdim // tk

    def mrow(i):
        return jnp.minimum((i * tm) // seq, n_batch)

    n_head_blocks = None
    if res_tail is None:
        res_specs, res_args = [pl.BlockSpec((tm, tn), lambda i, j, kk: (i, j))], [res]
    else:
        res_specs, n_head_blocks = _two_source_specs(res, res_tail, (tm, tn), lambda j, kk: j)
        res_args = [res, res_tail]
    return pl.pallas_call(
        functools.partial(_mm_resid_kernel, nk=nk, coef=coef, n_head_blocks=n_head_blocks),
        grid=(rows // tm, n // tn, nk),
        in_specs=[
            pl.BlockSpec((tm, tk), lambda i, j, kk: (i, kk)),
            pl.BlockSpec((None, tk, tn), lambda i, j, kk: (layer, kk, j)),
            pl.BlockSpec((None, None, 1, tn), lambda i, j, kk: (mrow(i), 3 * k + 2, 0, j)),
        ] + res_specs,
        out_specs=pl.BlockSpec((tm, tn), lambda i, j, kk: (i, j)),
        out_shape=jax.ShapeDtypeStruct((rows, n), F32),
        compiler_params=_params("parallel", "parallel", "arbitrary"),
        name="mm_resid",
    )(x, w3, mod4, *res_args)


def _mm_glu_kernel(y_ref, w_ref, b_ref, yj_ref, o_ref):
    z = jnp.dot(y_ref[...], w_ref[...], preferred_element_type=F32) + b_ref[...]
    o_ref[...] = (yj_ref[...].astype(F32) * jax.nn.sigmoid(z)).astype(o_ref.dtype)


def _mm_glu(y, w3, b3, layer, *, rows, tm, tn):
    kdim, n = w3.shape[1], w3.shape[2]
    return pl.pallas_call(
        _mm_glu_kernel,
        grid=(rows // tm, n // tn),
        in_specs=[
            pl.BlockSpec((tm, kdim), lambda i, j: (i, 0)),
            pl.BlockSpec((None, kdim, tn), lambda i, j: (layer, 0, j)),
            pl.BlockSpec((None, 1, tn), lambda i, j: (layer, 0, j)),
            pl.BlockSpec((tm, tn), lambda i, j: (i, j)),
        ],
        out_specs=pl.BlockSpec((tm, tn), lambda i, j: (i, j)),
        out_shape=jax.ShapeDtypeStruct((rows, n), BF16),
        compiler_params=_params("parallel", "parallel"),
        name="mm_glu",
    )(y, w3, b3, y)


def _mm_mix_out_kernel(a_ref, ac_ref, s_ref, w_ref, res_ref, gate_ref, o_ref, *, n_lat_blocks, ka):
    a = jnp.where(pl.program_id(0) < n_lat_blocks, a_ref[...], ac_ref[...])
    total = jnp.dot(a, w_ref[:ka, :], preferred_element_type=F32)
    total += jnp.dot(s_ref[...], w_ref[ka:, :], preferred_element_type=F32)
    o_ref[...] = res_ref[...] + gate_ref[...] * total


def _mm_mix_out(attn, attn_c, ssm, w3, layer, res, mod4, k, *, rows, tm, tn, seq, n_batch):
    ka, ks = attn.shape[1], ssm.shape[1]
    n = w3.shape[2]
    n_lat_blocks = attn.shape[0] // tm
    if attn_c is None:
        attn_c = attn
    n_ctx_blocks = attn_c.shape[0] // tm

    def mrow(i):
        return jnp.minimum((i * tm) // seq, n_batch)

    return pl.pallas_call(
        functools.partial(_mm_mix_out_kernel, n_lat_blocks=n_lat_blocks, ka=ka),
        grid=(rows // tm, n // tn),
        in_specs=[
            pl.BlockSpec((tm, ka), lambda i, j: (jnp.minimum(i, n_lat_blocks - 1), 0)),
            pl.BlockSpec((tm, ka), lambda i, j: (jnp.clip(i - n_lat_blocks, 0, n_ctx_blocks - 1), 0)),
            pl.BlockSpec((tm, ks), lambda i, j: (i, 0)),
            pl.BlockSpec((None, ka + ks, tn), lambda i, j: (layer, 0, j)),
            pl.BlockSpec((tm, tn), lambda i, j: (i, j)),
            pl.BlockSpec((None, None, 1, tn), lambda i, j: (mrow(i), 3 * k + 2, 0, j)),
        ],
        out_specs=pl.BlockSpec((tm, tn), lambda i, j: (i, j)),
        out_shape=jax.ShapeDtypeStruct((rows, n), F32),
        compiler_params=_params("parallel", "parallel"),
        name="mm_mix_out",
    )(attn, attn_c, ssm, w3, res, mod4)


def _lane_group(shape, period, width):
    lane = lax.broadcasted_iota(jnp.int32, shape, 1)
    return (lane % period) // width


def _ssm_summary_kernel(x_ref, z_ref, o_ref, w_ref):
    T, H, GL, P = SSM_CHUNK, SSM_GROUP, GROUPS_PER_LANE_BLOCK, SSM_STATE
    grp = _lane_group((H, w_ref.shape[1]), GL * P, P)
    for gl in range(GL):
        keep = grp == gl
        for s in range(T):
            r0 = s * LANES + gl * H
            w_ref[r0:r0 + H, :] = jnp.where(keep, z_ref[s * H:(s + 1) * H, :], 0.0).astype(BF16)
    o_ref[...] = jnp.dot(x_ref[...], w_ref[...], preferred_element_type=F32)


def _ssm_summary(x, z):
    nj, m, kdim = x.shape
    n = z.shape[2]
    tn = n // 2
    return pl.pallas_call(
        _ssm_summary_kernel,
        grid=(nj, n // tn),
        in_specs=[
            pl.BlockSpec((None, m, kdim), lambda j, c: (j, 0, 0)),
            pl.BlockSpec((None, z.shape[1], tn), lambda j, c: (j, 0, c)),
        ],
        out_specs=pl.BlockSpec((None, m, tn), lambda j, c: (j, 0, c)),
        out_shape=jax.ShapeDtypeStruct((nj, m, n), F32),
        scratch_shapes=[pltpu.VMEM((kdim, tn), BF16)],
        compiler_params=_params("parallel", "parallel"),
        name="ssm_summary",
    )(x, z)


def _ssm_readout_kernel(h_ref, v_ref, add_ref, u_ref, d_ref, o_ref, w_ref):
    GL, P = GROUPS_PER_LANE_BLOCK, SSM_STATE
    grp = _lane_group((P, w_ref.shape[1]), LANES, SSM_GROUP)
    for gl in range(GL):
        keep = grp == gl
        for x in range(v_ref.shape[0]):
            r0 = (x * GL + gl) * P
            w_ref[r0:r0 + P, :] = jnp.where(keep, v_ref[x], 0.0).astype(BF16)
    y = add_ref[...] + jnp.dot(h_ref[...].astype(BF16), w_ref[...], preferred_element_type=F32)
    y = y + d_ref[...] * u_ref[...].astype(F32)
    o_ref[...] = jax.nn.gelu(y).astype(o_ref.dtype)


def _ssm_readout(h, v, add, u, d):
    nj, m, kdim = h.shape
    n = v.shape[3]
    tn = n // 4
    return pl.pallas_call(
        _ssm_readout_kernel,
        grid=(nj, n // tn),
        in_specs=[
            pl.BlockSpec((None, m, kdim), lambda j, c: (j, 0, 0)),
            pl.BlockSpec((None, v.shape[1], v.shape[2], tn), lambda j, c: (j, 0, 0, c)),
            pl.BlockSpec((None, m, tn), lambda j, c: (j, 0, c)),
            pl.BlockSpec((None, m, tn), lambda j, c: (j, 0, c)),
            pl.BlockSpec((None, 1, tn), lambda j, c: (j, 0, c)),
        ],
        out_specs=pl.BlockSpec((None, m, tn), lambda j, c: (j, 0, c)),
        out_shape=jax.ShapeDtypeStruct((nj, m, n), BF16),
        scratch_shapes=[pltpu.VMEM((kdim, tn), BF16)],
        compiler_params=_params("parallel", "parallel"),
        name="ssm_readout",
    )(h, v, add, u, d)


def _ssm_intra_kernel(x_ref, t_ref, o_ref, w_ref, *, n_t):
    T = SSM_CHUNK
    for half in range(T // n_t):
        @pl.when(pl.program_id(1) == half)
        def _(half=half):
            for s in range(T):
                for tt in range(n_t):
                    lag = half * n_t + tt - s + T - 1
                    w_ref[s * LANES:(s + 1) * LANES, tt * LANES:(tt + 1) * LANES] = t_ref[lag]
    o_ref[...] = jnp.dot(x_ref[...], w_ref[...], preferred_element_type=F32)


def _ssm_intra(x, tiles):
    nj, m, kdim = x.shape
    n_t = SSM_CHUNK // 2
    tn = n_t * LANES
    return pl.pallas_call(
        functools.partial(_ssm_intra_kernel, n_t=n_t),
        grid=(nj, kdim // tn),
        in_specs=[
            pl.BlockSpec((None, m, kdim), lambda j, c: (j, 0, 0)),
            pl.BlockSpec((None, 2 * SSM_CHUNK - 1, LANES, LANES), lambda j, c: (j, 0, 0, 0)),
        ],
        out_specs=pl.BlockSpec((None, m, tn), lambda j, c: (j, 0, c)),
        out_shape=jax.ShapeDtypeStruct((nj, m, kdim), F32),
        scratch_shapes=[pltpu.VMEM((kdim, tn), BF16)],
        compiler_params=_params("parallel", "arbitrary"),
        name="ssm_intra",
    )(x, tiles)


def _mix_in_kernel(x_ref, w_ref, cos_ref, sin_ref, qg_ref, kg_ref, qkv_ref, u_ref, *, nq_tiles, heads):
    j = pl.program_id(1)

    def proj():
        return jnp.dot(x_ref[...], w_ref[...], preferred_element_type=F32)

    @pl.when(j <= nq_tiles)
    def _():
        is_q = j < nq_tiles
        gain = jnp.where(is_q, qg_ref[...], kg_ref[...])
        scale = jnp.where(is_q, HEAD_DIM ** -0.5 * math.log2(math.e), 1.0)
        rows = x_ref.shape[0]
        rc = min(rows, 256)
        for r0 in range(0, rows, rc):
            p = jnp.dot(x_ref[r0:r0 + rc, :], w_ref[...], preferred_element_type=F32)
            cosf = cos_ref[r0:r0 + rc, :]
            sins = sin_ref[r0:r0 + rc, :]
            lane = lax.broadcasted_iota(jnp.int32, cosf.shape, 1)
            first_half = (lane % (HEAD_DIM // 2)) < (HEAD_DIM // 4)
            for h in range(heads):
                sl = slice(h * HEAD_DIM, (h + 1) * HEAD_DIM)
                t = p[:, sl]
                r = lax.rsqrt(jnp.mean(t * t, axis=-1, keepdims=True) + EPS)
                t = (t * r) * gain
                partner = jnp.where(first_half,
                                    pltpu.roll(t, HEAD_DIM - HEAD_DIM // 4, axis=1),
                                    pltpu.roll(t, HEAD_DIM // 4, axis=1))
                t = (t * cosf + partner * sins) * scale
                qkv_ref[r0:r0 + rc, sl] = t.astype(qkv_ref.dtype)

    @pl.when(j == nq_tiles + 1)
    def _():
        qkv_ref[...] = proj().astype(qkv_ref.dtype)

    @pl.when(j > nq_tiles + 1)
    def _():
        u_ref[...] = proj().astype(u_ref.dtype)


def _mix_in(x, w3, layer, cosf, sins, qg3, kg3, *, rows, tm, n_q, n_kv):
    kdim, n = w3.shape[1], w3.shape[2]
    tn = n_kv * HEAD_DIM
    nq_tiles = n_q // n_kv
    n_qkv_tiles = nq_tiles + 2
    n_tiles = n // tn
    return pl.pallas_call(
        functools.partial(_mix_in_kernel, nq_tiles=nq_tiles, heads=n_kv),
        grid=(rows // tm, n_tiles),
        in_specs=[
            pl.BlockSpec((tm, kdim), lambda i, j: (i, 0)),
            pl.BlockSpec((None, kdim, tn), lambda i, j: (layer, 0, j)),
            pl.BlockSpec((tm, HEAD_DIM), lambda i, j: (i, 0)),
            pl.BlockSpec((tm, HEAD_DIM), lambda i, j: (i, 0)),
            pl.BlockSpec((None, 1, HEAD_DIM), lambda i, j: (layer, 0, 0)),
            pl.BlockSpec((None, 1, HEAD_DIM), lambda i, j: (layer, 0, 0)),
        ],
        out_specs=[
            pl.BlockSpec((tm, tn), lambda i, j: (i, jnp.minimum(j, n_qkv_tiles - 1))),
            pl.BlockSpec((tm, tn), lambda i, j: (i, jnp.maximum(j - n_qkv_tiles, 0))),
        ],
        out_shape=[
            jax.ShapeDtypeStruct((rows, n_qkv_tiles * tn), BF16),
            jax.ShapeDtypeStruct((rows, n - n_qkv_tiles * tn), BF16),
        ],
        compiler_params=_params("parallel", "arbitrary"),
        name="mix_in",
    )(x, w3, cosf, sins, qg3, kg3)


def _flash_kernel(q_ref, *rest, tq, kv_lens, tkv):
    o_ref = rest[-1]
    n_src = len(kv_lens)
    k_refs, v_refs = rest[:n_src], rest[n_src:2 * n_src]
    q = jnp.concatenate([q_ref[:, h * HEAD_DIM:(h + 1) * HEAD_DIM] for h in range(GQA_GROUP)], axis=0)
    nrow = GQA_GROUP * tq

    def step(k, v, carry):
        m, acc = carry
        s = lax.dot_general(q, k, (((1,), (1,)), ((), ())), preferred_element_type=F32)
        m_new = jnp.maximum(m, jnp.max(s, axis=-1, keepdims=True))
        alpha = jnp.exp2(m - m_new)
        p = jnp.exp2(s - m_new).astype(BF16)
        v_ones = jnp.concatenate([v, jnp.ones_like(v)], axis=1)
        acc = alpha * acc + jnp.dot(p, v_ones, preferred_element_type=F32)
        return m_new, acc

    carry = (jnp.full((nrow, 1), -jnp.inf, F32), jnp.zeros((nrow, 2 * HEAD_DIM), F32))
    for k_ref, v_ref, length in zip(k_refs, v_refs, kv_lens):
        t = min(tkv, length)
        for c in range(length // t):
            carry = step(k_ref[c * t:(c + 1) * t, :], v_ref[c * t:(c + 1) * t, :], carry)
    _, acc = carry
    out = acc[:, :HEAD_DIM] / acc[:, HEAD_DIM:]
    for h in range(GQA_GROUP):
        o_ref[:, h * HEAD_DIM:(h + 1) * HEAD_DIM] = out[h * tq:(h + 1) * tq].astype(o_ref.dtype)


def _flash(qkv, *, n_q, n_kv, n_batch, q_row0, q_len, kv_srcs, tq):
    qw = GQA_GROUP * HEAD_DIM
    nq_t = q_len // tq
    assert q_row0 % tq == 0
    in_specs = [pl.BlockSpec((tq, qw), lambda b, g, i: (q_row0 // tq + b * nq_t + i, g))]
    args = [qkv]
    for col0 in (n_q, n_q + n_kv):
        for (row0, length) in kv_srcs:
            assert row0 % length == 0
            in_specs.append(pl.BlockSpec(
                (length, HEAD_DIM),
                lambda b, g, i, row0=row0, length=length, col0=col0: (row0 // length + b, col0 + g)))
            args.append(qkv)
    return pl.pallas_call(
        functools.partial(_flash_kernel, tq=tq, kv_lens=tuple(l for _, l in kv_srcs), tkv=256),
        grid=(n_batch, n_kv, nq_t),
        in_specs=in_specs,
        out_specs=pl.BlockSpec((tq, qw), lambda b, g, i: (b * nq_t + i, g)),
        out_shape=jax.ShapeDtypeStruct((n_batch * q_len, n_q * HEAD_DIM), BF16),
        compiler_params=_params("parallel", "parallel", "arbitrary"),
        name="gqa_attention",
    )(*args)


def _ssm_taps_kernel(z_ref, c_ref, o_ref):
    o_ref[...] = lax.dot_general(z_ref[...], c_ref[...], (((1,), (1,)), ((), ())),
                                 precision=lax.Precision.HIGHEST, preferred_element_type=F32)


def _ssm_taps(z, c):
    nj, nd, rows, p2 = z.shape
    return pl.pallas_call(
        _ssm_taps_kernel,
        grid=(nj, nd),
        in_specs=[
            pl.BlockSpec((None, None, rows, p2), lambda j, d: (j, d, 0, 0)),
            pl.BlockSpec((None, None, LANES, p2), lambda j, d: (j, d, 0, 0)),
        ],
        out_specs=pl.BlockSpec((None, None, rows, LANES), lambda j, d: (j, d, 0, 0)),
        out_shape=jax.ShapeDtypeStruct((nj, nd, rows, LANES), F32),
        compiler_params=_params("parallel", "parallel"),
        name="ssm_taps",
    )(z, c)


def _ssm_scan_kernel(s_ref, a_ref, o_ref, *, n_ctx_tiles, n_tiles, half):
    row = lax.broadcasted_iota(jnp.int32, (8, half), 0)
    top = row < 4

    def run(col0, a_re, a_im, tiles_in_order, second_first):
        first = jnp.logical_not(top) if second_first else top

        def dup(t):
            return jnp.where(first, t, pltpu.roll(t, 4, axis=0))

        def tile_step(ti, carry):
            hr, hi = carry
            r0 = pl.multiple_of(ti * 8, 8)
            sr = s_ref[pl.ds(r0, 8), col0:col0 + half]
            si = s_ref[pl.ds(r0, 8), col0 + half:col0 + 2 * half]
            t1r = a_re * hr - a_im * hi + sr
            t1i = a_re * hi + a_im * hr + si
            h1r, h1i = dup(t1r), dup(t1i)
            o_ref[pl.ds(r0, 8), col0:col0 + half] = jnp.where(first, hr, h1r)
            o_ref[pl.ds(r0, 8), col0 + half:col0 + 2 * half] = jnp.where(first, hi, h1i)
            t2r = a_re * h1r - a_im * h1i + sr
            t2i = a_re * h1i + a_im * h1r + si
            second = jnp.logical_not(first)
            nr = jnp.where(second, t2r, pltpu.roll(t2r, 4, axis=0))
            ni = jnp.where(second, t2i, pltpu.roll(t2i, 4, axis=0))
            return nr, ni

        carry = (jnp.zeros((8, half), F32), jnp.zeros((8, half), F32))
        for (start, count, reverse) in tiles_in_order:
            if reverse:
                carry = lax.fori_loop(0, count, lambda i, c, s=start, n=count: tile_step(s + n - 1 - i, c), carry)
            else:
                carry = lax.fori_loop(0, count, lambda i, c, s=start: tile_step(s + i, c), carry)

    n_lat = n_tiles - n_ctx_tiles
    a = a_ref[...]
    run(0, a[0:1], a[1:2], [(0, n_ctx_tiles, False), (n_ctx_tiles, n_lat, False)], False)
    run(2 * half, a[2:3], a[3:4], [(0, n_ctx_tiles, True), (n_ctx_tiles, n_lat, True)], True)


def _ssm_scan(s, a_pow, *, n_ctx_tiles):
    nj, m, w = s.shape
    half = w // 4
    return pl.pallas_call(
        functools.partial(_ssm_scan_kernel, n_ctx_tiles=n_ctx_tiles, n_tiles=m // 8, half=half),
        grid=(nj,),
        in_specs=[
            pl.BlockSpec((None, m, w), lambda j: (j, 0, 0)),
            pl.BlockSpec((None, 8, half), lambda j: (j, 0, 0)),
        ],
        out_specs=pl.BlockSpec((None, m, w), lambda j: (j, 0, 0)),
        out_shape=jax.ShapeDtypeStruct((nj, m, w), F32),
        compiler_params=_params("parallel"),
        name="ssm_scan",
    )(s, a_pow)


def _ssm_weights(lam_re, lam_im, log_dt, b_re, b_im, c_re, c_im):
    T, H, P = SSM_CHUNK, SSM_GROUP, SSM_STATE
    G = lam_re.shape[1]
    GL = GROUPS_PER_LANE_BLOCK
    J = G // GL
    lam_re, lam_im = lam_re.astype(F32), lam_im.astype(F32)
    dt = jnp.exp(log_dt.astype(F32))[..., None]
    mag = jnp.exp(lam_re * dt)
    ab_r, ab_i = mag * jnp.cos(lam_im * dt), mag * jnp.sin(lam_im * dt)
    nr, ni = ab_r - 1.0, ab_i
    den = lam_re * lam_re + lam_im * lam_im
    cr = (nr * lam_re + ni * lam_im) / den
    ci = (ni * lam_re - nr * lam_im) / den
    b_re, b_im = b_re.astype(F32), b_im.astype(F32)
    bb_r = cr[..., None] * b_re - ci[..., None] * b_im
    bb_i = cr[..., None] * b_im + ci[..., None] * b_re
    c_re, c_im = c_re.astype(F32), c_im.astype(F32)

    def powers(d, exps):
        e = jnp.asarray(exps, F32)[:, None, None]
        pmag = jnp.exp(e * (lam_re[d] * dt[d])[None])
        pang = e * (lam_im[d] * dt[d])[None]
        return pmag * jnp.cos(pang), pmag * jnp.sin(pang)

    def drive(d, exps):
        pr, pi = powers(d, exps)
        return (pr[..., None] * bb_r[d][None] - pi[..., None] * bb_i[d][None],
                pr[..., None] * bb_i[d][None] + pi[..., None] * bb_r[d][None])

    asc = list(range(T))
    desc = [T - 1 - s for s in range(T)]

    def group_mask(n_rows, n_cols, rows_per_group, cols_per_group):
        r = lax.broadcasted_iota(jnp.int32, (n_rows, n_cols), 0) // rows_per_group
        c = lax.broadcasted_iota(jnp.int32, (n_rows, n_cols), 1) // cols_per_group
        return (r == c).astype(F32)

    def tap_lhs(d, exps):
        zr, zi = drive(d, exps)
        z = jnp.concatenate([zr, zi], axis=2).reshape(T, J, GL, 2 * P, H)
        return z.transpose(1, 0, 2, 4, 3).reshape(J, T * LANES, 2 * P)
    z_all = jnp.stack([tap_lhs(0, asc), tap_lhs(1, desc)], axis=1)
    c_all = jnp.concatenate([c_re, -c_im], axis=-1).reshape(2, J, LANES, 2 * P).transpose(1, 0, 2, 3)
    raw = _ssm_taps(z_all, c_all).reshape(J, 2, T, LANES, LANES)
    fwd, bwd = raw[:, 0], raw[:, 1]
    tiles = jnp.concatenate([bwd[:, :T - 1], bwd[:, T - 1:] + fwd[:, :1], fwd[:, 1:]], axis=1)
    tiles = (tiles * group_mask(LANES, LANES, H, H)).astype(BF16)

    def summary(d, exps):
        pr, pi = powers(d, exps)
        pr, pi = (a.reshape(T, J, 1, GL * P).transpose(1, 0, 2, 3) for a in (pr, pi))
        br, bi = (a.reshape(J, GL, P, H).transpose(0, 3, 1, 2).reshape(J, 1, H, GL * P)
                  for a in (bb_r[d], bb_i[d]))
        return jnp.concatenate([pr * br - pi * bi, pr * bi + pi * br], axis=-1).reshape(J, T * H, 2 * GL * P)
    z_sum = jnp.concatenate([summary(0, desc), summary(1, asc)], axis=-1)

    def readout(d, exps):
        pr, pi = powers(d, exps)
        pr, pi = (jnp.repeat(a.reshape(T, J, GL, P).transpose(1, 3, 0, 2), H, axis=-1)
                  for a in (pr, pi))
        cr_l, ci_l = (a.reshape(J, GL, H, P).transpose(0, 3, 1, 2).reshape(J, P, 1, LANES)
                      for a in (c_re[d], c_im[d]))
        vr = (cr_l * pr - ci_l * pi).reshape(J, 1, P, T * LANES)
        vi = (cr_l * pi + ci_l * pr).reshape(J, 1, P, T * LANES)
        return jnp.concatenate([vr, -vi], axis=1)
    v_out = jnp.concatenate([readout(0, [t + 1 for t in range(T)]), readout(1, [T - t for t in range(T)])],
                            axis=1)

    def lanes(x):
        return x.reshape(J, GL * P)
    a_pow = jnp.stack([lanes(a) for d in range(2) for a in powers(d, [T])], axis=1)
    a_pow = jnp.concatenate([a_pow, jnp.zeros_like(a_pow)], axis=1)
    return tiles, z_sum, v_out, a_pow


def _s5_mixer(u, d_skip, n_batch, seq, n_ctx, weights):
    tiles, z_sum, v_out, a_pow = weights
    T = SSM_CHUNK
    J = tiles.shape[0]
    n_lat_rows = n_batch * seq
    d_chunk = jnp.tile(d_skip.astype(F32).reshape(J, 1, LANES), (1, 1, T))

    def to_chunks(rows, length):
        r = rows.reshape(n_batch, length // T, T, J, LANES).transpose(3, 1, 0, 2, 4)
        return r.reshape(J, (length // T) * n_batch, T * LANES)

    lhs = jnp.concatenate([to_chunks(u[n_lat_rows:], n_ctx), to_chunks(u[:n_lat_rows], seq)], axis=1)
    y_intra = _ssm_intra(lhs, tiles)
    summaries = _ssm_summary(lhs, z_sum)
    h_in = _ssm_scan(summaries, a_pow, n_ctx_tiles=(n_ctx // T) * n_batch // 8)
    y = _ssm_readout(h_in, v_out, y_intra, lhs, d_chunk)

    def from_chunks(part, length):
        r = part.reshape(J, length // T, n_batch, T, LANES).transpose(2, 1, 3, 0, 4)
        return r.reshape(n_batch * length, J * LANES)

    n_ctx_chunk_rows = (n_ctx // T) * n_batch
    return jnp.concatenate([from_chunks(y[:, n_ctx_chunk_rows:], seq), from_chunks(y[:, :n_ctx_chunk_rows], n_ctx)],
                           axis=0)


def _rope_tables(seq, n_batch, n_ctx_rows):
    rows_n = seq // GRID_W
    pos = jnp.arange(seq)
    rows = (pos // GRID_W).astype(F32)
    cols = (pos % GRID_W).astype(F32)
    del rows_n
    axis_dim = HEAD_DIM // 2
    inv_freq = ROPE_THETA ** (-jnp.arange(0, axis_dim, 2, dtype=F32) / axis_dim)
    ang_r, ang_c = rows[:, None] * inv_freq, cols[:, None] * inv_freq
    cosf = jnp.concatenate([jnp.cos(ang_r)] * 2 + [jnp.cos(ang_c)] * 2, axis=1)
    sins = jnp.concatenate([-jnp.sin(ang_r), jnp.sin(ang_r), -jnp.sin(ang_c), jnp.sin(ang_c)], axis=1)
    cosf = jnp.concatenate([jnp.tile(cosf, (n_batch, 1)), jnp.ones((n_ctx_rows, HEAD_DIM), F32)], axis=0)
    sins = jnp.concatenate([jnp.tile(sins, (n_batch, 1)), jnp.zeros((n_ctx_rows, HEAD_DIM), F32)], axis=0)
    return cosf, sins


def kernel(x, c, ctx, c_ctx, w_ada, b_ada, norm_ffn1, ffn1_w_in, ffn1_w_out, norm_mix, w_mix_in, q_norm, k_norm, ssm_lam_re, ssm_lam_im, ssm_log_dt, ssm_b_re, ssm_b_im, ssm_c_re, ssm_c_im, ssm_d, w_glu, b_glu, w_mix_out, norm_ffn2, ffn2_w_in, ffn2_w_out, norm_final):
    n_batch, seq, d = x.shape
    n_ctx = ctx.shape[1]
    depth = w_ada.shape[0]
    ssm_width = w_glu.shape[1]
    attn_width = w_mix_out.shape[1] - ssm_width
    n_q = attn_width // HEAD_DIM
    n_kv = n_q // GQA_GROUP
    n_lat = n_batch * seq
    n_ctx_rows = n_batch * n_ctx
    n_all = n_lat + n_ctx_rows
    tm = n_ctx_rows
    tr = _tile(tm, 256, 8)
    assert seq % tm == 0 and n_batch == 4

    h, h_tail = x.reshape(n_lat, d), ctx.reshape(n_ctx_rows, d)
    cc =jnp.concatenate([c, c_ctx[None], jnp.zeros((8 - n_batch - 1, d), F32)], axis=0)
    cosf, sins = _rope_tables(seq, n_batch, n_ctx_rows)

    def row3(a):
        return a.reshape(a.shape[0], 1, a.shape[1])

    b_ada3 = row3(b_ada)
    g_ffn1, g_mix, g_ffn2 = row3(norm_ffn1), row3(norm_mix), row3(norm_ffn2)
    qg3, kg3, bglu3 = row3(q_norm), row3(k_norm), row3(b_glu)
    ffn1_in, ffn1_out = ffn1_w_in.astype(BF16), ffn1_w_out.astype(BF16)
    ffn2_in, ffn2_out = ffn2_w_in.astype(BF16), ffn2_w_out.astype(BF16)
    mix_in, mix_out, glu = w_mix_in.astype(BF16), w_mix_out.astype(BF16), w_glu.astype(BF16)

    def ffn(h, mod4, k, gain3, w_in, w_out, layer, rows, h_tail=None):
        hn = _norm(h, gain3, layer, rows=rows, tr=tr, out_dtype=BF16, mod4=mod4, k=k, seq=seq, n_batch=n_batch,
                   x_tail=h_tail)
        hid = _mm_swiglu(hn, w_in, layer, rows=rows, tm=tm, tn=_tile(w_in.shape[2] // 2, 512))
        tk = _tile(w_out.shape[1], 4096 if h_tail is None else 2048)
        return _mm_resid(hid, w_out, layer, h, mod4, k, 0.5, rows=rows, tm=tm, tn=_tile(d, 1024),
                         tk=tk, seq=seq, n_batch=n_batch, res_tail=h_tail)

    for layer in range(depth):
        last = layer == depth - 1
        mod4 = _mod_table(cc, w_ada, b_ada3, layer).reshape(8, 9, 1, d)

        h = ffn(h, mod4, 0, g_ffn1, ffn1_in, ffn1_out, layer, n_all, h_tail=h_tail)
        h_tail = None

        hn = _norm(h, g_mix, layer, rows=n_all, tr=tr, out_dtype=BF16, mod4=mod4, k=1, seq=seq, n_batch=n_batch)
        qkv, u = _mix_in(hn, mix_in, layer, cosf, sins, qg3, kg3, rows=n_all, tm=tm, n_q=n_q, n_kv=n_kv)
        attn = _flash(qkv, n_q=n_q, n_kv=n_kv, n_batch=n_batch, q_row0=0, q_len=seq,
                      kv_srcs=[(0, seq), (n_lat, n_ctx)], tq=min(256, seq))
        out_rows = n_lat if last else n_all
        attn_c = None
        if not last:
            attn_c = _flash(qkv, n_q=n_q, n_kv=n_kv, n_batch=n_batch, q_row0=n_lat, q_len=n_ctx,
                            kv_srcs=[(n_lat, n_ctx)], tq=min(128, n_ctx))

        weights = _ssm_weights(ssm_lam_re[layer], ssm_lam_im[layer], ssm_log_dt[layer], ssm_b_re[layer],
                               ssm_b_im[layer], ssm_c_re[layer], ssm_c_im[layer])
        yg = _s5_mixer(u, ssm_d[layer], n_batch, seq, n_ctx, weights)
        ssm_out = _mm_glu(yg, glu, bglu3, layer, rows=out_rows, tm=tm, tn=_tile(ssm_width, 512))

        h = _mm_mix_out(attn, attn_c, ssm_out, mix_out, layer, h, mod4, 1, rows=out_rows, tm=tm,
                        tn=_tile(d, 512), seq=seq, n_batch=n_batch)
        h = ffn(h, mod4, 2, g_ffn2, ffn2_in, ffn2_out, layer, out_rows)

    out = _norm(h, row3(norm_final[None]), 0, rows=n_lat, tr=tr, out_dtype=F32)
    return out.reshape(n_batch, seq, d)
```

```python
import functools
import math

import jax
import jax.numpy as jnp
from jax import lax
from jax.experimental import pallas as pl
from jax.experimental.pallas import tpu as pltpu

F32 = jnp.float32
BF16 = jnp.bfloat16

HEAD_DIM = 128
GQA_GROUP = 4
GRID_W = 64
ROPE_THETA = 10000.0
SSM_GROUP = 16
SSM_STATE = 64
SSM_CHUNK = 16
LANES = 128
GROUPS_PER_LANE_BLOCK = LANES // SSM_GROUP
EPS = 1e-6
VMEM_LIMIT_BYTES = 56 * 1024 * 1024


def _params(*sem):
    return pltpu.CompilerParams(dimension_semantics=sem, vmem_limit_bytes=VMEM_LIMIT_BYTES)


def _tile(n, pref, mult=LANES):
    if n <= pref:
        return n
    t = (pref // mult) * mult
    while t > mult and n % t:
        t -= mult
    assert n % t == 0, (n, pref, mult)
    return t


def _mod_kernel(c_ref, w_ref, b_ref, o_ref):
    c = c_ref[...]
    s = (c * jax.nn.sigmoid(c)).astype(BF16)
    o_ref[...] = jnp.dot(s, w_ref[...].astype(BF16), preferred_element_type=F32) + b_ref[...]


def _mod_table(cc, w_ada, b_ada3, layer):
    rows, d = cc.shape
    n = w_ada.shape[2]
    tn = _tile(n, 512)
    return pl.pallas_call(
        _mod_kernel,
        grid=(n // tn,),
        in_specs=[
            pl.BlockSpec((rows, d), lambda j: (0, 0)),
            pl.BlockSpec((None, d, tn), lambda j: (layer, 0, j)),
            pl.BlockSpec((None, 1, tn), lambda j: (layer, 0, j)),
        ],
        out_specs=pl.BlockSpec((rows, tn), lambda j: (0, j)),
        out_shape=jax.ShapeDtypeStruct((rows, n), F32),
        compiler_params=_params("parallel"),
        name="mod_table",
    )(cc, w_ada, b_ada3)


def _norm_kernel(x_ref, *rest, modulate, n_head_blocks):
    o_ref = rest[-1]
    if n_head_blocks is not None:
        tail_ref, rest = rest[0], rest[1:]
        from_head = pl.program_id(0) < n_head_blocks
    g_ref = rest[0]
    gain = g_ref[...]
    if modulate:
        sh_ref, sc_ref = rest[1], rest[2]
        gain = gain * (1.0 + sc_ref[...])
        shift = sh_ref[...]

    rc = 16

    def chunk(i, carry):
        rows = pl.ds(pl.multiple_of(i * rc, rc), rc)
        x = x_ref[rows, :]
        if n_head_blocks is not None:
            x = jnp.where(from_head, x, tail_ref[rows, :])
        r = lax.rsqrt(jnp.mean(x * x, axis=-1, keepdims=True) + EPS)
        h = (x * r) * gain
        if modulate:
            h = h + shift
        o_ref[rows, :] = h.astype(o_ref.dtype)
        return carry

    lax.fori_loop(0, x_ref.shape[0] // rc, chunk, 0, unroll=True)


def _two_source_specs(head, tail, block, col_of):
    nh, nt = head.shape[0] // block[0], tail.shape[0] // block[0]
    return [
        pl.BlockSpec(block, lambda i, *g: (jnp.minimum(i, nh - 1), col_of(*g))),
        pl.BlockSpec(block, lambda i, *g: (jnp.clip(i - nh, 0, nt - 1), col_of(*g))),
    ], nh


def _norm(x, gain3, layer, *, rows, tr, out_dtype, mod4=None, k=None, seq=None, n_batch=None, x_tail=None):
    d = x.shape[1]
    modulate = mod4 is not None
    n_head_blocks = None
    if x_tail is None:
        in_specs, args = [pl.BlockSpec((tr, d), lambda i: (i, 0))], [x]
    else:
        in_specs, n_head_blocks = _two_source_specs(x, x_tail, (tr, d), lambda: 0)
        args = [x, x_tail]
    in_specs.append(pl.BlockSpec((None, 1, d), lambda i: (layer, 0, 0)))
    args.append(gain3)
    if modulate:
        def mrow(i):
            return jnp.minimum((i * tr) // seq, n_batch)
        in_specs += [
            pl.BlockSpec((None, None, 1, d), lambda i: (mrow(i), 3 * k, 0, 0)),
            pl.BlockSpec((None, None, 1, d), lambda i: (mrow(i), 3 * k + 1, 0, 0)),
        ]
        args += [mod4, mod4]
    return pl.pallas_call(
        functools.partial(_norm_kernel, modulate=modulate, n_head_blocks=n_head_blocks),
        grid=(rows // tr,),
        in_specs=in_specs,
        out_specs=pl.BlockSpec((tr, d), lambda i: (i, 0)),
        out_shape=jax.ShapeDtypeStruct((rows, d), out_dtype),
        compiler_params=_params("parallel"),
        name="rmsnorm_mod" if modulate else "rmsnorm",
    )(*args)


def _mm_swiglu_kernel(x_ref, wg_ref, wu_ref, o_ref):
    x = x_ref[...]
    g = jnp.dot(x, wg_ref[...], preferred_element_type=F32)
    u = jnp.dot(x, wu_ref[...], preferred_element_type=F32)
    o_ref[...] = (g * jax.nn.sigmoid(g) * u).astype(o_ref.dtype)


def _mm_swiglu(x, w3, layer, *, rows, tm, tn):
    kdim, f = w3.shape[1], w3.shape[2] // 2
    nj = f // tn
    return pl.pallas_call(
        _mm_swiglu_kernel,
        grid=(rows // tm, nj),
        in_specs=[
            pl.BlockSpec((tm, kdim), lambda i, j: (i, 0)),
            pl.BlockSpec((None, kdim, tn), lambda i, j: (layer, 0, j)),
            pl.BlockSpec((None, kdim, tn), lambda i, j: (layer, 0, j + nj)),
        ],
        out_specs=pl.BlockSpec((tm, tn), lambda i, j: (i, j)),
        out_shape=jax.ShapeDtypeStruct((rows, f), BF16),
        compiler_params=_params("parallel", "parallel"),
        name="mm_swiglu",
    )(x, w3, w3)


def _mm_resid_kernel(x_ref, w_ref, gate_ref, res_ref, *rest, nk, coef, n_head_blocks):
    o_ref = rest[-1]

    def part():
        return (coef * gate_ref[...]) * jnp.dot(x_ref[...], w_ref[...], preferred_element_type=F32)

    def residual():
        if n_head_blocks is None:
            return res_ref[...]
        return jnp.where(pl.program_id(0) < n_head_blocks, res_ref[...], rest[0][...])

    if nk == 1:
        o_ref[...] = residual() + part()
    else:
        @pl.when(pl.program_id(2) == 0)
        def _():
            o_ref[...] = residual() + part()

        @pl.when(pl.program_id(2) != 0)
        def _():
            o_ref[...] += part()


def _mm_resid(x, w3, layer, res, mod4, k, coef, *, rows, tm, tn, tk, seq, n_batch, res_tail=None):
    kdim, n = w3.shape[1], w3.shape[2]
    nk = kdim // tk

    def mrow(i):
        return jnp.minimum((i * tm) // seq, n_batch)

    n_head_blocks = None
    if res_tail is None:
        res_specs, res_args = [pl.BlockSpec((tm, tn), lambda i, j, kk: (i, j))], [res]
    else:
        res_specs, n_head_blocks = _two_source_specs(res, res_tail, (tm, tn), lambda j, kk: j)
        res_args = [res, res_tail]
    return pl.pallas_call(
        functools.partial(_mm_resid_kernel, nk=nk, coef=coef, n_head_blocks=n_head_blocks),
        grid=(rows // tm, n // tn, nk),
        in_specs=[
            pl.BlockSpec((tm, tk), lambda i, j, kk: (i, kk)),
            pl.BlockSpec((None, tk, tn), lambda i, j, kk: (layer, kk, j)),
            pl.BlockSpec((None, None, 1, tn), lambda i, j, kk: (mrow(i), 3 * k + 2, 0, j)),
        ] + res_specs,
        out_specs=pl.BlockSpec((tm, tn), lambda i, j, kk: (i, j)),
        out_shape=jax.ShapeDtypeStruct((rows, n), F32),
        compiler_params=_params("parallel", "parallel", "arbitrary"),
        name="mm_resid",
    )(x, w3, mod4, *res_args)


def _mm_glu_kernel(y_ref, w_ref, b_ref, yj_ref, o_ref):
    z = jnp.dot(y_ref[...].astype(BF16), w_ref[...], preferred_element_type=F32) + b_ref[...]
    o_ref[...] = (yj_ref[...] * jax.nn.sigmoid(z)).astype(o_ref.dtype)


def _mm_glu(y, w3, b3, layer, *, rows, tm, tn):
    kdim, n = w3.shape[1], w3.shape[2]
    return pl.pallas_call(
        _mm_glu_kernel,
        grid=(rows // tm, n // tn),
        in_specs=[
            pl.BlockSpec((tm, kdim), lambda i, j: (i, 0)),
            pl.BlockSpec((None, kdim, tn), lambda i, j: (layer, 0, j)),
            pl.BlockSpec((None, 1, tn), lambda i, j: (layer, 0, j)),
            pl.BlockSpec((tm, tn), lambda i, j: (i, j)),
        ],
        out_specs=pl.BlockSpec((tm, tn), lambda i, j: (i, j)),
        out_shape=jax.ShapeDtypeStruct((rows, n), BF16),
        compiler_params=_params("parallel", "parallel"),
        name="mm_glu",
    )(y, w3, b3, y)


def _mm_mix_out_kernel(a_ref, ac_ref, s_ref, w_ref, res_ref, gate_ref, o_ref, *, n_lat_blocks, ka):
    a = jnp.where(pl.program_id(0) < n_lat_blocks, a_ref[...], ac_ref[...])
    total = jnp.dot(a, w_ref[:ka, :], preferred_element_type=F32)
    total += jnp.dot(s_ref[...], w_ref[ka:, :], preferred_element_type=F32)
    o_ref[...] = res_ref[...] + gate_ref[...] * total


def _mm_mix_out(attn, attn_c, ssm, w3, layer, res, mod4, k, *, rows, tm, tn, seq, n_batch):
    ka, ks = attn.shape[1], ssm.shape[1]
    n = w3.shape[2]
    n_lat_blocks = attn.shape[0] // tm
    if attn_c is None:
        attn_c = attn
    n_ctx_blocks = attn_c.shape[0] // tm

    def mrow(i):
        return jnp.minimum((i * tm) // seq, n_batch)

    return pl.pallas_call(
        functools.partial(_mm_mix_out_kernel, n_lat_blocks=n_lat_blocks, ka=ka),
        grid=(rows // tm, n // tn),
        in_specs=[
            pl.BlockSpec((tm, ka), lambda i, j: (jnp.minimum(i, n_lat_blocks - 1), 0)),
            pl.BlockSpec((tm, ka), lambda i, j: (jnp.clip(i - n_lat_blocks, 0, n_ctx_blocks - 1), 0)),
            pl.BlockSpec((tm, ks), lambda i, j: (i, 0)),
            pl.BlockSpec((None, ka + ks, tn), lambda i, j: (layer, 0, j)),
            pl.BlockSpec((tm, tn), lambda i, j: (i, j)),
            pl.BlockSpec((None, None, 1, tn), lambda i, j: (mrow(i), 3 * k + 2, 0, j)),
        ],
        out_specs=pl.BlockSpec((tm, tn), lambda i, j: (i, j)),
        out_shape=jax.ShapeDtypeStruct((rows, n), F32),
        compiler_params=_params("parallel", "parallel"),
        name="mm_mix_out",
    )(attn, attn_c, ssm, w3, res, mod4)


def _lane_group(shape, period, width):
    lane = lax.broadcasted_iota(jnp.int32, shape, 1)
    return (lane % period) // width


def _ssm_chunk_ops_kernel(x_ref, t_ref, z_ref, o_ref, lhs_ref, w_ref, *, n_t):
    T, H, GL, P = SSM_CHUNK, SSM_GROUP, GROUPS_PER_LANE_BLOCK, SSM_STATE
    c = pl.program_id(1)
    m = lhs_ref.shape[0]
    n_intra = T // n_t

    @pl.when(c == 0)
    def _():
        for s in range(T):
            lhs_ref[:, s * LANES:(s + 1) * LANES] = x_ref[pl.ds(s, m, stride=T), :].astype(BF16)

    for part in range(n_intra):
        @pl.when(c == part)
        def _(part=part):
            for s in range(T):
                for tt in range(n_t):
                    lag = part * n_t + tt - s + T - 1
                    w_ref[s * LANES:(s + 1) * LANES, tt * LANES:(tt + 1) * LANES] = t_ref[lag]

    @pl.when(c >= n_intra)
    def _():
        grp = _lane_group((H, w_ref.shape[1]), GL * P, P)
        for gl in range(GL):
            keep = grp == gl
            for s in range(T):
                r0 = s * LANES + gl * H
                w_ref[r0:r0 + H, :] = jnp.where(keep, z_ref[s * H:(s + 1) * H, :], 0.0).astype(BF16)

    o_ref[...] = jnp.dot(lhs_ref[...], w_ref[...], preferred_element_type=F32)


def _ssm_chunk_ops(u_tok, tiles, z):
    T = SSM_CHUNK
    rows = u_tok.shape[0]
    nj = tiles.shape[0]
    m = rows // T
    n_t = T // 2
    tn = n_t * LANES
    n_intra = T // n_t
    assert z.shape[2] % tn == 0
    n_steps = n_intra + z.shape[2] // tn
    return pl.pallas_call(
        functools.partial(_ssm_chunk_ops_kernel, n_t=n_t),
        grid=(nj, n_steps),
        in_specs=[
            pl.BlockSpec((rows, LANES), lambda j, c: (0, j)),
            pl.BlockSpec((None, 2 * T - 1, LANES, LANES), lambda j, c: (j, 0, 0, 0)),
            pl.BlockSpec((None, z.shape[1], tn), lambda j, c: (j, 0, jnp.maximum(c - n_intra, 0))),
        ],
        out_specs=pl.BlockSpec((None, m, tn), lambda j, c: (j, 0, c)),
        out_shape=jax.ShapeDtypeStruct((nj, m, n_steps * tn), F32),
        scratch_shapes=[pltpu.VMEM((m, T * LANES), BF16), pltpu.VMEM((T * LANES, tn), BF16)],
        compiler_params=_params("parallel", "arbitrary"),
        name="ssm_chunk_ops",
    )(u_tok, tiles, z)


def _ssm_readout_kernel(h_ref, v_ref, add_ref, x_ref, d_ref, o_ref, w_ref):
    T, GL, P = SSM_CHUNK, GROUPS_PER_LANE_BLOCK, SSM_STATE

    @pl.when(pl.program_id(1) == 0)
    def _():
        grp = _lane_group((P, w_ref.shape[1]), LANES, SSM_GROUP)
        for gl in range(GL):
            keep = grp == gl
            for x in range(v_ref.shape[0]):
                r0 = (x * GL + gl) * P
                w_ref[r0:r0 + P, :] = jnp.where(keep, v_ref[x], 0.0).astype(BF16)

    y = add_ref[...] + jnp.dot(h_ref[...].astype(BF16), w_ref[...], preferred_element_type=F32)
    mq = y.shape[0]
    for t in range(T):
        tok = pl.ds(t, mq, stride=T)
        yt = y[:, t * LANES:(t + 1) * LANES] + d_ref[...] * x_ref[tok, :]
        o_ref[tok, :] = jax.nn.gelu(yt)


def _ssm_readout(h, v, merged, u_tok, d4, layer):
    T = SSM_CHUNK
    nj, m, kdim = h.shape
    n = v.shape[3]
    nq = next(q for q in (4, 3, 2, 1) if m % (8 * q) == 0)
    mq = m // nq
    return pl.pallas_call(
        _ssm_readout_kernel,
        grid=(nj, nq),
        in_specs=[
            pl.BlockSpec((None, mq, kdim), lambda j, q: (j, q, 0)),
            pl.BlockSpec((None, v.shape[1], v.shape[2], n), lambda j, q: (j, 0, 0, 0)),
            pl.BlockSpec((None, mq, n), lambda j, q: (j, q, 0)),
            pl.BlockSpec((mq * T, LANES), lambda j, q: (q, j)),
            pl.BlockSpec((None, None, 1, LANES), lambda j, q: (layer, j, 0, 0)),
        ],
        out_specs=pl.BlockSpec((mq * T, LANES), lambda j, q: (q, j)),
        out_shape=jax.ShapeDtypeStruct(u_tok.shape, F32),
        scratch_shapes=[pltpu.VMEM((kdim, n), BF16)],
        compiler_params=_params("parallel", "arbitrary"),
        name="ssm_readout",
    )(h, v, merged, u_tok, d4)


def _mix_in_kernel(x_ref, w_ref, cos_ref, sin_ref, qg_ref, kg_ref, qkv_ref, u_ref, *, nq_tiles, heads):
    j = pl.program_id(1)

    def proj():
        return jnp.dot(x_ref[...], w_ref[...], preferred_element_type=F32)

    @pl.when(j <= nq_tiles)
    def _():
        is_q = j < nq_tiles
        gain = jnp.where(is_q, qg_ref[...], kg_ref[...])
        scale = jnp.where(is_q, HEAD_DIM ** -0.5 * math.log2(math.e), 1.0)
        rows = x_ref.shape[0]
        rc = min(rows, 256)
        for r0 in range(0, rows, rc):
            p = jnp.dot(x_ref[r0:r0 + rc, :], w_ref[...], preferred_element_type=F32)
            cosf = cos_ref[r0:r0 + rc, :]
            sins = sin_ref[r0:r0 + rc, :]
            lane = lax.broadcasted_iota(jnp.int32, cosf.shape, 1)
            first_half = (lane % (HEAD_DIM // 2)) < (HEAD_DIM // 4)
            for h in range(heads):
                sl = slice(h * HEAD_DIM, (h + 1) * HEAD_DIM)
                t = p[:, sl]
                r = lax.rsqrt(jnp.mean(t * t, axis=-1, keepdims=True) + EPS)
                t = (t * r) * gain
                partner = jnp.where(first_half,
                                    pltpu.roll(t, HEAD_DIM - HEAD_DIM // 4, axis=1),
                                    pltpu.roll(t, HEAD_DIM // 4, axis=1))
                t = (t * cosf + partner * sins) * scale
                qkv_ref[r0:r0 + rc, sl] = t.astype(qkv_ref.dtype)

    @pl.when(j == nq_tiles + 1)
    def _():
        qkv_ref[...] = proj().astype(qkv_ref.dtype)

    @pl.when(j > nq_tiles + 1)
    def _():
        u_ref[...] = proj().astype(u_ref.dtype)


def _mix_in(x, w3, layer, cosf, sins, qg3, kg3, *, rows, tm, n_q, n_kv):
    kdim, n = w3.shape[1], w3.shape[2]
    tn = n_kv * HEAD_DIM
    nq_tiles = n_q // n_kv
    n_qkv_tiles = nq_tiles + 2
    n_tiles = n // tn
    return pl.pallas_call(
        functools.partial(_mix_in_kernel, nq_tiles=nq_tiles, heads=n_kv),
        grid=(rows // tm, n_tiles),
        in_specs=[
            pl.BlockSpec((tm, kdim), lambda i, j: (i, 0)),
            pl.BlockSpec((None, kdim, tn), lambda i, j: (layer, 0, j)),
            pl.BlockSpec((tm, HEAD_DIM), lambda i, j: (i, 0)),
            pl.BlockSpec((tm, HEAD_DIM), lambda i, j: (i, 0)),
            pl.BlockSpec((None, 1, HEAD_DIM), lambda i, j: (layer, 0, 0)),
            pl.BlockSpec((None, 1, HEAD_DIM), lambda i, j: (layer, 0, 0)),
        ],
        out_specs=[
            pl.BlockSpec((tm, tn), lambda i, j: (i, jnp.minimum(j, n_qkv_tiles - 1))),
            pl.BlockSpec((tm, tn), lambda i, j: (i, jnp.maximum(j - n_qkv_tiles, 0))),
        ],
        out_shape=[
            jax.ShapeDtypeStruct((rows, n_qkv_tiles * tn), BF16),
            jax.ShapeDtypeStruct((rows, n - n_qkv_tiles * tn), F32),
        ],
        compiler_params=_params("parallel", "arbitrary"),
        name="mix_in",
    )(x, w3, cosf, sins, qg3, kg3)


def _flash_kernel(q_ref, *rest, tq, kv_lens, tkv):
    o_ref = rest[-1]
    n_src = len(kv_lens)
    k_refs, v_refs = rest[:n_src], rest[n_src:2 * n_src]
    q = jnp.concatenate([q_ref[:, h * HEAD_DIM:(h + 1) * HEAD_DIM] for h in range(GQA_GROUP)], axis=0)
    nrow = GQA_GROUP * tq

    def step(k, v, carry):
        m, acc = carry
        s = lax.dot_general(q, k, (((1,), (1,)), ((), ())), preferred_element_type=F32)
        m_new = jnp.maximum(m, jnp.max(s, axis=-1, keepdims=True))
        alpha = jnp.exp2(m - m_new)
        p = jnp.exp2(s - m_new).astype(BF16)
        v_ones = jnp.concatenate([v, jnp.ones_like(v)], axis=1)
        acc = alpha * acc + jnp.dot(p, v_ones, preferred_element_type=F32)
        return m_new, acc

    carry = (jnp.full((nrow, 1), -jnp.inf, F32), jnp.zeros((nrow, 2 * HEAD_DIM), F32))
    for k_ref, v_ref, length in zip(k_refs, v_refs, kv_lens):
        t = min(tkv, length)
        for c in range(length // t):
            carry = step(k_ref[c * t:(c + 1) * t, :], v_ref[c * t:(c + 1) * t, :], carry)
    _, acc = carry
    out = acc[:, :HEAD_DIM] / acc[:, HEAD_DIM:]
    for h in range(GQA_GROUP):
        o_ref[:, h * HEAD_DIM:(h + 1) * HEAD_DIM] = out[h * tq:(h + 1) * tq].astype(o_ref.dtype)


def _flash(qkv, *, n_q, n_kv, n_batch, q_row0, q_len, kv_srcs, tq):
    qw = GQA_GROUP * HEAD_DIM
    nq_t = q_len // tq
    assert q_row0 % tq == 0
    in_specs = [pl.BlockSpec((tq, qw), lambda b, g, i: (q_row0 // tq + b * nq_t + i, g))]
    args = [qkv]
    for col0 in (n_q, n_q + n_kv):
        for (row0, length) in kv_srcs:
            assert row0 % length == 0
            in_specs.append(pl.BlockSpec(
                (length, HEAD_DIM),
                lambda b, g, i, row0=row0, length=length, col0=col0: (row0 // length + b, col0 + g)))
            args.append(qkv)
    return pl.pallas_call(
        functools.partial(_flash_kernel, tq=tq, kv_lens=tuple(l for _, l in kv_srcs), tkv=256),
        grid=(n_batch, n_kv, nq_t),
        in_specs=in_specs,
        out_specs=pl.BlockSpec((tq, qw), lambda b, g, i: (b * nq_t + i, g)),
        out_shape=jax.ShapeDtypeStruct((n_batch * q_len, n_q * HEAD_DIM), BF16),
        compiler_params=_params("parallel", "parallel", "arbitrary"),
        name="gqa_attention",
    )(*args)


def _ssm_taps_kernel(z_ref, c_ref, o_ref):
    o_ref[...] = lax.dot_general(z_ref[...], c_ref[...], (((1,), (1,)), ((), ())),
                                 precision=lax.Precision.HIGHEST, preferred_element_type=F32)


def _ssm_taps(z, c):
    nj, nd, rows, p2 = z.shape
    return pl.pallas_call(
        _ssm_taps_kernel,
        grid=(nj, nd),
        in_specs=[
            pl.BlockSpec((None, None, rows, p2), lambda j, d: (j, d, 0, 0)),
            pl.BlockSpec((None, None, LANES, p2), lambda j, d: (j, d, 0, 0)),
        ],
        out_specs=pl.BlockSpec((None, None, rows, LANES), lambda j, d: (j, d, 0, 0)),
        out_shape=jax.ShapeDtypeStruct((nj, nd, rows, LANES), F32),
        compiler_params=_params("parallel", "parallel"),
        name="ssm_taps",
    )(z, c)


def _ssm_scan_kernel(s_ref, p_ref, o_ref, *, n_batch, lat_tiles, ctx_tiles, half):
    row = lax.broadcasted_iota(jnp.int32, (8, half), 0)
    lat_rows = n_batch * lat_tiles * 8

    def run(d, col0, reverse):
        pr, pi = p_ref[2 * d], p_ref[2 * d + 1]
        ent_r, ent_i = pr[0:8], pi[0:8]
        steps = [(k, pr[8 + n:9 + n], pi[8 + n:9 + n]) for n, k in enumerate((1, 2, 4))]
        a8_r, a8_i = pr[11:12], pi[11:12]

        def shift(x, k):
            if reverse:
                return jnp.where(row < 8 - k, pltpu.roll(x, 8 - k, axis=0), 0.0)
            return jnp.where(row >= k, pltpu.roll(x, k, axis=0), 0.0)

        def tile(r0, h):
            hr, hi = h
            ir = s_ref[pl.ds(r0, 8), col0:col0 + half]
            ii = s_ref[pl.ds(r0, 8), col0 + half:col0 + 2 * half]
            for k, ar, ai in steps:
                tr, ti = shift(ir, k), shift(ii, k)
                ir, ii = ir + ar * tr - ai * ti, ii + ar * ti + ai * tr
            o_ref[pl.ds(r0, 8), col0:col0 + half] = ent_r * hr - ent_i * hi + shift(ir, 1)
            o_ref[pl.ds(r0, 8), col0 + half:col0 + 2 * half] = ent_r * hi + ent_i * hr + shift(ii, 1)
            nr = a8_r * hr - a8_i * hi + ir
            ni = a8_r * hi + a8_i * hr + ii
            last = 0 if reverse else 7
            return (jnp.broadcast_to(nr[last:last + 1], (8, half)), jnp.broadcast_to(ni[last:last + 1], (8, half)))

        def phase(base, n_tiles, hs):
            def body(i, hs):
                t = (n_tiles - 1 - i) if reverse else i
                return tuple(tile(pl.multiple_of(base + (b * n_tiles + t) * 8, 8), hs[b]) for b in range(n_batch))
            return lax.fori_loop(0, n_tiles, body, hs)

        zero = jnp.zeros((8, half), F32)
        hs = phase(lat_rows, ctx_tiles, tuple((zero, zero) for _ in range(n_batch)))
        phase(0, lat_tiles, hs)

    run(0, 0, False)
    run(1, 2 * half, True)


def _ssm_scan(merged, pw, *, n_batch, lat_tiles, ctx_tiles):
    nj, m, _ = merged.shape
    half = pw.shape[3]
    w = 4 * half
    assert merged.shape[2] == 2 * w and m == n_batch * (lat_tiles + ctx_tiles) * 8
    return pl.pallas_call(
        functools.partial(_ssm_scan_kernel, n_batch=n_batch, lat_tiles=lat_tiles, ctx_tiles=ctx_tiles, half=half),
        grid=(nj,),
        in_specs=[
            pl.BlockSpec((None, m, w), lambda j: (j, 0, 1)),
            pl.BlockSpec((None, 4, 16, half), lambda j: (j, 0, 0, 0)),
        ],
        out_specs=pl.BlockSpec((None, m, w), lambda j: (j, 0, 0)),
        out_shape=jax.ShapeDtypeStruct((nj, m, w), F32),
        compiler_params=_params("parallel"),
        name="ssm_scan",
    )(merged, pw)


def _ssm_weights(lam_re, lam_im, log_dt, b_re, b_im, c_re, c_im):
    T, H, P = SSM_CHUNK, SSM_GROUP, SSM_STATE
    G = lam_re.shape[1]
    GL = GROUPS_PER_LANE_BLOCK
    J = G // GL
    lam_re, lam_im = lam_re.astype(F32), lam_im.astype(F32)
    dt = jnp.exp(log_dt.astype(F32))[..., None]
    mag = jnp.exp(lam_re * dt)
    ab_r, ab_i = mag * jnp.cos(lam_im * dt), mag * jnp.sin(lam_im * dt)
    nr, ni = ab_r - 1.0, ab_i
    den = lam_re * lam_re + lam_im * lam_im
    cr = (nr * lam_re + ni * lam_im) / den
    ci = (ni * lam_re - nr * lam_im) / den
    b_re, b_im = b_re.astype(F32), b_im.astype(F32)
    bb_r = cr[..., None] * b_re - ci[..., None] * b_im
    bb_i = cr[..., None] * b_im + ci[..., None] * b_re
    c_re, c_im = c_re.astype(F32), c_im.astype(F32)

    def powers(d, exps):
        e = jnp.asarray(exps, F32)[:, None, None]
        pmag = jnp.exp(e * (lam_re[d] * dt[d])[None])
        pang = e * (lam_im[d] * dt[d])[None]
        return pmag * jnp.cos(pang), pmag * jnp.sin(pang)

    def drive(d, exps):
        pr, pi = powers(d, exps)
        return (pr[..., None] * bb_r[d][None] - pi[..., None] * bb_i[d][None],
                pr[..., None] * bb_i[d][None] + pi[..., None] * bb_r[d][None])

    asc = list(range(T))
    desc = [T - 1 - s for s in range(T)]

    def group_mask(n_rows, n_cols, rows_per_group, cols_per_group):
        r = lax.broadcasted_iota(jnp.int32, (n_rows, n_cols), 0) // rows_per_group
        c = lax.broadcasted_iota(jnp.int32, (n_rows, n_cols), 1) // cols_per_group
        return (r == c).astype(F32)

    def tap_lhs(d, exps):
        zr, zi = drive(d, exps)
        z = jnp.concatenate([zr, zi], axis=2).reshape(T, J, GL, 2 * P, H)
        return z.transpose(1, 0, 2, 4, 3).reshape(J, T * LANES, 2 * P)
    z_all = jnp.stack([tap_lhs(0, asc), tap_lhs(1, desc)], axis=1)
    c_all = jnp.concatenate([c_re, -c_im], axis=-1).reshape(2, J, LANES, 2 * P).transpose(1, 0, 2, 3)
    raw = _ssm_taps(z_all, c_all).reshape(J, 2, T, LANES, LANES)
    fwd, bwd = raw[:, 0], raw[:, 1]
    tiles = jnp.concatenate([bwd[:, :T - 1], bwd[:, T - 1:] + fwd[:, :1], fwd[:, 1:]], axis=1)
    tiles = (tiles * group_mask(LANES, LANES, H, H)).astype(BF16)

    def summary(d, exps):
        pr, pi = powers(d, exps)
        pr, pi = (a.reshape(T, J, 1, GL * P).transpose(1, 0, 2, 3) for a in (pr, pi))
        br, bi = (a.reshape(J, GL, P, H).transpose(0, 3, 1, 2).reshape(J, 1, H, GL * P)
                  for a in (bb_r[d], bb_i[d]))
        return jnp.concatenate([pr * br - pi * bi, pr * bi + pi * br], axis=-1).reshape(J, T * H, 2 * GL * P)
    z_sum = jnp.concatenate([summary(0, desc), summary(1, asc)], axis=-1)

    def readout(d, exps):
        pr, pi = powers(d, exps)
        pr, pi = (jnp.repeat(a.reshape(T, J, GL, P).transpose(1, 3, 0, 2), H, axis=-1)
                  for a in (pr, pi))
        cr_l, ci_l = (a.reshape(J, GL, H, P).transpose(0, 3, 1, 2).reshape(J, P, 1, LANES)
                      for a in (c_re[d], c_im[d]))
        vr = (cr_l * pr - ci_l * pi).reshape(J, 1, P, T * LANES)
        vi = (cr_l * pi + ci_l * pr).reshape(J, 1, P, T * LANES)
        return jnp.concatenate([vr, -vi], axis=1)
    v_out = jnp.concatenate([readout(0, [t + 1 for t in range(T)]), readout(1, [T - t for t in range(T)])],
                            axis=1)

    def scan_powers(d, entry):
        exps = [T * k for k in entry] + [T, 2 * T, 4 * T, 8 * T] + [0] * 4
        return jnp.stack([a.reshape(16, J, GL * P).transpose(1, 0, 2) for a in powers(d, exps)], axis=1)
    pw = jnp.concatenate([scan_powers(0, list(range(8))), scan_powers(1, [7 - r for r in range(8)])], axis=1)
    return tiles, z_sum, v_out, pw


def _s5_mixer(u, d4, layer, n_batch, seq, n_ctx, weights):
    tiles, z_sum, v_out, pw = weights
    T = SSM_CHUNK
    merged = _ssm_chunk_ops(u, tiles, z_sum)
    h_in = _ssm_scan(merged, pw, n_batch=n_batch, lat_tiles=seq // (8 * T), ctx_tiles=n_ctx // (8 * T))
    return _ssm_readout(h_in, v_out, merged, u, d4, layer)


def _rope_tables(seq, n_batch, n_ctx_rows):
    rows_n = seq // GRID_W
    pos = jnp.arange(seq)
    rows = (pos // GRID_W).astype(F32)
    cols = (pos % GRID_W).astype(F32)
    del rows_n
    axis_dim = HEAD_DIM // 2
    inv_freq = ROPE_THETA ** (-jnp.arange(0, axis_dim, 2, dtype=F32) / axis_dim)
    ang_r, ang_c = rows[:, None] * inv_freq, cols[:, None] * inv_freq
    cosf = jnp.concatenate([jnp.cos(ang_r)] * 2 + [jnp.cos(ang_c)] * 2, axis=1)
    sins = jnp.concatenate([-jnp.sin(ang_r), jnp.sin(ang_r), -jnp.sin(ang_c), jnp.sin(ang_c)], axis=1)
    cosf = jnp.concatenate([jnp.tile(cosf, (n_batch, 1)), jnp.ones((n_ctx_rows, HEAD_DIM), F32)], axis=0)
    sins = jnp.concatenate([jnp.tile(sins, (n_batch, 1)), jnp.zeros((n_ctx_rows, HEAD_DIM), F32)], axis=0)
    return cosf, sins


def kernel(x, c, ctx, c_ctx, w_ada, b_ada, norm_ffn1, ffn1_w_in, ffn1_w_out, norm_mix, w_mix_in, q_norm, k_norm, ssm_lam_re, ssm_lam_im, ssm_log_dt, ssm_b_re, ssm_b_im, ssm_c_re, ssm_c_im, ssm_d, w_glu, b_glu, w_mix_out, norm_ffn2, ffn2_w_in, ffn2_w_out, norm_final):
    n_batch, seq, d = x.shape
    n_ctx = ctx.shape[1]
    depth = w_ada.shape[0]
    ssm_width = w_glu.shape[1]
    attn_width = w_mix_out.shape[1] - ssm_width
    n_q = attn_width // HEAD_DIM
    n_kv = n_q // GQA_GROUP
    n_lat = n_batch * seq
    n_ctx_rows = n_batch * n_ctx
    n_all = n_lat + n_ctx_rows
    tm = n_ctx_rows
    tr = _tile(tm, 256, 8)
    assert seq % tm == 0 and n_batch == 4

    h, h_tail = x.reshape(n_lat, d), ctx.reshape(n_ctx_rows, d)
    cc =jnp.concatenate([c, c_ctx[None], jnp.zeros((8 - n_batch - 1, d), F32)], axis=0)
    cosf, sins = _rope_tables(seq, n_batch, n_ctx_rows)

    def row3(a):
        return a.reshape(a.shape[0], 1, a.shape[1])

    b_ada3 = row3(b_ada)
    g_ffn1, g_mix, g_ffn2 = row3(norm_ffn1), row3(norm_mix), row3(norm_ffn2)
    qg3, kg3, bglu3 = row3(q_norm), row3(k_norm), row3(b_glu)
    d4 = ssm_d.astype(F32).reshape(depth, ssm_width // LANES, 1, LANES)
    ffn1_in, ffn1_out = ffn1_w_in.astype(BF16), ffn1_w_out.astype(BF16)
    ffn2_in, ffn2_out = ffn2_w_in.astype(BF16), ffn2_w_out.astype(BF16)
    mix_in, mix_out, glu = w_mix_in.astype(BF16), w_mix_out.astype(BF16), w_glu.astype(BF16)

    def ffn(h, mod4, k, gain3, w_in, w_out, layer, rows, h_tail=None):
        hn = _norm(h, gain3, layer, rows=rows, tr=tr, out_dtype=BF16, mod4=mod4, k=k, seq=seq, n_batch=n_batch,
                   x_tail=h_tail)
        hid = _mm_swiglu(hn, w_in, layer, rows=rows, tm=tm, tn=_tile(w_in.shape[2] // 2, 512))
        tk = _tile(w_out.shape[1], 4096 if h_tail is None else 2048)
        return _mm_resid(hid, w_out, layer, h, mod4, k, 0.5, rows=rows, tm=tm, tn=_tile(d, 1024),
                         tk=tk, seq=seq, n_batch=n_batch, res_tail=h_tail)

    for layer in range(depth):
        last = layer == depth - 1
        mod4 = _mod_table(cc, w_ada, b_ada3, layer).reshape(8, 9, 1, d)

        h = ffn(h, mod4, 0, g_ffn1, ffn1_in, ffn1_out, layer, n_all, h_tail=h_tail)
        h_tail = None

        hn = _norm(h, g_mix, layer, rows=n_all, tr=tr, out_dtype=BF16, mod4=mod4, k=1, seq=seq, n_batch=n_batch)
        qkv, u = _mix_in(hn, mix_in, layer, cosf, sins, qg3, kg3, rows=n_all, tm=tm, n_q=n_q, n_kv=n_kv)
        attn = _flash(qkv, n_q=n_q, n_kv=n_kv, n_batch=n_batch, q_row0=0, q_len=seq,
                      kv_srcs=[(0, seq), (n_lat, n_ctx)], tq=min(256, seq))
        out_rows = n_lat if last else n_all
        attn_c = None
        if not last:
            attn_c = _flash(qkv, n_q=n_q, n_kv=n_kv, n_batch=n_batch, q_row0=n_lat, q_len=n_ctx,
                            kv_srcs=[(n_lat, n_ctx)], tq=min(128, n_ctx))

        weights = _ssm_weights(ssm_lam_re[layer], ssm_lam_im[layer], ssm_log_dt[layer], ssm_b_re[layer],
                               ssm_b_im[layer], ssm_c_re[layer], ssm_c_im[layer])
        yg = _s5_mixer(u, d4, layer, n_batch, seq, n_ctx, weights)
        ssm_out = _mm_glu(yg, glu, bglu3, layer, rows=out_rows, tm=tm, tn=_tile(ssm_width, 512))

        h = _mm_mix_out(attn, attn_c, ssm_out, mix_out, layer, h, mod4, 1, rows=out_rows, tm=tm,
                        tn=_tile(d, 512), seq=seq, n_batch=n_batch)
        h = ffn(h, mod4, 2, g_ffn2, ffn2_in, ffn2_out, layer, out_rows)

    out = _norm(h, row3(norm_final[None]), 0, rows=n_lat, tr=tr, out_dtype=F32)
    return out.reshape(n_batch, seq, d)
```

```python
import functools
import math

import jax
import jax.numpy as jnp
from jax import lax
from jax.experimental import pallas as pl
from jax.experimental.pallas import tpu as pltpu

F32 = jnp.float32
BF16 = jnp.bfloat16

HEAD_DIM = 128
GQA_GROUP = 4
GRID_W = 64
ROPE_THETA = 10000.0
SSM_GROUP = 16
SSM_STATE = 64
SSM_CHUNK = 16
LANES = 128
GROUPS_PER_LANE_BLOCK = LANES // SSM_GROUP
EPS = 1e-6
VMEM_LIMIT_BYTES = 56 * 1024 * 1024


def _params(*sem):
    return pltpu.CompilerParams(dimension_semantics=sem, vmem_limit_bytes=VMEM_LIMIT_BYTES)


def _tile(n, pref, mult=LANES):
    if n <= pref:
        return n
    t = (pref // mult) * mult
    while t > mult and n % t:
        t -= mult
    assert n % t == 0, (n, pref, mult)
    return t


def _mod_kernel(c_ref, w_ref, b_ref, o_ref):
    c = c_ref[...]
    s = (c * jax.nn.sigmoid(c)).astype(BF16)
    o_ref[...] = jnp.dot(s, w_ref[...].astype(BF16), preferred_element_type=F32) + b_ref[...]


def _mod_table(cc, w_ada, b_ada3, layer):
    rows, d = cc.shape
    n = w_ada.shape[2]
    tn = _tile(n, 512)
    return pl.pallas_call(
        _mod_kernel,
        grid=(n // tn,),
        in_specs=[
            pl.BlockSpec((rows, d), lambda j: (0, 0)),
            pl.BlockSpec((None, d, tn), lambda j: (layer, 0, j)),
            pl.BlockSpec((None, 1, tn), lambda j: (layer, 0, j)),
        ],
        out_specs=pl.BlockSpec((rows, tn), lambda j: (0, j)),
        out_shape=jax.ShapeDtypeStruct((rows, n), F32),
        compiler_params=_params("parallel"),
        name="mod_table",
    )(cc, w_ada, b_ada3)


def _norm_kernel(x_ref, *rest, modulate, n_head_blocks):
    o_ref = rest[-1]
    if n_head_blocks is not None:
        tail_ref, rest = rest[0], rest[1:]
        from_head = pl.program_id(0) < n_head_blocks
    g_ref = rest[0]
    gain = g_ref[...]
    if modulate:
        sh_ref, sc_ref = rest[1], rest[2]
        gain = gain * (1.0 + sc_ref[...])
        shift = sh_ref[...]

    rc = 16

    def chunk(i, carry):
        rows = pl.ds(pl.multiple_of(i * rc, rc), rc)
        x = x_ref[rows, :]
        if n_head_blocks is not None:
            x = jnp.where(from_head, x, tail_ref[rows, :])
        r = lax.rsqrt(jnp.mean(x * x, axis=-1, keepdims=True) + EPS)
        h = (x * r) * gain
        if modulate:
            h = h + shift
        o_ref[rows, :] = h.astype(o_ref.dtype)
        return carry

    lax.fori_loop(0, x_ref.shape[0] // rc, chunk, 0, unroll=True)


def _two_source_specs(head, tail, block, col_of):
    nh, nt = head.shape[0] // block[0], tail.shape[0] // block[0]
    return [
        pl.BlockSpec(block, lambda i, *g: (jnp.minimum(i, nh - 1), col_of(*g))),
        pl.BlockSpec(block, lambda i, *g: (jnp.clip(i - nh, 0, nt - 1), col_of(*g))),
    ], nh


def _norm(x, gain3, layer, *, rows, tr, out_dtype, mod4=None, k=None, seq=None, n_batch=None, x_tail=None):
    d = x.shape[1]
    modulate = mod4 is not None
    n_head_blocks = None
    if x_tail is None:
        in_specs, args = [pl.BlockSpec((tr, d), lambda i: (i, 0))], [x]
    else:
        in_specs, n_head_blocks = _two_source_specs(x, x_tail, (tr, d), lambda: 0)
        args = [x, x_tail]
    in_specs.append(pl.BlockSpec((None, 1, d), lambda i: (layer, 0, 0)))
    args.append(gain3)
    if modulate:
        def mrow(i):
            return jnp.minimum((i * tr) // seq, n_batch)
        in_specs += [
            pl.BlockSpec((None, None, 1, d), lambda i: (mrow(i), 3 * k, 0, 0)),
            pl.BlockSpec((None, None, 1, d), lambda i: (mrow(i), 3 * k + 1, 0, 0)),
        ]
        args += [mod4, mod4]
    return pl.pallas_call(
        functools.partial(_norm_kernel, modulate=modulate, n_head_blocks=n_head_blocks),
        grid=(rows // tr,),
        in_specs=in_specs,
        out_specs=pl.BlockSpec((tr, d), lambda i: (i, 0)),
        out_shape=jax.ShapeDtypeStruct((rows, d), out_dtype),
        compiler_params=_params("parallel"),
        name="rmsnorm_mod" if modulate else "rmsnorm",
    )(*args)


def _mm_swiglu_kernel(x_ref, wg_ref, wu_ref, o_ref):
    x = x_ref[...]
    g = jnp.dot(x, wg_ref[...], preferred_element_type=F32)
    u = jnp.dot(x, wu_ref[...], preferred_element_type=F32)
    o_ref[...] = (g * jax.nn.sigmoid(g) * u).astype(o_ref.dtype)


def _mm_swiglu(x, w3, layer, *, rows, tm, tn):
    kdim, f = w3.shape[1], w3.shape[2] // 2
    nj = f // tn
    return pl.pallas_call(
        _mm_swiglu_kernel,
        grid=(rows // tm, nj),
        in_specs=[
            pl.BlockSpec((tm, kdim), lambda i, j: (i, 0)),
            pl.BlockSpec((None, kdim, tn), lambda i, j: (layer, 0, j)),
            pl.BlockSpec((None, kdim, tn), lambda i, j: (layer, 0, j + nj)),
        ],
        out_specs=pl.BlockSpec((tm, tn), lambda i, j: (i, j)),
        out_shape=jax.ShapeDtypeStruct((rows, f), BF16),
        compiler_params=_params("parallel", "parallel"),
        name="mm_swiglu",
    )(x, w3, w3)


def _mm_resid_kernel(x_ref, w_ref, gate_ref, res_ref, *rest, nk, coef, n_head_blocks):
    o_ref = rest[-1]

    def part():
        return (coef * gate_ref[...]) * jnp.dot(x_ref[...], w_ref[...], preferred_element_type=F32)

    def residual():
        if n_head_blocks is None:
            return res_ref[...]
        return jnp.where(pl.program_id(0) < n_head_blocks, res_ref[...], rest[0][...])

    if nk == 1:
        o_ref[...] = residual() + part()
    else:
        @pl.when(pl.program_id(2) == 0)
        def _():
            o_ref[...] = residual() + part()

        @pl.when(pl.program_id(2) != 0)
        def _():
            o_ref[...] += part()


def _mm_resid(x, w3, layer, res, mod4, k, coef, *, rows, tm, tn, tk, seq, n_batch, res_tail=None):
    kdim, n = w3.shape[1], w3.shape[2]
    nk = kdim // tk

    def mrow(i):
        return jnp.minimum((i * tm) // seq, n_batch)

    n_head_blocks = None
    if res_tail is None:
        res_specs, res_args = [pl.BlockSpec((tm, tn), lambda i, j, kk: (i, j))], [res]
    else:
        res_specs, n_head_blocks = _two_source_specs(res, res_tail, (tm, tn), lambda j, kk: j)
        res_args = [res, res_tail]
    return pl.pallas_call(
        functools.partial(_mm_resid_kernel, nk=nk, coef=coef, n_head_blocks=n_head_blocks),
        grid=(rows // tm, n // tn, nk),
        in_specs=[
            pl.BlockSpec((tm, tk), lambda i, j, kk: (i, kk)),
            pl.BlockSpec((None, tk, tn), lambda i, j, kk: (layer, kk, j)),
            pl.BlockSpec((None, None, 1, tn), lambda i, j, kk: (mrow(i), 3 * k + 2, 0, j)),
        ] + res_specs,
        out_specs=pl.BlockSpec((tm, tn), lambda i, j, kk: (i, j)),
        out_shape=jax.ShapeDtypeStruct((rows, n), F32),
        compiler_params=_params("parallel", "parallel", "arbitrary"),
        name="mm_resid",
    )(x, w3, mod4, *res_args)


def _mm_glu_kernel(y_ref, w_ref, b_ref, yj_ref, o_ref):
    z = jnp.dot(y_ref[...].astype(BF16), w_ref[...], preferred_element_type=F32) + b_ref[...]
    o_ref[...] = (yj_ref[...] * jax.nn.sigmoid(z)).astype(o_ref.dtype)


def _mm_glu(y, w3, b3, layer, *, rows, tm, tn):
    kdim, n = w3.shape[1], w3.shape[2]
    return pl.pallas_call(
        _mm_glu_kernel,
        grid=(rows // tm, n // tn),
        in_specs=[
            pl.BlockSpec((tm, kdim), lambda i, j: (i, 0)),
            pl.BlockSpec((None, kdim, tn), lambda i, j: (layer, 0, j)),
            pl.BlockSpec((None, 1, tn), lambda i, j: (layer, 0, j)),
            pl.BlockSpec((tm, tn), lambda i, j: (i, j)),
        ],
        out_specs=pl.BlockSpec((tm, tn), lambda i, j: (i, j)),
        out_shape=jax.ShapeDtypeStruct((rows, n), BF16),
        compiler_params=_params("parallel", "parallel"),
        name="mm_glu",
    )(y, w3, b3, y)


def _mm_mix_out_kernel(a_ref, ac_ref, s_ref, w_ref, res_ref, gate_ref, o_ref, *, n_lat_blocks, ka):
    a = jnp.where(pl.program_id(0) < n_lat_blocks, a_ref[...], ac_ref[...])
    total = jnp.dot(a, w_ref[:ka, :], preferred_element_type=F32)
    total += jnp.dot(s_ref[...], w_ref[ka:, :], preferred_element_type=F32)
    o_ref[...] = res_ref[...] + gate_ref[...] * total


def _mm_mix_out(attn, attn_c, ssm, w3, layer, res, mod4, k, *, rows, tm, tn, seq, n_batch):
    ka, ks = attn.shape[1], ssm.shape[1]
    n = w3.shape[2]
    n_lat_blocks = attn.shape[0] // tm
    if attn_c is None:
        attn_c = attn
    n_ctx_blocks = attn_c.shape[0] // tm

    def mrow(i):
        return jnp.minimum((i * tm) // seq, n_batch)

    return pl.pallas_call(
        functools.partial(_mm_mix_out_kernel, n_lat_blocks=n_lat_blocks, ka=ka),
        grid=(rows // tm, n // tn),
        in_specs=[
            pl.BlockSpec((tm, ka), lambda i, j: (jnp.minimum(i, n_lat_blocks - 1), 0)),
            pl.BlockSpec((tm, ka), lambda i, j: (jnp.clip(i - n_lat_blocks, 0, n_ctx_blocks - 1), 0)),
            pl.BlockSpec((tm, ks), lambda i, j: (i, 0)),
            pl.BlockSpec((None, ka + ks, tn), lambda i, j: (layer, 0, j)),
            pl.BlockSpec((tm, tn), lambda i, j: (i, j)),
            pl.BlockSpec((None, None, 1, tn), lambda i, j: (mrow(i), 3 * k + 2, 0, j)),
        ],
        out_specs=pl.BlockSpec((tm, tn), lambda i, j: (i, j)),
        out_shape=jax.ShapeDtypeStruct((rows, n), F32),
        compiler_params=_params("parallel", "parallel"),
        name="mm_mix_out",
    )(attn, attn_c, ssm, w3, res, mod4)


def _lane_group(shape, period, width):
    lane = lax.broadcasted_iota(jnp.int32, shape, 1)
    return (lane % period) // width


def _ssm_chunk_ops_kernel(x_ref, t_ref, z_ref, o_ref, lhs_ref, w_ref, *, n_t):
    T, H, GL, P = SSM_CHUNK, SSM_GROUP, GROUPS_PER_LANE_BLOCK, SSM_STATE
    c = pl.program_id(1)
    m = lhs_ref.shape[0]
    n_intra = T // n_t

    @pl.when(c == 0)
    def _():
        for s in range(T):
            lhs_ref[:, s * LANES:(s + 1) * LANES] = x_ref[pl.ds(s, m, stride=T), :].astype(BF16)

    for part in range(n_intra):
        @pl.when(c == part)
        def _(part=part):
            for s in range(T):
                for tt in range(n_t):
                    lag = part * n_t + tt - s + T - 1
                    w_ref[s * LANES:(s + 1) * LANES, tt * LANES:(tt + 1) * LANES] = t_ref[lag]

    @pl.when(c >= n_intra)
    def _():
        grp = _lane_group((H, w_ref.shape[1]), GL * P, P)
        for gl in range(GL):
            keep = grp == gl
            for s in range(T):
                r0 = s * LANES + gl * H
                w_ref[r0:r0 + H, :] = jnp.where(keep, z_ref[s * H:(s + 1) * H, :], 0.0).astype(BF16)

    o_ref[...] = jnp.dot(lhs_ref[...], w_ref[...], preferred_element_type=F32)


def _ssm_chunk_ops(u_tok, tiles, z, layer):
    T = SSM_CHUNK
    rows = u_tok.shape[0]
    nj = tiles.shape[1]
    m = rows // T
    n_t = T // 2
    tn = n_t * LANES
    n_intra = T // n_t
    assert z.shape[3] % tn == 0
    n_steps = n_intra + z.shape[3] // tn
    return pl.pallas_call(
        functools.partial(_ssm_chunk_ops_kernel, n_t=n_t),
        grid=(nj, n_steps),
        in_specs=[
            pl.BlockSpec((rows, LANES), lambda j, c: (0, j)),
            pl.BlockSpec((None, None, 2 * T - 1, LANES, LANES), lambda j, c: (layer, j, 0, 0, 0)),
            pl.BlockSpec((None, None, z.shape[2], tn), lambda j, c: (layer, j, 0, jnp.maximum(c - n_intra, 0))),
        ],
        out_specs=pl.BlockSpec((None, m, tn), lambda j, c: (j, 0, c)),
        out_shape=jax.ShapeDtypeStruct((nj, m, n_steps * tn), F32),
        scratch_shapes=[pltpu.VMEM((m, T * LANES), BF16), pltpu.VMEM((T * LANES, tn), BF16)],
        compiler_params=_params("parallel", "arbitrary"),
        name="ssm_chunk_ops",
    )(u_tok, tiles, z)


def _ssm_readout_kernel(h_ref, v_ref, add_ref, x_ref, d_ref, o_ref, w_ref):
    T, GL, P = SSM_CHUNK, GROUPS_PER_LANE_BLOCK, SSM_STATE

    @pl.when(pl.program_id(1) == 0)
    def _():
        grp = _lane_group((P, w_ref.shape[1]), LANES, SSM_GROUP)
        for gl in range(GL):
            keep = grp == gl
            for x in range(v_ref.shape[0]):
                r0 = (x * GL + gl) * P
                w_ref[r0:r0 + P, :] = jnp.where(keep, v_ref[x], 0.0).astype(BF16)

    y = add_ref[...] + jnp.dot(h_ref[...].astype(BF16), w_ref[...], preferred_element_type=F32)
    mq = y.shape[0]
    for t in range(T):
        tok = pl.ds(t, mq, stride=T)
        yt = y[:, t * LANES:(t + 1) * LANES] + d_ref[...] * x_ref[tok, :]
        o_ref[tok, :] = jax.nn.gelu(yt)


def _ssm_readout(h, v, merged, u_tok, d4, layer):
    T = SSM_CHUNK
    nj, m, kdim = h.shape
    n = v.shape[4]
    nq = next(q for q in (4, 3, 2, 1) if m % (8 * q) == 0)
    mq = m // nq
    return pl.pallas_call(
        _ssm_readout_kernel,
        grid=(nj, nq),
        in_specs=[
            pl.BlockSpec((None, mq, kdim), lambda j, q: (j, q, 0)),
            pl.BlockSpec((None, None, v.shape[2], v.shape[3], n), lambda j, q: (layer, j, 0, 0, 0)),
            pl.BlockSpec((None, mq, n), lambda j, q: (j, q, 0)),
            pl.BlockSpec((mq * T, LANES), lambda j, q: (q, j)),
            pl.BlockSpec((None, None, 1, LANES), lambda j, q: (layer, j, 0, 0)),
        ],
        out_specs=pl.BlockSpec((mq * T, LANES), lambda j, q: (q, j)),
        out_shape=jax.ShapeDtypeStruct(u_tok.shape, F32),
        scratch_shapes=[pltpu.VMEM((kdim, n), BF16)],
        compiler_params=_params("parallel", "arbitrary"),
        name="ssm_readout",
    )(h, v, merged, u_tok, d4)


def _mix_in_kernel(x_ref, w_ref, cos_ref, sin_ref, qg_ref, kg_ref, qkv_ref, u_ref, *, nq_tiles, heads):
    j = pl.program_id(1)

    def proj():
        return jnp.dot(x_ref[...], w_ref[...], preferred_element_type=F32)

    @pl.when(j <= nq_tiles)
    def _():
        is_q = j < nq_tiles
        gain = jnp.where(is_q, qg_ref[...], kg_ref[...])
        scale = jnp.where(is_q, HEAD_DIM ** -0.5 * math.log2(math.e), 1.0)
        rows = x_ref.shape[0]
        rc = min(rows, 256)
        for r0 in range(0, rows, rc):
            p = jnp.dot(x_ref[r0:r0 + rc, :], w_ref[...], preferred_element_type=F32)
            cosf = cos_ref[r0:r0 + rc, :]
            sins = sin_ref[r0:r0 + rc, :]
            lane = lax.broadcasted_iota(jnp.int32, cosf.shape, 1)
            first_half = (lane % (HEAD_DIM // 2)) < (HEAD_DIM // 4)
            for h in range(heads):
                sl = slice(h * HEAD_DIM, (h + 1) * HEAD_DIM)
                t = p[:, sl]
                r = lax.rsqrt(jnp.mean(t * t, axis=-1, keepdims=True) + EPS)
                t = (t * r) * gain
                partner = jnp.where(first_half,
                                    pltpu.roll(t, HEAD_DIM - HEAD_DIM // 4, axis=1),
                                    pltpu.roll(t, HEAD_DIM // 4, axis=1))
                t = (t * cosf + partner * sins) * scale
                qkv_ref[r0:r0 + rc, sl] = t.astype(qkv_ref.dtype)

    @pl.when(j == nq_tiles + 1)
    def _():
        qkv_ref[...] = proj().astype(qkv_ref.dtype)

    @pl.when(j > nq_tiles + 1)
    def _():
        u_ref[...] = proj().astype(u_ref.dtype)


def _mix_in(x, w3, layer, cosf, sins, qg3, kg3, *, rows, tm, n_q, n_kv):
    kdim, n = w3.shape[1], w3.shape[2]
    tn = n_kv * HEAD_DIM
    nq_tiles = n_q // n_kv
    n_qkv_tiles = nq_tiles + 2
    n_tiles = n // tn
    return pl.pallas_call(
        functools.partial(_mix_in_kernel, nq_tiles=nq_tiles, heads=n_kv),
        grid=(rows // tm, n_tiles),
        in_specs=[
            pl.BlockSpec((tm, kdim), lambda i, j: (i, 0)),
            pl.BlockSpec((None, kdim, tn), lambda i, j: (layer, 0, j)),
            pl.BlockSpec((tm, HEAD_DIM), lambda i, j: (i, 0)),
            pl.BlockSpec((tm, HEAD_DIM), lambda i, j: (i, 0)),
            pl.BlockSpec((None, 1, HEAD_DIM), lambda i, j: (layer, 0, 0)),
            pl.BlockSpec((None, 1, HEAD_DIM), lambda i, j: (layer, 0, 0)),
        ],
        out_specs=[
            pl.BlockSpec((tm, tn), lambda i, j: (i, jnp.minimum(j, n_qkv_tiles - 1))),
            pl.BlockSpec((tm, tn), lambda i, j: (i, jnp.maximum(j - n_qkv_tiles, 0))),
        ],
        out_shape=[
            jax.ShapeDtypeStruct((rows, n_qkv_tiles * tn), BF16),
            jax.ShapeDtypeStruct((rows, n - n_qkv_tiles * tn), F32),
        ],
        compiler_params=_params("parallel", "arbitrary"),
        name="mix_in",
    )(x, w3, cosf, sins, qg3, kg3)


def _flash_kernel(q_ref, *rest, tq, kv_lens, tkv):
    o_ref = rest[-1]
    n_src = len(kv_lens)
    k_refs, v_refs = rest[:n_src], rest[n_src:2 * n_src]
    q = jnp.concatenate([q_ref[:, h * HEAD_DIM:(h + 1) * HEAD_DIM] for h in range(GQA_GROUP)], axis=0)
    nrow = GQA_GROUP * tq

    def step(k, v, carry):
        m, acc = carry
        s = lax.dot_general(q, k, (((1,), (1,)), ((), ())), preferred_element_type=F32)
        m_new = jnp.maximum(m, jnp.max(s, axis=-1, keepdims=True))
        alpha = jnp.exp2(m - m_new)
        p = jnp.exp2(s - m_new).astype(BF16)
        v_ones = jnp.concatenate([v, jnp.ones_like(v)], axis=1)
        acc = alpha * acc + jnp.dot(p, v_ones, preferred_element_type=F32)
        return m_new, acc

    carry = (jnp.full((nrow, 1), -jnp.inf, F32), jnp.zeros((nrow, 2 * HEAD_DIM), F32))
    for k_ref, v_ref, length in zip(k_refs, v_refs, kv_lens):
        t = min(tkv, length)
        for c in range(length // t):
            carry = step(k_ref[c * t:(c + 1) * t, :], v_ref[c * t:(c + 1) * t, :], carry)
    _, acc = carry
    out = acc[:, :HEAD_DIM] / acc[:, HEAD_DIM:]
    for h in range(GQA_GROUP):
        o_ref[:, h * HEAD_DIM:(h + 1) * HEAD_DIM] = out[h * tq:(h + 1) * tq].astype(o_ref.dtype)


def _flash(qkv, *, n_q, n_kv, n_batch, q_row0, q_len, kv_srcs, tq):
    qw = GQA_GROUP * HEAD_DIM
    nq_t = q_len // tq
    assert q_row0 % tq == 0
    in_specs = [pl.BlockSpec((tq, qw), lambda b, g, i: (q_row0 // tq + b * nq_t + i, g))]
    args = [qkv]
    for col0 in (n_q, n_q + n_kv):
        for (row0, length) in kv_srcs:
            assert row0 % length == 0
            in_specs.append(pl.BlockSpec(
                (length, HEAD_DIM),
                lambda b, g, i, row0=row0, length=length, col0=col0: (row0 // length + b, col0 + g)))
            args.append(qkv)
    return pl.pallas_call(
        functools.partial(_flash_kernel, tq=tq, kv_lens=tuple(l for _, l in kv_srcs), tkv=256),
        grid=(n_batch, n_kv, nq_t),
        in_specs=in_specs,
        out_specs=pl.BlockSpec((tq, qw), lambda b, g, i: (b * nq_t + i, g)),
        out_shape=jax.ShapeDtypeStruct((n_batch * q_len, n_q * HEAD_DIM), BF16),
        compiler_params=_params("parallel", "parallel", "arbitrary"),
        name="gqa_attention",
    )(*args)


def _ssm_taps_kernel(z_ref, c_ref, o_ref):
    o_ref[...] = lax.dot_general(z_ref[...], c_ref[...], (((1,), (1,)), ((), ())),
                                 precision=lax.Precision.HIGHEST, preferred_element_type=F32)


def _ssm_taps(z, c):
    nj, nd, rows, p2 = z.shape
    return pl.pallas_call(
        _ssm_taps_kernel,
        grid=(nj, nd),
        in_specs=[
            pl.BlockSpec((None, None, rows, p2), lambda j, d: (j, d, 0, 0)),
            pl.BlockSpec((None, None, LANES, p2), lambda j, d: (j, d, 0, 0)),
        ],
        out_specs=pl.BlockSpec((None, None, rows, LANES), lambda j, d: (j, d, 0, 0)),
        out_shape=jax.ShapeDtypeStruct((nj, nd, rows, LANES), F32),
        compiler_params=_params("parallel", "parallel"),
        name="ssm_taps",
    )(z, c)


def _ssm_scan_kernel(s_ref, p_ref, o_ref, *, n_batch, lat_tiles, ctx_tiles, half):
    row = lax.broadcasted_iota(jnp.int32, (8, half), 0)
    lat_rows = n_batch * lat_tiles * 8

    def run(d, col0, reverse):
        pr, pi = p_ref[2 * d], p_ref[2 * d + 1]
        ent_r, ent_i = pr[0:8], pi[0:8]
        steps = [(k, pr[8 + n:9 + n], pi[8 + n:9 + n]) for n, k in enumerate((1, 2, 4))]
        a8_r, a8_i = pr[11:12], pi[11:12]

        def shift(x, k):
            if reverse:
                return jnp.where(row < 8 - k, pltpu.roll(x, 8 - k, axis=0), 0.0)
            return jnp.where(row >= k, pltpu.roll(x, k, axis=0), 0.0)

        def tile(r0, h):
            hr, hi = h
            ir = s_ref[pl.ds(r0, 8), col0:col0 + half]
            ii = s_ref[pl.ds(r0, 8), col0 + half:col0 + 2 * half]
            for k, ar, ai in steps:
                tr, ti = shift(ir, k), shift(ii, k)
                ir, ii = ir + ar * tr - ai * ti, ii + ar * ti + ai * tr
            o_ref[pl.ds(r0, 8), col0:col0 + half] = ent_r * hr - ent_i * hi + shift(ir, 1)
            o_ref[pl.ds(r0, 8), col0 + half:col0 + 2 * half] = ent_r * hi + ent_i * hr + shift(ii, 1)
            nr = a8_r * hr - a8_i * hi + ir
            ni = a8_r * hi + a8_i * hr + ii
            last = 0 if reverse else 7
            return (jnp.broadcast_to(nr[last:last + 1], (8, half)), jnp.broadcast_to(ni[last:last + 1], (8, half)))

        def phase(base, n_tiles, hs):
            def body(i, hs):
                t = (n_tiles - 1 - i) if reverse else i
                return tuple(tile(pl.multiple_of(base + (b * n_tiles + t) * 8, 8), hs[b]) for b in range(n_batch))
            return lax.fori_loop(0, n_tiles, body, hs)

        zero = jnp.zeros((8, half), F32)
        hs = phase(lat_rows, ctx_tiles, tuple((zero, zero) for _ in range(n_batch)))
        phase(0, lat_tiles, hs)

    run(0, 0, False)
    run(1, 2 * half, True)


def _ssm_scan(merged, pw, layer, *, n_batch, lat_tiles, ctx_tiles):
    nj, m, _ = merged.shape
    half = pw.shape[4]
    w = 4 * half
    assert merged.shape[2] == 2 * w and m == n_batch * (lat_tiles + ctx_tiles) * 8
    return pl.pallas_call(
        functools.partial(_ssm_scan_kernel, n_batch=n_batch, lat_tiles=lat_tiles, ctx_tiles=ctx_tiles, half=half),
        grid=(nj,),
        in_specs=[
            pl.BlockSpec((None, m, w), lambda j: (j, 0, 1)),
            pl.BlockSpec((None, None, 4, 16, half), lambda j: (layer, j, 0, 0, 0)),
        ],
        out_specs=pl.BlockSpec((None, m, w), lambda j: (j, 0, 0)),
        out_shape=jax.ShapeDtypeStruct((nj, m, w), F32),
        compiler_params=_params("parallel"),
        name="ssm_scan",
    )(merged, pw)


def _ssm_weights(lam_re, lam_im, log_dt, b_re, b_im, c_re, c_im):
    T, H, P = SSM_CHUNK, SSM_GROUP, SSM_STATE
    G = lam_re.shape[1]
    GL = GROUPS_PER_LANE_BLOCK
    J = G // GL
    lam_re, lam_im = lam_re.astype(F32), lam_im.astype(F32)
    dt = jnp.exp(log_dt.astype(F32))[..., None]
    mag = jnp.exp(lam_re * dt)
    ab_r, ab_i = mag * jnp.cos(lam_im * dt), mag * jnp.sin(lam_im * dt)
    nr, ni = ab_r - 1.0, ab_i
    den = lam_re * lam_re + lam_im * lam_im
    cr = (nr * lam_re + ni * lam_im) / den
    ci = (ni * lam_re - nr * lam_im) / den
    b_re, b_im = b_re.astype(F32), b_im.astype(F32)
    bb_r = cr[..., None] * b_re - ci[..., None] * b_im
    bb_i = cr[..., None] * b_im + ci[..., None] * b_re
    c_re, c_im = c_re.astype(F32), c_im.astype(F32)

    def powers(d, exps):
        e = jnp.asarray(exps, F32)[:, None, None]
        pmag = jnp.exp(e * (lam_re[d] * dt[d])[None])
        pang = e * (lam_im[d] * dt[d])[None]
        return pmag * jnp.cos(pang), pmag * jnp.sin(pang)

    def drive(d, exps):
        pr, pi = powers(d, exps)
        return (pr[..., None] * bb_r[d][None] - pi[..., None] * bb_i[d][None],
                pr[..., None] * bb_i[d][None] + pi[..., None] * bb_r[d][None])

    asc = list(range(T))
    desc = [T - 1 - s for s in range(T)]

    def tap_lhs(d, exps):
        zr, zi = drive(d, exps)
        z = jnp.concatenate([zr, zi], axis=2).reshape(T, J, GL, 2 * P, H)
        return z.transpose(1, 0, 2, 4, 3).reshape(J, T * LANES, 2 * P)
    z_all = jnp.stack([tap_lhs(0, asc), tap_lhs(1, desc)], axis=1)
    c_all = jnp.concatenate([c_re, -c_im], axis=-1).reshape(2, J, LANES, 2 * P).transpose(1, 0, 2, 3)

    def summary(d, exps):
        pr, pi = powers(d, exps)
        pr, pi = (a.reshape(T, J, 1, GL * P).transpose(1, 0, 2, 3) for a in (pr, pi))
        br, bi = (a.reshape(J, GL, P, H).transpose(0, 3, 1, 2).reshape(J, 1, H, GL * P)
                  for a in (bb_r[d], bb_i[d]))
        return jnp.concatenate([pr * br - pi * bi, pr * bi + pi * br], axis=-1).reshape(J, T * H, 2 * GL * P)
    z_sum = jnp.concatenate([summary(0, desc), summary(1, asc)], axis=-1)

    def readout(d, exps):
        pr, pi = powers(d, exps)
        pr, pi = (jnp.repeat(a.reshape(T, J, GL, P).transpose(1, 3, 0, 2), H, axis=-1)
                  for a in (pr, pi))
        cr_l, ci_l = (a.reshape(J, GL, H, P).transpose(0, 3, 1, 2).reshape(J, P, 1, LANES)
                      for a in (c_re[d], c_im[d]))
        vr = (cr_l * pr - ci_l * pi).reshape(J, 1, P, T * LANES)
        vi = (cr_l * pi + ci_l * pr).reshape(J, 1, P, T * LANES)
        return jnp.concatenate([vr, -vi], axis=1)
    v_out = jnp.concatenate([readout(0, [t + 1 for t in range(T)]), readout(1, [T - t for t in range(T)])],
                            axis=1)

    def scan_powers(d, entry):
        exps = [T * k for k in entry] + [T, 2 * T, 4 * T, 8 * T] + [0] * 4
        return jnp.stack([a.reshape(16, J, GL * P).transpose(1, 0, 2) for a in powers(d, exps)], axis=1)
    pw = jnp.concatenate([scan_powers(0, list(range(8))), scan_powers(1, [7 - r for r in range(8)])], axis=1)
    return z_all, c_all, z_sum, v_out, pw


def _ssm_tap_tiles(raw):
    T, H = SSM_CHUNK, SSM_GROUP
    raw = raw.reshape(raw.shape[0], 2, T, LANES, LANES)
    fwd, bwd = raw[:, 0], raw[:, 1]
    tiles = jnp.concatenate([bwd[:, :T - 1], bwd[:, T - 1:] + fwd[:, :1], fwd[:, 1:]], axis=1)
    r = lax.broadcasted_iota(jnp.int32, (LANES, LANES), 0) // H
    c = lax.broadcasted_iota(jnp.int32, (LANES, LANES), 1) // H
    return jnp.where(r == c, tiles, 0.0).astype(BF16)


def _ssm_weights_all(*params):
    z_all, c_all, z_sum, v_out, pw = jax.vmap(_ssm_weights)(*params)
    n_layers, nj = z_all.shape[:2]
    raw = _ssm_taps(z_all.reshape((n_layers * nj,) + z_all.shape[2:]), c_all.reshape((n_layers * nj,) + c_all.shape[2:]))
    tiles = jax.vmap(_ssm_tap_tiles)(raw.reshape((n_layers, nj) + raw.shape[1:]))
    return tiles, z_sum, v_out, pw


def _s5_mixer(u, d4, layer, n_batch, seq, n_ctx, weights):
    tiles, z_sum, v_out, pw = weights
    T = SSM_CHUNK
    merged = _ssm_chunk_ops(u, tiles, z_sum, layer)
    h_in = _ssm_scan(merged, pw, layer, n_batch=n_batch, lat_tiles=seq // (8 * T), ctx_tiles=n_ctx // (8 * T))
    return _ssm_readout(h_in, v_out, merged, u, d4, layer)


def _rope_tables(seq, n_batch, n_ctx_rows):
    rows_n = seq // GRID_W
    pos = jnp.arange(seq)
    rows = (pos // GRID_W).astype(F32)
    cols = (pos % GRID_W).astype(F32)
    del rows_n
    axis_dim = HEAD_DIM // 2
    inv_freq = ROPE_THETA ** (-jnp.arange(0, axis_dim, 2, dtype=F32) / axis_dim)
    ang_r, ang_c = rows[:, None] * inv_freq, cols[:, None] * inv_freq
    cosf = jnp.concatenate([jnp.cos(ang_r)] * 2 + [jnp.cos(ang_c)] * 2, axis=1)
    sins = jnp.concatenate([-jnp.sin(ang_r), jnp.sin(ang_r), -jnp.sin(ang_c), jnp.sin(ang_c)], axis=1)
    cosf = jnp.concatenate([jnp.tile(cosf, (n_batch, 1)), jnp.ones((n_ctx_rows, HEAD_DIM), F32)], axis=0)
    sins = jnp.concatenate([jnp.tile(sins, (n_batch, 1)), jnp.zeros((n_ctx_rows, HEAD_DIM), F32)], axis=0)
    return cosf, sins


def kernel(x, c, ctx, c_ctx, w_ada, b_ada, norm_ffn1, ffn1_w_in, ffn1_w_out, norm_mix, w_mix_in, q_norm, k_norm, ssm_lam_re, ssm_lam_im, ssm_log_dt, ssm_b_re, ssm_b_im, ssm_c_re, ssm_c_im, ssm_d, w_glu, b_glu, w_mix_out, norm_ffn2, ffn2_w_in, ffn2_w_out, norm_final):
    n_batch, seq, d = x.shape
    n_ctx = ctx.shape[1]
    depth = w_ada.shape[0]
    ssm_width = w_glu.shape[1]
    attn_width = w_mix_out.shape[1] - ssm_width
    n_q = attn_width // HEAD_DIM
    n_kv = n_q // GQA_GROUP
    n_lat = n_batch * seq
    n_ctx_rows = n_batch * n_ctx
    n_all = n_lat + n_ctx_rows
    tm = n_ctx_rows
    tr = _tile(tm, 512, 8)
    assert seq % tm == 0 and n_batch == 4

    h, h_tail = x.reshape(n_lat, d), ctx.reshape(n_ctx_rows, d)
    cc =jnp.concatenate([c, c_ctx[None], jnp.zeros((8 - n_batch - 1, d), F32)], axis=0)
    cosf, sins = _rope_tables(seq, n_batch, n_ctx_rows)

    def row3(a):
        return a.reshape(a.shape[0], 1, a.shape[1])

    b_ada3 = row3(b_ada)
    g_ffn1, g_mix, g_ffn2 = row3(norm_ffn1), row3(norm_mix), row3(norm_ffn2)
    qg3, kg3, bglu3 = row3(q_norm), row3(k_norm), row3(b_glu)
    d4 = ssm_d.astype(F32).reshape(depth, ssm_width // LANES, 1, LANES)
    ffn1_in, ffn1_out = ffn1_w_in.astype(BF16), ffn1_w_out.astype(BF16)
    ffn2_in, ffn2_out = ffn2_w_in.astype(BF16), ffn2_w_out.astype(BF16)
    mix_in, mix_out, glu = w_mix_in.astype(BF16), w_mix_out.astype(BF16), w_glu.astype(BF16)

    def ffn(h, mod4, k, gain3, w_in, w_out, layer, rows, h_tail=None):
        hn = _norm(h, gain3, layer, rows=rows, tr=tr, out_dtype=BF16, mod4=mod4, k=k, seq=seq, n_batch=n_batch,
                   x_tail=h_tail)
        hid = _mm_swiglu(hn, w_in, layer, rows=rows, tm=tm, tn=_tile(w_in.shape[2] // 2, 512))
        tk = _tile(w_out.shape[1], 4096 if h_tail is None else 2048)
        return _mm_resid(hid, w_out, layer, h, mod4, k, 0.5, rows=rows, tm=tm, tn=_tile(d, 1024),
                         tk=tk, seq=seq, n_batch=n_batch, res_tail=h_tail)

    weights = _ssm_weights_all(ssm_lam_re, ssm_lam_im, ssm_log_dt, ssm_b_re, ssm_b_im, ssm_c_re, ssm_c_im)

    for layer in range(depth):
        last = layer == depth - 1
        mod4 = _mod_table(cc, w_ada, b_ada3, layer).reshape(8, 9, 1, d)

        h = ffn(h, mod4, 0, g_ffn1, ffn1_in, ffn1_out, layer, n_all, h_tail=h_tail)
        h_tail = None

        hn = _norm(h, g_mix, layer, rows=n_all, tr=tr, out_dtype=BF16, mod4=mod4, k=1, seq=seq, n_batch=n_batch)
        qkv, u = _mix_in(hn, mix_in, layer, cosf, sins, qg3, kg3, rows=n_all, tm=tm, n_q=n_q, n_kv=n_kv)
        attn = _flash(qkv, n_q=n_q, n_kv=n_kv, n_batch=n_batch, q_row0=0, q_len=seq,
                      kv_srcs=[(0, seq), (n_lat, n_ctx)], tq=min(512, seq))
        out_rows = n_lat if last else n_all
        attn_c = None
        if not last:
            attn_c = _flash(qkv, n_q=n_q, n_kv=n_kv, n_batch=n_batch, q_row0=n_lat, q_len=n_ctx,
                            kv_srcs=[(n_lat, n_ctx)], tq=min(128, n_ctx))

        yg = _s5_mixer(u, d4, layer, n_batch, seq, n_ctx, weights)
        ssm_out = _mm_glu(yg, glu, bglu3, layer, rows=out_rows, tm=tm, tn=_tile(ssm_width, 1024))

        h = _mm_mix_out(attn, attn_c, ssm_out, mix_out, layer, h, mod4, 1, rows=out_rows, tm=tm,
                        tn=_tile(d, 512), seq=seq, n_batch=n_batch)
        h = ffn(h, mod4, 2, g_ffn2, ffn2_in, ffn2_out, layer, out_rows)

    out = _norm(h, row3(norm_final[None]), 0, rows=n_lat, tr=tr, out_dtype=F32)
    return out.reshape(n_batch, seq, d)
```

```python
import functools
import math

import jax
import jax.numpy as jnp
from jax import lax
from jax.experimental import pallas as pl
from jax.experimental.pallas import tpu as pltpu

F32 = jnp.float32
BF16 = jnp.bfloat16

HEAD_DIM = 128
GQA_GROUP = 4
GRID_W = 64
ROPE_THETA = 10000.0
SSM_GROUP = 16
SSM_STATE = 64
SSM_CHUNK = 16
LANES = 128
GROUPS_PER_LANE_BLOCK = LANES // SSM_GROUP
EPS = 1e-6
VMEM_LIMIT_BYTES = 56 * 1024 * 1024


def _params(*sem):
    return pltpu.CompilerParams(dimension_semantics=sem, vmem_limit_bytes=VMEM_LIMIT_BYTES)


def _tile(n, pref, mult=LANES):
    if n <= pref:
        return n
    t = (pref // mult) * mult
    while t > mult and n % t:
        t -= mult
    assert n % t == 0, (n, pref, mult)
    return t


def _mod_kernel(c_ref, w_ref, b_ref, o_ref):
    c = c_ref[...]
    s = (c * jax.nn.sigmoid(c)).astype(BF16)
    o_ref[...] = jnp.dot(s, w_ref[...].astype(BF16), preferred_element_type=F32) + b_ref[...]


def _mod_table(cc, w_ada, b_ada3, layer):
    rows, d = cc.shape
    n = w_ada.shape[2]
    tn = _tile(n, 512)
    return pl.pallas_call(
        _mod_kernel,
        grid=(n // tn,),
        in_specs=[
            pl.BlockSpec((rows, d), lambda j: (0, 0)),
            pl.BlockSpec((None, d, tn), lambda j: (layer, 0, j)),
            pl.BlockSpec((None, 1, tn), lambda j: (layer, 0, j)),
        ],
        out_specs=pl.BlockSpec((rows, tn), lambda j: (0, j)),
        out_shape=jax.ShapeDtypeStruct((rows, n), F32),
        compiler_params=_params("parallel"),
        name="mod_table",
    )(cc, w_ada, b_ada3)


def _norm_kernel(x_ref, *rest, modulate, n_head_blocks):
    o_ref = rest[-1]
    if n_head_blocks is not None:
        tail_ref, rest = rest[0], rest[1:]
        from_head = pl.program_id(0) < n_head_blocks
    g_ref = rest[0]
    gain = g_ref[...]
    if modulate:
        sh_ref, sc_ref = rest[1], rest[2]
        gain = gain * (1.0 + sc_ref[...])
        shift = sh_ref[...]

    rc = 16

    def chunk(i, carry):
        rows = pl.ds(pl.multiple_of(i * rc, rc), rc)
        x = x_ref[rows, :]
        if n_head_blocks is not None:
            x = jnp.where(from_head, x, tail_ref[rows, :])
        r = lax.rsqrt(jnp.mean(x * x, axis=-1, keepdims=True) + EPS)
        h = (x * r) * gain
        if modulate:
            h = h + shift
        o_ref[rows, :] = h.astype(o_ref.dtype)
        return carry

    lax.fori_loop(0, x_ref.shape[0] // rc, chunk, 0, unroll=True)


def _two_source_specs(head, tail, block, col_of):
    nh, nt = head.shape[0] // block[0], tail.shape[0] // block[0]
    return [
        pl.BlockSpec(block, lambda i, *g: (jnp.minimum(i, nh - 1), col_of(*g))),
        pl.BlockSpec(block, lambda i, *g: (jnp.clip(i - nh, 0, nt - 1), col_of(*g))),
    ], nh


def _norm(x, gain3, layer, *, rows, tr, out_dtype, mod4=None, k=None, seq=None, n_batch=None, x_tail=None):
    d = x.shape[1]
    modulate = mod4 is not None
    n_head_blocks = None
    if x_tail is None:
        in_specs, args = [pl.BlockSpec((tr, d), lambda i: (i, 0))], [x]
    else:
        in_specs, n_head_blocks = _two_source_specs(x, x_tail, (tr, d), lambda: 0)
        args = [x, x_tail]
    in_specs.append(pl.BlockSpec((None, 1, d), lambda i: (layer, 0, 0)))
    args.append(gain3)
    if modulate:
        def mrow(i):
            return jnp.minimum((i * tr) // seq, n_batch)
        in_specs += [
            pl.BlockSpec((None, None, 1, d), lambda i: (mrow(i), 3 * k, 0, 0)),
            pl.BlockSpec((None, None, 1, d), lambda i: (mrow(i), 3 * k + 1, 0, 0)),
        ]
        args += [mod4, mod4]
    return pl.pallas_call(
        functools.partial(_norm_kernel, modulate=modulate, n_head_blocks=n_head_blocks),
        grid=(rows // tr,),
        in_specs=in_specs,
        out_specs=pl.BlockSpec((tr, d), lambda i: (i, 0)),
        out_shape=jax.ShapeDtypeStruct((rows, d), out_dtype),
        compiler_params=_params("parallel"),
        name="rmsnorm_mod" if modulate else "rmsnorm",
    )(*args)


def _cast_jobs(jobs, n_steps, step_of):
    in_specs, args, out_specs, out_shapes = [], [], [], []
    for w3, layer in jobs:
        kdim, n = w3.shape[1:]
        bk = 16
        while kdim % bk or kdim // bk > n_steps:
            bk += 16
        nb = kdim // bk
        in_specs.append(pl.BlockSpec(
            (None, bk, n), lambda *g, layer=layer, nb=nb: (layer, jnp.minimum(step_of(*g), nb - 1), 0)))
        out_specs.append(pl.BlockSpec((None, bk, n), lambda *g, nb=nb: (0, jnp.minimum(step_of(*g), nb - 1), 0)))
        out_shapes.append(jax.ShapeDtypeStruct((1, kdim, n), BF16))
        args.append(w3)
    return in_specs, args, out_specs, out_shapes


def _run_cast_jobs(cast_in, cast_out):
    for src, dst in zip(cast_in, cast_out):
        dst[...] = src[...].astype(dst.dtype)


def _mm_swiglu_kernel(x_ref, wg_ref, wu_ref, *rest, n_cast):
    o_ref = rest[n_cast]
    x = x_ref[...]
    g = jnp.dot(x, wg_ref[...], preferred_element_type=F32)
    u = jnp.dot(x, wu_ref[...], preferred_element_type=F32)
    o_ref[...] = (g * jax.nn.sigmoid(g) * u).astype(o_ref.dtype)
    _run_cast_jobs(rest[:n_cast], rest[n_cast + 1:])


def _mm_swiglu(x, w3, layer, *, rows, tm, tn, casts=()):
    kdim, f = w3.shape[1], w3.shape[2] // 2
    nj = f // tn
    grid = (rows // tm, nj)
    c_in, c_args, c_out, c_shapes = _cast_jobs(casts, grid[0] * grid[1], lambda i, j: i * nj + j)
    outs = pl.pallas_call(
        functools.partial(_mm_swiglu_kernel, n_cast=len(casts)),
        grid=grid,
        in_specs=[
            pl.BlockSpec((tm, kdim), lambda i, j: (i, 0)),
            pl.BlockSpec((None, kdim, tn), lambda i, j: (layer, 0, j)),
            pl.BlockSpec((None, kdim, tn), lambda i, j: (layer, 0, j + nj)),
        ] + c_in,
        out_specs=[pl.BlockSpec((tm, tn), lambda i, j: (i, j))] + c_out,
        out_shape=[jax.ShapeDtypeStruct((rows, f), BF16)] + c_shapes,
        compiler_params=_params("arbitrary", "arbitrary"),
        name="mm_swiglu",
    )(x, w3, w3, *c_args)
    return outs[0], tuple(outs[1:])


def _mm_resid_kernel(x_ref, w_ref, gate_ref, res_ref, *rest, nk, coef, n_head_blocks):
    o_ref = rest[-1]

    def part():
        return (coef * gate_ref[...]) * jnp.dot(x_ref[...], w_ref[...], preferred_element_type=F32)

    def residual():
        if n_head_blocks is None:
            return res_ref[...]
        return jnp.where(pl.program_id(0) < n_head_blocks, res_ref[...], rest[0][...])

    if nk == 1:
        o_ref[...] = residual() + part()
    else:
        @pl.when(pl.program_id(2) == 0)
        def _():
            o_ref[...] = residual() + part()

        @pl.when(pl.program_id(2) != 0)
        def _():
            o_ref[...] += part()


def _mm_resid(x, w3, layer, res, mod4, k, coef, *, rows, tm, tn, tk, seq, n_batch, res_tail=None):
    kdim, n = w3.shape[1], w3.shape[2]
    nk = kdim // tk

    def mrow(i):
        return jnp.minimum((i * tm) // seq, n_batch)

    n_head_blocks = None
    if res_tail is None:
        res_specs, res_args = [pl.BlockSpec((tm, tn), lambda i, j, kk: (i, j))], [res]
    else:
        res_specs, n_head_blocks = _two_source_specs(res, res_tail, (tm, tn), lambda j, kk: j)
        res_args = [res, res_tail]
    return pl.pallas_call(
        functools.partial(_mm_resid_kernel, nk=nk, coef=coef, n_head_blocks=n_head_blocks),
        grid=(rows // tm, n // tn, nk),
        in_specs=[
            pl.BlockSpec((tm, tk), lambda i, j, kk: (i, kk)),
            pl.BlockSpec((None, tk, tn), lambda i, j, kk: (layer, kk, j)),
            pl.BlockSpec((None, None, 1, tn), lambda i, j, kk: (mrow(i), 3 * k + 2, 0, j)),
        ] + res_specs,
        out_specs=pl.BlockSpec((tm, tn), lambda i, j, kk: (i, j)),
        out_shape=jax.ShapeDtypeStruct((rows, n), F32),
        compiler_params=_params("parallel", "parallel", "arbitrary"),
        name="mm_resid",
    )(x, w3, mod4, *res_args)


def _mm_glu_kernel(y_ref, w_ref, b_ref, yj_ref, o_ref):
    z = jnp.dot(y_ref[...].astype(BF16), w_ref[...], preferred_element_type=F32) + b_ref[...]
    o_ref[...] = (yj_ref[...] * jax.nn.sigmoid(z)).astype(o_ref.dtype)


def _mm_glu(y, w3, b3, w_layer, b_layer, *, rows, tm, tn):
    kdim, n = w3.shape[1], w3.shape[2]
    return pl.pallas_call(
        _mm_glu_kernel,
        grid=(rows // tm, n // tn),
        in_specs=[
            pl.BlockSpec((tm, kdim), lambda i, j: (i, 0)),
            pl.BlockSpec((None, kdim, tn), lambda i, j: (w_layer, 0, j)),
            pl.BlockSpec((None, 1, tn), lambda i, j: (b_layer, 0, j)),
            pl.BlockSpec((tm, tn), lambda i, j: (i, j)),
        ],
        out_specs=pl.BlockSpec((tm, tn), lambda i, j: (i, j)),
        out_shape=jax.ShapeDtypeStruct((rows, n), BF16),
        compiler_params=_params("parallel", "parallel"),
        name="mm_glu",
    )(y, w3, b3, y)


def _mm_mix_out_kernel(a_ref, ac_ref, s_ref, w_ref, res_ref, gate_ref, o_ref, *, n_lat_blocks, ka):
    a = jnp.where(pl.program_id(0) < n_lat_blocks, a_ref[...], ac_ref[...])
    total = jnp.dot(a, w_ref[:ka, :], preferred_element_type=F32)
    total += jnp.dot(s_ref[...], w_ref[ka:, :], preferred_element_type=F32)
    o_ref[...] = res_ref[...] + gate_ref[...] * total


def _mm_mix_out(attn, attn_c, ssm, w3, layer, res, mod4, k, *, rows, tm, tn, seq, n_batch):
    ka, ks = attn.shape[1], ssm.shape[1]
    n = w3.shape[2]
    n_lat_blocks = attn.shape[0] // tm
    if attn_c is None:
        attn_c = attn
    n_ctx_blocks = attn_c.shape[0] // tm

    def mrow(i):
        return jnp.minimum((i * tm) // seq, n_batch)

    return pl.pallas_call(
        functools.partial(_mm_mix_out_kernel, n_lat_blocks=n_lat_blocks, ka=ka),
        grid=(rows // tm, n // tn),
        in_specs=[
            pl.BlockSpec((tm, ka), lambda i, j: (jnp.minimum(i, n_lat_blocks - 1), 0)),
            pl.BlockSpec((tm, ka), lambda i, j: (jnp.clip(i - n_lat_blocks, 0, n_ctx_blocks - 1), 0)),
            pl.BlockSpec((tm, ks), lambda i, j: (i, 0)),
            pl.BlockSpec((None, ka + ks, tn), lambda i, j: (layer, 0, j)),
            pl.BlockSpec((tm, tn), lambda i, j: (i, j)),
            pl.BlockSpec((None, None, 1, tn), lambda i, j: (mrow(i), 3 * k + 2, 0, j)),
        ],
        out_specs=pl.BlockSpec((tm, tn), lambda i, j: (i, j)),
        out_shape=jax.ShapeDtypeStruct((rows, n), F32),
        compiler_params=_params("parallel", "parallel"),
        name="mm_mix_out",
    )(attn, attn_c, ssm, w3, res, mod4)


def _lane_group(shape, period, width):
    lane = lax.broadcasted_iota(jnp.int32, shape, 1)
    return (lane % period) // width


def _ssm_chunk_ops_kernel(x_ref, t_ref, z_ref, o_ref, lhs_ref, w_ref, *, n_t):
    T, H, GL, P = SSM_CHUNK, SSM_GROUP, GROUPS_PER_LANE_BLOCK, SSM_STATE
    c = pl.program_id(1)
    m = lhs_ref.shape[0]
    n_intra = T // n_t

    @pl.when(c == 0)
    def _():
        for s in range(T):
            lhs_ref[:, s * LANES:(s + 1) * LANES] = x_ref[pl.ds(s, m, stride=T), :].astype(BF16)

    for part in range(n_intra):
        @pl.when(c == part)
        def _(part=part):
            for s in range(T):
                for tt in range(n_t):
                    lag = part * n_t + tt - s + T - 1
                    w_ref[s * LANES:(s + 1) * LANES, tt * LANES:(tt + 1) * LANES] = t_ref[lag]

    @pl.when(c >= n_intra)
    def _():
        grp = _lane_group((H, w_ref.shape[1]), GL * P, P)
        for gl in range(GL):
            keep = grp == gl
            for s in range(T):
                r0 = s * LANES + gl * H
                w_ref[r0:r0 + H, :] = jnp.where(keep, z_ref[s * H:(s + 1) * H, :], 0.0).astype(BF16)

    o_ref[...] = jnp.dot(lhs_ref[...], w_ref[...], preferred_element_type=F32)


def _ssm_chunk_ops(u_tok, tiles, z, layer):
    T = SSM_CHUNK
    rows = u_tok.shape[0]
    nj = tiles.shape[1]
    m = rows // T
    n_t = T // 2
    tn = n_t * LANES
    n_intra = T // n_t
    assert z.shape[3] % tn == 0
    n_steps = n_intra + z.shape[3] // tn
    return pl.pallas_call(
        functools.partial(_ssm_chunk_ops_kernel, n_t=n_t),
        grid=(nj, n_steps),
        in_specs=[
            pl.BlockSpec((rows, LANES), lambda j, c: (0, j)),
            pl.BlockSpec((None, None, 2 * T - 1, LANES, LANES), lambda j, c: (layer, j, 0, 0, 0)),
            pl.BlockSpec((None, None, z.shape[2], tn), lambda j, c: (layer, j, 0, jnp.maximum(c - n_intra, 0))),
        ],
        out_specs=pl.BlockSpec((None, m, tn), lambda j, c: (j, 0, c)),
        out_shape=jax.ShapeDtypeStruct((nj, m, n_steps * tn), F32),
        scratch_shapes=[pltpu.VMEM((m, T * LANES), BF16), pltpu.VMEM((T * LANES, tn), BF16)],
        compiler_params=_params("parallel", "arbitrary"),
        name="ssm_chunk_ops",
    )(u_tok, tiles, z)


def _ssm_readout_kernel(h_ref, v_ref, add_ref, x_ref, d_ref, o_ref, w_ref):
    T, GL, P = SSM_CHUNK, GROUPS_PER_LANE_BLOCK, SSM_STATE

    @pl.when(pl.program_id(1) == 0)
    def _():
        grp = _lane_group((P, w_ref.shape[1]), LANES, SSM_GROUP)
        for gl in range(GL):
            keep = grp == gl
            for x in range(v_ref.shape[0]):
                r0 = (x * GL + gl) * P
                w_ref[r0:r0 + P, :] = jnp.where(keep, v_ref[x], 0.0).astype(BF16)

    y = add_ref[...] + jnp.dot(h_ref[...].astype(BF16), w_ref[...], preferred_element_type=F32)
    mq = y.shape[0]
    for t in range(T):
        tok = pl.ds(t, mq, stride=T)
        yt = y[:, t * LANES:(t + 1) * LANES] + d_ref[...] * x_ref[tok, :]
        o_ref[tok, :] = jax.nn.gelu(yt)


def _ssm_readout(h, v, merged, u_tok, d4, layer):
    T = SSM_CHUNK
    nj, m, kdim = h.shape
    n = v.shape[4]
    nq = next(q for q in (4, 3, 2, 1) if m % (8 * q) == 0)
    mq = m // nq
    return pl.pallas_call(
        _ssm_readout_kernel,
        grid=(nj, nq),
        in_specs=[
            pl.BlockSpec((None, mq, kdim), lambda j, q: (j, q, 0)),
            pl.BlockSpec((None, None, v.shape[2], v.shape[3], n), lambda j, q: (layer, j, 0, 0, 0)),
            pl.BlockSpec((None, mq, n), lambda j, q: (j, q, 0)),
            pl.BlockSpec((mq * T, LANES), lambda j, q: (q, j)),
            pl.BlockSpec((None, None, 1, LANES), lambda j, q: (layer, j, 0, 0)),
        ],
        out_specs=pl.BlockSpec((mq * T, LANES), lambda j, q: (q, j)),
        out_shape=jax.ShapeDtypeStruct(u_tok.shape, F32),
        scratch_shapes=[pltpu.VMEM((kdim, n), BF16)],
        compiler_params=_params("parallel", "arbitrary"),
        name="ssm_readout",
    )(h, v, merged, u_tok, d4)


def _mix_in_kernel(x_ref, w_ref, cos_ref, sin_ref, qg_ref, kg_ref, qkv_ref, u_ref, *, nq_tiles, heads):
    j = pl.program_id(1)

    def proj():
        return jnp.dot(x_ref[...], w_ref[...], preferred_element_type=F32)

    @pl.when(j <= nq_tiles)
    def _():
        is_q = j < nq_tiles
        gain = jnp.where(is_q, qg_ref[...], kg_ref[...])
        scale = jnp.where(is_q, HEAD_DIM ** -0.5 * math.log2(math.e), 1.0)
        rows = x_ref.shape[0]
        rc = min(rows, 256)
        for r0 in range(0, rows, rc):
            p = jnp.dot(x_ref[r0:r0 + rc, :], w_ref[...], preferred_element_type=F32)
            cosf = cos_ref[r0:r0 + rc, :]
            sins = sin_ref[r0:r0 + rc, :]
            lane = lax.broadcasted_iota(jnp.int32, cosf.shape, 1)
            first_half = (lane % (HEAD_DIM // 2)) < (HEAD_DIM // 4)
            for h in range(heads):
                sl = slice(h * HEAD_DIM, (h + 1) * HEAD_DIM)
                t = p[:, sl]
                r = lax.rsqrt(jnp.mean(t * t, axis=-1, keepdims=True) + EPS)
                t = (t * r) * gain
                partner = jnp.where(first_half,
                                    pltpu.roll(t, HEAD_DIM - HEAD_DIM // 4, axis=1),
                                    pltpu.roll(t, HEAD_DIM // 4, axis=1))
                t = (t * cosf + partner * sins) * scale
                qkv_ref[r0:r0 + rc, sl] = t.astype(qkv_ref.dtype)

    @pl.when(j == nq_tiles + 1)
    def _():
        qkv_ref[...] = proj().astype(qkv_ref.dtype)

    @pl.when(j > nq_tiles + 1)
    def _():
        u_ref[...] = proj().astype(u_ref.dtype)


def _mix_in(x, w3, layer, cosf, sins, qg3, kg3, g_layer, *, rows, tm, n_q, n_kv):
    kdim, n = w3.shape[1], w3.shape[2]
    tn = n_kv * HEAD_DIM
    nq_tiles = n_q // n_kv
    n_qkv_tiles = nq_tiles + 2
    n_tiles = n // tn
    return pl.pallas_call(
        functools.partial(_mix_in_kernel, nq_tiles=nq_tiles, heads=n_kv),
        grid=(rows // tm, n_tiles),
        in_specs=[
            pl.BlockSpec((tm, kdim), lambda i, j: (i, 0)),
            pl.BlockSpec((None, kdim, tn), lambda i, j: (layer, 0, j)),
            pl.BlockSpec((tm, HEAD_DIM), lambda i, j: (i, 0)),
            pl.BlockSpec((tm, HEAD_DIM), lambda i, j: (i, 0)),
            pl.BlockSpec((None, 1, HEAD_DIM), lambda i, j: (g_layer, 0, 0)),
            pl.BlockSpec((None, 1, HEAD_DIM), lambda i, j: (g_layer, 0, 0)),
        ],
        out_specs=[
            pl.BlockSpec((tm, tn), lambda i, j: (i, jnp.minimum(j, n_qkv_tiles - 1))),
            pl.BlockSpec((tm, tn), lambda i, j: (i, jnp.maximum(j - n_qkv_tiles, 0))),
        ],
        out_shape=[
            jax.ShapeDtypeStruct((rows, n_qkv_tiles * tn), BF16),
            jax.ShapeDtypeStruct((rows, n - n_qkv_tiles * tn), F32),
        ],
        compiler_params=_params("parallel", "arbitrary"),
        name="mix_in",
    )(x, w3, cosf, sins, qg3, kg3)


def _flash_kernel(q_ref, *rest, tq, kv_lens, tkv, n_cast):
    n_src = len(kv_lens)
    k_refs, v_refs = rest[:n_src], rest[n_src:2 * n_src]
    o_ref = rest[2 * n_src + n_cast]
    _run_cast_jobs(rest[2 * n_src:2 * n_src + n_cast], rest[2 * n_src + n_cast + 1:])
    q = jnp.concatenate([q_ref[:, h * HEAD_DIM:(h + 1) * HEAD_DIM] for h in range(GQA_GROUP)], axis=0)
    nrow = GQA_GROUP * tq

    def step(k, v, carry):
        m, acc = carry
        s = lax.dot_general(q, k, (((1,), (1,)), ((), ())), preferred_element_type=F32)
        m_new = jnp.maximum(m, jnp.max(s, axis=-1, keepdims=True))
        alpha = jnp.exp2(m - m_new)
        p = jnp.exp2(s - m_new).astype(BF16)
        v_ones = jnp.concatenate([v, jnp.ones_like(v)], axis=1)
        acc = alpha * acc + jnp.dot(p, v_ones, preferred_element_type=F32)
        return m_new, acc

    carry = (jnp.full((nrow, 1), -jnp.inf, F32), jnp.zeros((nrow, 2 * HEAD_DIM), F32))
    for k_ref, v_ref, length in zip(k_refs, v_refs, kv_lens):
        t = min(tkv, length)
        for c in range(length // t):
            carry = step(k_ref[c * t:(c + 1) * t, :], v_ref[c * t:(c + 1) * t, :], carry)
    _, acc = carry
    out = acc[:, :HEAD_DIM] / acc[:, HEAD_DIM:]
    for h in range(GQA_GROUP):
        o_ref[:, h * HEAD_DIM:(h + 1) * HEAD_DIM] = out[h * tq:(h + 1) * tq].astype(o_ref.dtype)


def _flash(qkv, *, n_q, n_kv, n_batch, q_row0, q_len, kv_srcs, tq, casts=()):
    qw = GQA_GROUP * HEAD_DIM
    nq_t = q_len // tq
    assert q_row0 % tq == 0
    in_specs = [pl.BlockSpec((tq, qw), lambda b, g, i: (q_row0 // tq + b * nq_t + i, g))]
    args = [qkv]
    for col0 in (n_q, n_q + n_kv):
        for (row0, length) in kv_srcs:
            assert row0 % length == 0
            in_specs.append(pl.BlockSpec(
                (length, HEAD_DIM),
                lambda b, g, i, row0=row0, length=length, col0=col0: (row0 // length + b, col0 + g)))
            args.append(qkv)
    c_in, c_args, c_out, c_shapes = _cast_jobs(casts, n_batch * n_kv * nq_t,
                                               lambda b, g, i: (b * n_kv + g) * nq_t + i)
    outs = pl.pallas_call(
        functools.partial(_flash_kernel, tq=tq, kv_lens=tuple(l for _, l in kv_srcs), tkv=256,
                          n_cast=len(casts)),
        grid=(n_batch, n_kv, nq_t),
        in_specs=in_specs + c_in,
        out_specs=[pl.BlockSpec((tq, qw), lambda b, g, i: (b * nq_t + i, g))] + c_out,
        out_shape=[jax.ShapeDtypeStruct((n_batch * q_len, n_q * HEAD_DIM), BF16)] + c_shapes,
        compiler_params=_params("arbitrary", "arbitrary", "arbitrary"),
        name="gqa_attention",
    )(*args, *c_args)
    return outs[0], tuple(outs[1:])


def _ssm_taps_kernel(z_ref, c_ref, o_ref):
    o_ref[...] = lax.dot_general(z_ref[...], c_ref[...], (((1,), (1,)), ((), ())),
                                 precision=lax.Precision.HIGHEST, preferred_element_type=F32)


def _ssm_taps(z, c):
    nj, nd, rows, p2 = z.shape
    return pl.pallas_call(
        _ssm_taps_kernel,
        grid=(nj, nd),
        in_specs=[
            pl.BlockSpec((None, None, rows, p2), lambda j, d: (j, d, 0, 0)),
            pl.BlockSpec((None, None, LANES, p2), lambda j, d: (j, d, 0, 0)),
        ],
        out_specs=pl.BlockSpec((None, None, rows, LANES), lambda j, d: (j, d, 0, 0)),
        out_shape=jax.ShapeDtypeStruct((nj, nd, rows, LANES), F32),
        compiler_params=_params("parallel", "parallel"),
        name="ssm_taps",
    )(z, c)


def _ssm_scan_kernel(s_ref, p_ref, o_ref, *, n_batch, lat_tiles, ctx_tiles, half):
    row = lax.broadcasted_iota(jnp.int32, (8, half), 0)
    lat_rows = n_batch * lat_tiles * 8

    def run(d, col0, reverse):
        pr, pi = p_ref[2 * d], p_ref[2 * d + 1]
        ent_r, ent_i = pr[0:8], pi[0:8]
        steps = [(k, pr[8 + n:9 + n], pi[8 + n:9 + n]) for n, k in enumerate((1, 2, 4))]
        a8_r, a8_i = pr[11:12], pi[11:12]

        def shift(x, k):
            if reverse:
                return jnp.where(row < 8 - k, pltpu.roll(x, 8 - k, axis=0), 0.0)
            return jnp.where(row >= k, pltpu.roll(x, k, axis=0), 0.0)

        def tile(r0, h):
            hr, hi = h
            ir = s_ref[pl.ds(r0, 8), col0:col0 + half]
            ii = s_ref[pl.ds(r0, 8), col0 + half:col0 + 2 * half]
            for k, ar, ai in steps:
                tr, ti = shift(ir, k), shift(ii, k)
                ir, ii = ir + ar * tr - ai * ti, ii + ar * ti + ai * tr
            o_ref[pl.ds(r0, 8), col0:col0 + half] = ent_r * hr - ent_i * hi + shift(ir, 1)
            o_ref[pl.ds(r0, 8), col0 + half:col0 + 2 * half] = ent_r * hi + ent_i * hr + shift(ii, 1)
            nr = a8_r * hr - a8_i * hi + ir
            ni = a8_r * hi + a8_i * hr + ii
            last = 0 if reverse else 7
            return (jnp.broadcast_to(nr[last:last + 1], (8, half)), jnp.broadcast_to(ni[last:last + 1], (8, half)))

        def phase(base, n_tiles, hs):
            def body(i, hs):
                t = (n_tiles - 1 - i) if reverse else i
                return tuple(tile(pl.multiple_of(base + (b * n_tiles + t) * 8, 8), hs[b]) for b in range(n_batch))
            return lax.fori_loop(0, n_tiles, body, hs)

        zero = jnp.zeros((8, half), F32)
        hs = phase(lat_rows, ctx_tiles, tuple((zero, zero) for _ in range(n_batch)))
        phase(0, lat_tiles, hs)

    run(0, 0, False)
    run(1, 2 * half, True)


def _ssm_scan(merged, pw, layer, *, n_batch, lat_tiles, ctx_tiles):
    nj, m, _ = merged.shape
    half = pw.shape[4]
    w = 4 * half
    assert merged.shape[2] == 2 * w and m == n_batch * (lat_tiles + ctx_tiles) * 8
    return pl.pallas_call(
        functools.partial(_ssm_scan_kernel, n_batch=n_batch, lat_tiles=lat_tiles, ctx_tiles=ctx_tiles, half=half),
        grid=(nj,),
        in_specs=[
            pl.BlockSpec((None, m, w), lambda j: (j, 0, 1)),
            pl.BlockSpec((None, None, 4, 16, half), lambda j: (layer, j, 0, 0, 0)),
        ],
        out_specs=pl.BlockSpec((None, m, w), lambda j: (j, 0, 0)),
        out_shape=jax.ShapeDtypeStruct((nj, m, w), F32),
        compiler_params=_params("parallel"),
        name="ssm_scan",
    )(merged, pw)


def _ssm_weights(lam_re, lam_im, log_dt, b_re, b_im, c_re, c_im):
    T, H, P = SSM_CHUNK, SSM_GROUP, SSM_STATE
    G = lam_re.shape[1]
    GL = GROUPS_PER_LANE_BLOCK
    J = G // GL
    lam_re, lam_im = lam_re.astype(F32), lam_im.astype(F32)
    dt = jnp.exp(log_dt.astype(F32))[..., None]
    mag = jnp.exp(lam_re * dt)
    ab_r, ab_i = mag * jnp.cos(lam_im * dt), mag * jnp.sin(lam_im * dt)
    nr, ni = ab_r - 1.0, ab_i
    den = lam_re * lam_re + lam_im * lam_im
    cr = (nr * lam_re + ni * lam_im) / den
    ci = (ni * lam_re - nr * lam_im) / den
    b_re, b_im = b_re.astype(F32), b_im.astype(F32)
    bb_r = cr[..., None] * b_re - ci[..., None] * b_im
    bb_i = cr[..., None] * b_im + ci[..., None] * b_re
    c_re, c_im = c_re.astype(F32), c_im.astype(F32)

    def powers(d, exps):
        e = jnp.asarray(exps, F32)[:, None, None]
        pmag = jnp.exp(e * (lam_re[d] * dt[d])[None])
        pang = e * (lam_im[d] * dt[d])[None]
        return pmag * jnp.cos(pang), pmag * jnp.sin(pang)

    def drive(d, exps):
        pr, pi = powers(d, exps)
        return (pr[..., None] * bb_r[d][None] - pi[..., None] * bb_i[d][None],
                pr[..., None] * bb_i[d][None] + pi[..., None] * bb_r[d][None])

    asc = list(range(T))
    desc = [T - 1 - s for s in range(T)]

    def tap_lhs(d, exps):
        zr, zi = drive(d, exps)
        z = jnp.concatenate([zr, zi], axis=2).reshape(T, J, GL, 2 * P, H)
        return z.transpose(1, 0, 2, 4, 3).reshape(J, T * LANES, 2 * P)
    z_all = jnp.stack([tap_lhs(0, asc), tap_lhs(1, desc)], axis=1)
    c_all = jnp.concatenate([c_re, -c_im], axis=-1).reshape(2, J, LANES, 2 * P).transpose(1, 0, 2, 3)

    def summary(d, exps):
        pr, pi = powers(d, exps)
        pr, pi = (a.reshape(T, J, 1, GL * P).transpose(1, 0, 2, 3) for a in (pr, pi))
        br, bi = (a.reshape(J, GL, P, H).transpose(0, 3, 1, 2).reshape(J, 1, H, GL * P)
                  for a in (bb_r[d], bb_i[d]))
        return jnp.concatenate([pr * br - pi * bi, pr * bi + pi * br], axis=-1).reshape(J, T * H, 2 * GL * P)
    z_sum = jnp.concatenate([summary(0, desc), summary(1, asc)], axis=-1)

    def readout(d, exps):
        pr, pi = powers(d, exps)
        pr, pi = (jnp.repeat(a.reshape(T, J, GL, P).transpose(1, 3, 0, 2), H, axis=-1)
                  for a in (pr, pi))
        cr_l, ci_l = (a.reshape(J, GL, H, P).transpose(0, 3, 1, 2).reshape(J, P, 1, LANES)
                      for a in (c_re[d], c_im[d]))
        vr = (cr_l * pr - ci_l * pi).reshape(J, 1, P, T * LANES)
        vi = (cr_l * pi + ci_l * pr).reshape(J, 1, P, T * LANES)
        return jnp.concatenate([vr, -vi], axis=1)
    v_out = jnp.concatenate([readout(0, [t + 1 for t in range(T)]), readout(1, [T - t for t in range(T)])],
                            axis=1)

    def scan_powers(d, entry):
        exps = [T * k for k in entry] + [T, 2 * T, 4 * T, 8 * T] + [0] * 4
        return jnp.stack([a.reshape(16, J, GL * P).transpose(1, 0, 2) for a in powers(d, exps)], axis=1)
    pw = jnp.concatenate([scan_powers(0, list(range(8))), scan_powers(1, [7 - r for r in range(8)])], axis=1)
    return z_all, c_all, z_sum, v_out, pw


def _ssm_tap_tiles(raw):
    T, H = SSM_CHUNK, SSM_GROUP
    raw = raw.reshape(raw.shape[0], 2, T, LANES, LANES)
    fwd, bwd = raw[:, 0], raw[:, 1]
    tiles = jnp.concatenate([bwd[:, :T - 1], bwd[:, T - 1:] + fwd[:, :1], fwd[:, 1:]], axis=1)
    r = lax.broadcasted_iota(jnp.int32, (LANES, LANES), 0) // H
    c = lax.broadcasted_iota(jnp.int32, (LANES, LANES), 1) // H
    return jnp.where(r == c, tiles, 0.0).astype(BF16)


def _ssm_weights_all(*params):
    z_all, c_all, z_sum, v_out, pw = jax.vmap(_ssm_weights)(*params)
    n_layers, nj = z_all.shape[:2]
    raw = _ssm_taps(z_all.reshape((n_layers * nj,) + z_all.shape[2:]), c_all.reshape((n_layers * nj,) + c_all.shape[2:]))
    tiles = jax.vmap(_ssm_tap_tiles)(raw.reshape((n_layers, nj) + raw.shape[1:]))
    return tiles, z_sum, v_out, pw


def _s5_mixer(u, d4, layer, n_batch, seq, n_ctx, weights):
    tiles, z_sum, v_out, pw = weights
    T = SSM_CHUNK
    merged = _ssm_chunk_ops(u, tiles, z_sum, layer)
    h_in = _ssm_scan(merged, pw, layer, n_batch=n_batch, lat_tiles=seq // (8 * T), ctx_tiles=n_ctx // (8 * T))
    return _ssm_readout(h_in, v_out, merged, u, d4, layer)


def _rope_tables(seq, n_batch, n_ctx_rows):
    rows_n = seq // GRID_W
    pos = jnp.arange(seq)
    rows = (pos // GRID_W).astype(F32)
    cols = (pos % GRID_W).astype(F32)
    del rows_n
    axis_dim = HEAD_DIM // 2
    inv_freq = ROPE_THETA ** (-jnp.arange(0, axis_dim, 2, dtype=F32) / axis_dim)
    ang_r, ang_c = rows[:, None] * inv_freq, cols[:, None] * inv_freq
    cosf = jnp.concatenate([jnp.cos(ang_r)] * 2 + [jnp.cos(ang_c)] * 2, axis=1)
    sins = jnp.concatenate([-jnp.sin(ang_r), jnp.sin(ang_r), -jnp.sin(ang_c), jnp.sin(ang_c)], axis=1)
    cosf = jnp.concatenate([jnp.tile(cosf, (n_batch, 1)), jnp.ones((n_ctx_rows, HEAD_DIM), F32)], axis=0)
    sins = jnp.concatenate([jnp.tile(sins, (n_batch, 1)), jnp.zeros((n_ctx_rows, HEAD_DIM), F32)], axis=0)
    return cosf, sins


def kernel(x, c, ctx, c_ctx, w_ada, b_ada, norm_ffn1, ffn1_w_in, ffn1_w_out, norm_mix, w_mix_in, q_norm, k_norm, ssm_lam_re, ssm_lam_im, ssm_log_dt, ssm_b_re, ssm_b_im, ssm_c_re, ssm_c_im, ssm_d, w_glu, b_glu, w_mix_out, norm_ffn2, ffn2_w_in, ffn2_w_out, norm_final):
    n_batch, seq, d = x.shape
    n_ctx = ctx.shape[1]
    depth = w_ada.shape[0]
    ssm_width = w_glu.shape[1]
    attn_width = w_mix_out.shape[1] - ssm_width
    n_q = attn_width // HEAD_DIM
    n_kv = n_q // GQA_GROUP
    n_lat = n_batch * seq
    n_ctx_rows = n_batch * n_ctx
    n_all = n_lat + n_ctx_rows
    tm = n_ctx_rows
    tr = _tile(tm, 512, 8)
    assert seq % tm == 0 and n_batch == 4

    h, h_tail = x.reshape(n_lat, d), ctx.reshape(n_ctx_rows, d)
    cc =jnp.concatenate([c, c_ctx[None], jnp.zeros((8 - n_batch - 1, d), F32)], axis=0)
    cosf, sins = _rope_tables(seq, n_batch, n_ctx_rows)

    def row3(a):
        return a.reshape(a.shape[0], 1, a.shape[1])

    b_ada3 = row3(b_ada)
    g_ffn1, g_mix, g_ffn2 = row3(norm_ffn1), row3(norm_mix), row3(norm_ffn2)
    qg3, kg3, bglu3 = row3(q_norm), row3(k_norm), row3(b_glu)
    d4 = ssm_d.astype(F32).reshape(depth, ssm_width // LANES, 1, LANES)
    def ffn(h, mod4, k, gain3, layer, rows, w_in, w_out_f32, casts, h_tail=None):
        hn = _norm(h, gain3, layer, rows=rows, tr=tr, out_dtype=BF16, mod4=mod4, k=k, seq=seq, n_batch=n_batch,
                   x_tail=h_tail)
        hid, cast = _mm_swiglu(hn, w_in, 0, rows=rows, tm=tm, tn=_tile(w_in.shape[2] // 2, 512),
                               casts=[(w_out_f32, layer)] + casts)
        tk = _tile(w_out_f32.shape[1], 4096 if h_tail is None else 2048)
        h = _mm_resid(hid, cast[0], 0, h, mod4, k, 0.5, rows=rows, tm=tm, tn=_tile(d, 1024),
                      tk=tk, seq=seq, n_batch=n_batch, res_tail=h_tail)
        return h, cast[1:]

    weights = _ssm_weights_all(ssm_lam_re, ssm_lam_im, ssm_log_dt, ssm_b_re, ssm_b_im, ssm_c_re, ssm_c_im)

    ffn1_in = ffn1_w_in[:1].astype(BF16)
    for layer in range(depth):
        last = layer == depth - 1
        mod4 = _mod_table(cc, w_ada, b_ada3, layer).reshape(8, 9, 1, d)

        h, (mix_in, glu, mix_out) = ffn(h, mod4, 0, g_ffn1, layer, n_all, ffn1_in, ffn1_w_out,
                                        [(w_mix_in, layer), (w_glu, layer), (w_mix_out, layer)], h_tail=h_tail)
        h_tail = None

        hn = _norm(h, g_mix, layer, rows=n_all, tr=tr, out_dtype=BF16, mod4=mod4, k=1, seq=seq, n_batch=n_batch)
        qkv, u = _mix_in(hn, mix_in, 0, cosf, sins, qg3, kg3, layer, rows=n_all, tm=tm, n_q=n_q, n_kv=n_kv)
        attn, (ffn2_in,) = _flash(qkv, n_q=n_q, n_kv=n_kv, n_batch=n_batch, q_row0=0, q_len=seq,
                                  kv_srcs=[(0, seq), (n_lat, n_ctx)], tq=min(512, seq), casts=[(ffn2_w_in, layer)])
        out_rows = n_lat if last else n_all
        attn_c = None
        if not last:
            attn_c, _ = _flash(qkv, n_q=n_q, n_kv=n_kv, n_batch=n_batch, q_row0=n_lat, q_len=n_ctx,
                               kv_srcs=[(n_lat, n_ctx)], tq=min(128, n_ctx))

        yg = _s5_mixer(u, d4, layer, n_batch, seq, n_ctx, weights)
        ssm_out = _mm_glu(yg, glu, bglu3, 0, layer, rows=out_rows, tm=tm, tn=_tile(ssm_width, 1024))

        h = _mm_mix_out(attn, attn_c, ssm_out, mix_out, 0, h, mod4, 1, rows=out_rows, tm=tm,
                        tn=_tile(d, 512), seq=seq, n_batch=n_batch)
        h, next_in = ffn(h, mod4, 2, g_ffn2, layer, out_rows, ffn2_in, ffn2_w_out,
                         [] if last else [(ffn1_w_in, layer + 1)])
        if not last:
            ffn1_in = next_in[0]

    out = _norm(h, row3(norm_final[None]), 0, rows=n_lat, tr=tr, out_dtype=F32)
    return out.reshape(n_batch, seq, d)
```

```python
import functools
import math

import jax
import jax.numpy as jnp
from jax import lax
from jax.experimental import pallas as pl
from jax.experimental.pallas import tpu as pltpu

F32 = jnp.float32
BF16 = jnp.bfloat16

HEAD_DIM = 128
GQA_GROUP = 4
GRID_W = 64
ROPE_THETA = 10000.0
SSM_GROUP = 16
SSM_STATE = 64
SSM_CHUNK = 16
LANES = 128
GROUPS_PER_LANE_BLOCK = LANES // SSM_GROUP
EPS = 1e-6
VMEM_LIMIT_BYTES = 56 * 1024 * 1024


def _params(*sem):
    return pltpu.CompilerParams(dimension_semantics=sem, vmem_limit_bytes=VMEM_LIMIT_BYTES)


def _tile(n, pref, mult=LANES):
    if n <= pref:
        return n
    t = (pref // mult) * mult
    while t > mult and n % t:
        t -= mult
    assert n % t == 0, (n, pref, mult)
    return t


def _mod_kernel(c_ref, w_ref, b_ref, o_ref):
    c = c_ref[...]
    s = (c * jax.nn.sigmoid(c)).astype(BF16)
    o_ref[...] = jnp.dot(s, w_ref[...].astype(BF16), preferred_element_type=F32) + b_ref[...]


def _mod_table(cc, w_ada, b_ada3, layer):
    rows, d = cc.shape
    n = w_ada.shape[2]
    tn = _tile(n, 512)
    return pl.pallas_call(
        _mod_kernel,
        grid=(n // tn,),
        in_specs=[
            pl.BlockSpec((rows, d), lambda j: (0, 0)),
            pl.BlockSpec((None, d, tn), lambda j: (layer, 0, j)),
            pl.BlockSpec((None, 1, tn), lambda j: (layer, 0, j)),
        ],
        out_specs=pl.BlockSpec((rows, tn), lambda j: (0, j)),
        out_shape=jax.ShapeDtypeStruct((rows, n), F32),
        compiler_params=_params("parallel"),
        name="mod_table",
    )(cc, w_ada, b_ada3)


def _norm_kernel(x_ref, *rest, modulate, n_head_blocks):
    o_ref = rest[-1]
    if n_head_blocks is not None:
        tail_ref, rest = rest[0], rest[1:]
        o_ref, merged_ref = rest[-2], rest[-1]
        from_head = pl.program_id(0) < n_head_blocks
    g_ref = rest[0]
    gain = g_ref[...]
    if modulate:
        sh_ref, sc_ref = rest[1], rest[2]
        gain = gain * (1.0 + sc_ref[...])
        shift = sh_ref[...]

    rc = 16

    def chunk(i, carry):
        rows = pl.ds(pl.multiple_of(i * rc, rc), rc)
        x = x_ref[rows, :]
        if n_head_blocks is not None:
            x = jnp.where(from_head, x, tail_ref[rows, :])
            merged_ref[rows, :] = x
        r = lax.rsqrt(jnp.mean(x * x, axis=-1, keepdims=True) + EPS)
        h = (x * r) * gain
        if modulate:
            h = h + shift
        o_ref[rows, :] = h.astype(o_ref.dtype)
        return carry

    lax.fori_loop(0, x_ref.shape[0] // rc, chunk, 0, unroll=True)


def _two_source_specs(head, tail, block, col_of):
    nh, nt = head.shape[0] // block[0], tail.shape[0] // block[0]
    return [
        pl.BlockSpec(block, lambda i, *g: (jnp.minimum(i, nh - 1), col_of(*g))),
        pl.BlockSpec(block, lambda i, *g: (jnp.clip(i - nh, 0, nt - 1), col_of(*g))),
    ], nh


def _norm(x, gain3, layer, *, rows, tr, out_dtype, mod4=None, k=None, seq=None, n_batch=None, x_tail=None):
    d = x.shape[1]
    modulate = mod4 is not None
    n_head_blocks = None
    if x_tail is None:
        in_specs, args = [pl.BlockSpec((tr, d), lambda i: (i, 0))], [x]
    else:
        in_specs, n_head_blocks = _two_source_specs(x, x_tail, (tr, d), lambda: 0)
        args = [x, x_tail]
    in_specs.append(pl.BlockSpec((None, 1, d), lambda i: (layer, 0, 0)))
    args.append(gain3)
    if modulate:
        def mrow(i):
            return jnp.minimum((i * tr) // seq, n_batch)
        in_specs += [
            pl.BlockSpec((None, None, 1, d), lambda i: (mrow(i), 3 * k, 0, 0)),
            pl.BlockSpec((None, None, 1, d), lambda i: (mrow(i), 3 * k + 1, 0, 0)),
        ]
        args += [mod4, mod4]
    out_specs = [pl.BlockSpec((tr, d), lambda i: (i, 0))]
    out_shape = [jax.ShapeDtypeStruct((rows, d), out_dtype)]
    if x_tail is not None:
        out_specs.append(pl.BlockSpec((tr, d), lambda i: (i, 0)))
        out_shape.append(jax.ShapeDtypeStruct((rows, d), x.dtype))
    outs = pl.pallas_call(
        functools.partial(_norm_kernel, modulate=modulate, n_head_blocks=n_head_blocks),
        grid=(rows // tr,),
        in_specs=in_specs,
        out_specs=out_specs,
        out_shape=out_shape,
        compiler_params=_params("parallel"),
        name="rmsnorm_mod" if modulate else "rmsnorm",
    )(*args)
    return outs[0] if x_tail is None else tuple(outs)


def _cast_jobs(jobs, n_steps, step_of):
    in_specs, args, out_specs, out_shapes = [], [], [], []
    for w3, layer in jobs:
        kdim, n = w3.shape[1:]
        bk = 16
        while kdim % bk or kdim // bk > n_steps:
            bk += 16
        nb = kdim // bk
        in_specs.append(pl.BlockSpec(
            (None, bk, n), lambda *g, layer=layer, nb=nb: (layer, jnp.minimum(step_of(*g), nb - 1), 0)))
        out_specs.append(pl.BlockSpec((None, bk, n), lambda *g, nb=nb: (0, jnp.minimum(step_of(*g), nb - 1), 0)))
        out_shapes.append(jax.ShapeDtypeStruct((1, kdim, n), BF16))
        args.append(w3)
    return in_specs, args, out_specs, out_shapes


def _run_cast_jobs(cast_in, cast_out):
    for src, dst in zip(cast_in, cast_out):
        dst[...] = src[...].astype(dst.dtype)


def _mm_swiglu_kernel(x_ref, wg_ref, wu_ref, *rest, n_cast):
    o_ref = rest[n_cast]
    x = x_ref[...]
    g = jnp.dot(x, wg_ref[...], preferred_element_type=F32)
    u = jnp.dot(x, wu_ref[...], preferred_element_type=F32)
    o_ref[...] = (g * jax.nn.sigmoid(g) * u).astype(o_ref.dtype)
    _run_cast_jobs(rest[:n_cast], rest[n_cast + 1:])


def _mm_swiglu(x, w3, layer, *, rows, tm, tn, casts=()):
    kdim, f = w3.shape[1], w3.shape[2] // 2
    nj = f // tn
    grid = (rows // tm, nj)
    c_in, c_args, c_out, c_shapes = _cast_jobs(casts, grid[0] * grid[1], lambda i, j: i * nj + j)
    outs = pl.pallas_call(
        functools.partial(_mm_swiglu_kernel, n_cast=len(casts)),
        grid=grid,
        in_specs=[
            pl.BlockSpec((tm, kdim), lambda i, j: (i, 0)),
            pl.BlockSpec((None, kdim, tn), lambda i, j: (layer, 0, j)),
            pl.BlockSpec((None, kdim, tn), lambda i, j: (layer, 0, j + nj)),
        ] + c_in,
        out_specs=[pl.BlockSpec((tm, tn), lambda i, j: (i, j))] + c_out,
        out_shape=[jax.ShapeDtypeStruct((rows, f), BF16)] + c_shapes,
        compiler_params=_params("arbitrary", "arbitrary"),
        name="mm_swiglu",
    )(x, w3, w3, *c_args)
    return outs[0], tuple(outs[1:])


def _mm_resid_kernel(x_ref, w_ref, gate_ref, res_ref, o_ref, *, nk, coef):
    def part():
        return (coef * gate_ref[...]) * jnp.dot(x_ref[...], w_ref[...], preferred_element_type=F32)

    if nk == 1:
        o_ref[...] = res_ref[...] + part()
    else:
        @pl.when(pl.program_id(2) == 0)
        def _():
            o_ref[...] = res_ref[...] + part()

        @pl.when(pl.program_id(2) != 0)
        def _():
            o_ref[...] += part()


def _mm_resid(x, w3, layer, res, mod4, k, coef, *, rows, tm, tn, tk, seq, n_batch):
    kdim, n = w3.shape[1], w3.shape[2]
    nk = kdim // tk

    def mrow(i):
        return jnp.minimum((i * tm) // seq, n_batch)

    return pl.pallas_call(
        functools.partial(_mm_resid_kernel, nk=nk, coef=coef),
        grid=(rows // tm, n // tn, nk),
        in_specs=[
            pl.BlockSpec((tm, tk), lambda i, j, kk: (i, kk)),
            pl.BlockSpec((None, tk, tn), lambda i, j, kk: (layer, kk, j)),
            pl.BlockSpec((None, None, 1, tn), lambda i, j, kk: (mrow(i), 3 * k + 2, 0, j)),
            pl.BlockSpec((tm, tn), lambda i, j, kk: (i, j)),
        ],
        out_specs=pl.BlockSpec((tm, tn), lambda i, j, kk: (i, j)),
        out_shape=jax.ShapeDtypeStruct((rows, n), F32),
        compiler_params=_params("parallel", "parallel", "arbitrary"),
        name="mm_resid",
    )(x, w3, mod4, res)


def _mm_glu_kernel(y_ref, w_ref, b_ref, yj_ref, o_ref):
    z = jnp.dot(y_ref[...].astype(BF16), w_ref[...], preferred_element_type=F32) + b_ref[...]
    o_ref[...] = (yj_ref[...] * jax.nn.sigmoid(z)).astype(o_ref.dtype)


def _mm_glu(y, w3, b3, w_layer, b_layer, *, rows, tm, tn):
    kdim, n = w3.shape[1], w3.shape[2]
    return pl.pallas_call(
        _mm_glu_kernel,
        grid=(rows // tm, n // tn),
        in_specs=[
            pl.BlockSpec((tm, kdim), lambda i, j: (i, 0)),
            pl.BlockSpec((None, kdim, tn), lambda i, j: (w_layer, 0, j)),
            pl.BlockSpec((None, 1, tn), lambda i, j: (b_layer, 0, j)),
            pl.BlockSpec((tm, tn), lambda i, j: (i, j)),
        ],
        out_specs=pl.BlockSpec((tm, tn), lambda i, j: (i, j)),
        out_shape=jax.ShapeDtypeStruct((rows, n), BF16),
        compiler_params=_params("parallel", "parallel"),
        name="mm_glu",
    )(y, w3, b3, y)


def _mm_mix_out_kernel(a_ref, ac_ref, s_ref, w_ref, res_ref, gate_ref, o_ref, *, n_lat_blocks, ka):
    a = jnp.where(pl.program_id(0) < n_lat_blocks, a_ref[...], ac_ref[...])
    total = jnp.dot(a, w_ref[:ka, :], preferred_element_type=F32)
    total += jnp.dot(s_ref[...], w_ref[ka:, :], preferred_element_type=F32)
    o_ref[...] = res_ref[...] + gate_ref[...] * total


def _mm_mix_out(attn, attn_c, ssm, w3, layer, res, mod4, k, *, rows, tm, tn, seq, n_batch):
    ka, ks = attn.shape[1], ssm.shape[1]
    n = w3.shape[2]
    n_lat_blocks = attn.shape[0] // tm
    if attn_c is None:
        attn_c = attn
    n_ctx_blocks = attn_c.shape[0] // tm

    def mrow(i):
        return jnp.minimum((i * tm) // seq, n_batch)

    return pl.pallas_call(
        functools.partial(_mm_mix_out_kernel, n_lat_blocks=n_lat_blocks, ka=ka),
        grid=(rows // tm, n // tn),
        in_specs=[
            pl.BlockSpec((tm, ka), lambda i, j: (jnp.minimum(i, n_lat_blocks - 1), 0)),
            pl.BlockSpec((tm, ka), lambda i, j: (jnp.clip(i - n_lat_blocks, 0, n_ctx_blocks - 1), 0)),
            pl.BlockSpec((tm, ks), lambda i, j: (i, 0)),
            pl.BlockSpec((None, ka + ks, tn), lambda i, j: (layer, 0, j)),
            pl.BlockSpec((tm, tn), lambda i, j: (i, j)),
            pl.BlockSpec((None, None, 1, tn), lambda i, j: (mrow(i), 3 * k + 2, 0, j)),
        ],
        out_specs=pl.BlockSpec((tm, tn), lambda i, j: (i, j)),
        out_shape=jax.ShapeDtypeStruct((rows, n), F32),
        compiler_params=_params("parallel", "parallel"),
        name="mm_mix_out",
    )(attn, attn_c, ssm, w3, res, mod4)


def _lane_group(shape, period, width):
    lane = lax.broadcasted_iota(jnp.int32, shape, 1)
    return (lane % period) // width


def _ssm_chunk_ops_kernel(x_ref, t_ref, z_ref, o_ref, lhs_ref, w_ref, *, n_t):
    T, H, GL, P = SSM_CHUNK, SSM_GROUP, GROUPS_PER_LANE_BLOCK, SSM_STATE
    c = pl.program_id(1)
    m = lhs_ref.shape[0]
    n_intra = T // n_t

    @pl.when(c == 0)
    def _():
        for s in range(T):
            lhs_ref[:, s * LANES:(s + 1) * LANES] = x_ref[pl.ds(s, m, stride=T), :].astype(BF16)

    for part in range(n_intra):
        @pl.when(c == part)
        def _(part=part):
            for s in range(T):
                for tt in range(n_t):
                    lag = part * n_t + tt - s + T - 1
                    w_ref[s * LANES:(s + 1) * LANES, tt * LANES:(tt + 1) * LANES] = t_ref[lag]

    @pl.when(c >= n_intra)
    def _():
        grp = _lane_group((H, w_ref.shape[1]), GL * P, P)
        for gl in range(GL):
            keep = grp == gl
            for s in range(T):
                r0 = s * LANES + gl * H
                w_ref[r0:r0 + H, :] = jnp.where(keep, z_ref[s * H:(s + 1) * H, :], 0.0).astype(BF16)

    o_ref[...] = jnp.dot(lhs_ref[...], w_ref[...], preferred_element_type=F32)


def _ssm_chunk_ops(u_tok, tiles, z, layer):
    T = SSM_CHUNK
    rows = u_tok.shape[0]
    nj = tiles.shape[1]
    m = rows // T
    n_t = T // 2
    tn = n_t * LANES
    n_intra = T // n_t
    assert z.shape[3] % tn == 0
    n_steps = n_intra + z.shape[3] // tn
    return pl.pallas_call(
        functools.partial(_ssm_chunk_ops_kernel, n_t=n_t),
        grid=(nj, n_steps),
        in_specs=[
            pl.BlockSpec((rows, LANES), lambda j, c: (0, j)),
            pl.BlockSpec((None, None, 2 * T - 1, LANES, LANES), lambda j, c: (layer, j, 0, 0, 0)),
            pl.BlockSpec((None, None, z.shape[2], tn), lambda j, c: (layer, j, 0, jnp.maximum(c - n_intra, 0))),
        ],
        out_specs=pl.BlockSpec((None, m, tn), lambda j, c: (j, 0, c)),
        out_shape=jax.ShapeDtypeStruct((nj, m, n_steps * tn), F32),
        scratch_shapes=[pltpu.VMEM((m, T * LANES), BF16), pltpu.VMEM((T * LANES, tn), BF16)],
        compiler_params=_params("parallel", "arbitrary"),
        name="ssm_chunk_ops",
    )(u_tok, tiles, z)


def _ssm_readout_kernel(h_ref, v_ref, add_ref, x_ref, d_ref, o_ref, w_ref):
    T, GL, P = SSM_CHUNK, GROUPS_PER_LANE_BLOCK, SSM_STATE

    @pl.when(pl.program_id(1) == 0)
    def _():
        grp = _lane_group((P, w_ref.shape[1]), LANES, SSM_GROUP)
        for gl in range(GL):
            keep = grp == gl
            for x in range(v_ref.shape[0]):
                r0 = (x * GL + gl) * P
                w_ref[r0:r0 + P, :] = jnp.where(keep, v_ref[x], 0.0).astype(BF16)

    y = add_ref[...] + jnp.dot(h_ref[...].astype(BF16), w_ref[...], preferred_element_type=F32)
    mq = y.shape[0]
    for t in range(T):
        tok = pl.ds(t, mq, stride=T)
        yt = y[:, t * LANES:(t + 1) * LANES] + d_ref[...] * x_ref[tok, :]
        o_ref[tok, :] = jax.nn.gelu(yt)


def _ssm_readout(h, v, merged, u_tok, d4, layer):
    T = SSM_CHUNK
    nj, m, kdim = h.shape
    n = v.shape[4]
    nq = next(q for q in (4, 3, 2, 1) if m % (8 * q) == 0)
    mq = m // nq
    return pl.pallas_call(
        _ssm_readout_kernel,
        grid=(nj, nq),
        in_specs=[
            pl.BlockSpec((None, mq, kdim), lambda j, q: (j, q, 0)),
            pl.BlockSpec((None, None, v.shape[2], v.shape[3], n), lambda j, q: (layer, j, 0, 0, 0)),
            pl.BlockSpec((None, mq, n), lambda j, q: (j, q, 0)),
            pl.BlockSpec((mq * T, LANES), lambda j, q: (q, j)),
            pl.BlockSpec((None, None, 1, LANES), lambda j, q: (layer, j, 0, 0)),
        ],
        out_specs=pl.BlockSpec((mq * T, LANES), lambda j, q: (q, j)),
        out_shape=jax.ShapeDtypeStruct(u_tok.shape, F32),
        scratch_shapes=[pltpu.VMEM((kdim, n), BF16)],
        compiler_params=_params("parallel", "arbitrary"),
        name="ssm_readout",
    )(h, v, merged, u_tok, d4)


def _mix_in_kernel(x_ref, w_ref, cos_ref, sin_ref, qg_ref, kg_ref, qkv_ref, u_ref, *, nq_tiles, heads):
    j = pl.program_id(1)

    def proj():
        return jnp.dot(x_ref[...], w_ref[...], preferred_element_type=F32)

    @pl.when(j <= nq_tiles)
    def _():
        is_q = j < nq_tiles
        gain = jnp.where(is_q, qg_ref[...], kg_ref[...])
        scale = jnp.where(is_q, HEAD_DIM ** -0.5 * math.log2(math.e), 1.0)
        rows = x_ref.shape[0]
        rc = min(rows, 256)
        for r0 in range(0, rows, rc):
            p = jnp.dot(x_ref[r0:r0 + rc, :], w_ref[...], preferred_element_type=F32)
            cosf = cos_ref[r0:r0 + rc, :]
            sins = sin_ref[r0:r0 + rc, :]
            lane = lax.broadcasted_iota(jnp.int32, cosf.shape, 1)
            first_half = (lane % (HEAD_DIM // 2)) < (HEAD_DIM // 4)
            for h in range(heads):
                sl = slice(h * HEAD_DIM, (h + 1) * HEAD_DIM)
                t = p[:, sl]
                r = lax.rsqrt(jnp.mean(t * t, axis=-1, keepdims=True) + EPS)
                t = (t * r) * gain
                partner = jnp.where(first_half,
                                    pltpu.roll(t, HEAD_DIM - HEAD_DIM // 4, axis=1),
                                    pltpu.roll(t, HEAD_DIM // 4, axis=1))
                t = (t * cosf + partner * sins) * scale
                qkv_ref[r0:r0 + rc, sl] = t.astype(qkv_ref.dtype)

    @pl.when(j == nq_tiles + 1)
    def _():
        qkv_ref[...] = proj().astype(qkv_ref.dtype)

    @pl.when(j > nq_tiles + 1)
    def _():
        u_ref[...] = proj().astype(u_ref.dtype)


def _mix_in(x, w3, layer, cosf, sins, qg3, kg3, g_layer, *, rows, tm, n_q, n_kv):
    kdim, n = w3.shape[1], w3.shape[2]
    tn = n_kv * HEAD_DIM
    nq_tiles = n_q // n_kv
    n_qkv_tiles = nq_tiles + 2
    n_tiles = n // tn
    return pl.pallas_call(
        functools.partial(_mix_in_kernel, nq_tiles=nq_tiles, heads=n_kv),
        grid=(rows // tm, n_tiles),
        in_specs=[
            pl.BlockSpec((tm, kdim), lambda i, j: (i, 0)),
            pl.BlockSpec((None, kdim, tn), lambda i, j: (layer, 0, j)),
            pl.BlockSpec((tm, HEAD_DIM), lambda i, j: (i, 0)),
            pl.BlockSpec((tm, HEAD_DIM), lambda i, j: (i, 0)),
            pl.BlockSpec((None, 1, HEAD_DIM), lambda i, j: (g_layer, 0, 0)),
            pl.BlockSpec((None, 1, HEAD_DIM), lambda i, j: (g_layer, 0, 0)),
        ],
        out_specs=[
            pl.BlockSpec((tm, tn), lambda i, j: (i, jnp.minimum(j, n_qkv_tiles - 1))),
            pl.BlockSpec((tm, tn), lambda i, j: (i, jnp.maximum(j - n_qkv_tiles, 0))),
        ],
        out_shape=[
            jax.ShapeDtypeStruct((rows, n_qkv_tiles * tn), BF16),
            jax.ShapeDtypeStruct((rows, n - n_qkv_tiles * tn), F32),
        ],
        compiler_params=_params("parallel", "arbitrary"),
        name="mix_in",
    )(x, w3, cosf, sins, qg3, kg3)


def _flash_kernel(q_ref, *rest, tq, kv_lens, tkv, n_cast):
    n_src = len(kv_lens)
    k_refs, v_refs = rest[:n_src], rest[n_src:2 * n_src]
    o_ref = rest[2 * n_src + n_cast]
    _run_cast_jobs(rest[2 * n_src:2 * n_src + n_cast], rest[2 * n_src + n_cast + 1:])
    q = jnp.concatenate([q_ref[:, h * HEAD_DIM:(h + 1) * HEAD_DIM] for h in range(GQA_GROUP)], axis=0)
    nrow = GQA_GROUP * tq

    def step(k, v, carry):
        m, acc = carry
        s = lax.dot_general(q, k, (((1,), (1,)), ((), ())), preferred_element_type=F32)
        m_new = jnp.maximum(m, jnp.max(s, axis=-1, keepdims=True))
        alpha = jnp.exp2(m - m_new)
        p = jnp.exp2(s - m_new).astype(BF16)
        v_ones = jnp.concatenate([v, jnp.ones_like(v)], axis=1)
        acc = alpha * acc + jnp.dot(p, v_ones, preferred_element_type=F32)
        return m_new, acc

    carry = (jnp.full((nrow, 1), -jnp.inf, F32), jnp.zeros((nrow, 2 * HEAD_DIM), F32))
    for k_ref, v_ref, length in zip(k_refs, v_refs, kv_lens):
        t = min(tkv, length)
        for c in range(length // t):
            carry = step(k_ref[c * t:(c + 1) * t, :], v_ref[c * t:(c + 1) * t, :], carry)
    _, acc = carry
    out = acc[:, :HEAD_DIM] / acc[:, HEAD_DIM:]
    for h in range(GQA_GROUP):
        o_ref[:, h * HEAD_DIM:(h + 1) * HEAD_DIM] = out[h * tq:(h + 1) * tq].astype(o_ref.dtype)


def _flash(qkv, *, n_q, n_kv, n_batch, q_row0, q_len, kv_srcs, tq, casts=()):
    qw = GQA_GROUP * HEAD_DIM
    nq_t = q_len // tq
    assert q_row0 % tq == 0
    in_specs = [pl.BlockSpec((tq, qw), lambda b, g, i: (q_row0 // tq + b * nq_t + i, g))]
    args = [qkv]
    for col0 in (n_q, n_q + n_kv):
        for (row0, length) in kv_srcs:
            assert row0 % length == 0
            in_specs.append(pl.BlockSpec(
                (length, HEAD_DIM),
                lambda b, g, i, row0=row0, length=length, col0=col0: (row0 // length + b, col0 + g)))
            args.append(qkv)
    c_in, c_args, c_out, c_shapes = _cast_jobs(casts, n_batch * n_kv * nq_t,
                                               lambda b, g, i: (b * n_kv + g) * nq_t + i)
    outs = pl.pallas_call(
        functools.partial(_flash_kernel, tq=tq, kv_lens=tuple(l for _, l in kv_srcs), tkv=256,
                          n_cast=len(casts)),
        grid=(n_batch, n_kv, nq_t),
        in_specs=in_specs + c_in,
        out_specs=[pl.BlockSpec((tq, qw), lambda b, g, i: (b * nq_t + i, g))] + c_out,
        out_shape=[jax.ShapeDtypeStruct((n_batch * q_len, n_q * HEAD_DIM), BF16)] + c_shapes,
        compiler_params=_params("arbitrary", "arbitrary", "arbitrary"),
        name="gqa_attention",
    )(*args, *c_args)
    return outs[0], tuple(outs[1:])


def _ssm_taps_kernel(z_ref, c_ref, o_ref):
    o_ref[...] = lax.dot_general(z_ref[...], c_ref[...], (((1,), (1,)), ((), ())),
                                 precision=lax.Precision.HIGHEST, preferred_element_type=F32)


def _ssm_taps(z, c):
    nj, nd, rows, p2 = z.shape
    return pl.pallas_call(
        _ssm_taps_kernel,
        grid=(nj, nd),
        in_specs=[
            pl.BlockSpec((None, None, rows, p2), lambda j, d: (j, d, 0, 0)),
            pl.BlockSpec((None, None, LANES, p2), lambda j, d: (j, d, 0, 0)),
        ],
        out_specs=pl.BlockSpec((None, None, rows, LANES), lambda j, d: (j, d, 0, 0)),
        out_shape=jax.ShapeDtypeStruct((nj, nd, rows, LANES), F32),
        compiler_params=_params("parallel", "parallel"),
        name="ssm_taps",
    )(z, c)


def _ssm_scan_kernel(s_ref, p_ref, o_ref, *, n_batch, lat_tiles, ctx_tiles, half):
    row = lax.broadcasted_iota(jnp.int32, (8, half), 0)
    lat_rows = n_batch * lat_tiles * 8

    def run(d, col0, reverse):
        pr, pi = p_ref[2 * d], p_ref[2 * d + 1]
        ent_r, ent_i = pr[0:8], pi[0:8]
        steps = [(k, pr[8 + n:9 + n], pi[8 + n:9 + n]) for n, k in enumerate((1, 2, 4))]
        a8_r, a8_i = pr[11:12], pi[11:12]

        def shift(x, k):
            if reverse:
                return jnp.where(row < 8 - k, pltpu.roll(x, 8 - k, axis=0), 0.0)
            return jnp.where(row >= k, pltpu.roll(x, k, axis=0), 0.0)

        def tile(r0, h):
            hr, hi = h
            ir = s_ref[pl.ds(r0, 8), col0:col0 + half]
            ii = s_ref[pl.ds(r0, 8), col0 + half:col0 + 2 * half]
            for k, ar, ai in steps:
                tr, ti = shift(ir, k), shift(ii, k)
                ir, ii = ir + ar * tr - ai * ti, ii + ar * ti + ai * tr
            o_ref[pl.ds(r0, 8), col0:col0 + half] = ent_r * hr - ent_i * hi + shift(ir, 1)
            o_ref[pl.ds(r0, 8), col0 + half:col0 + 2 * half] = ent_r * hi + ent_i * hr + shift(ii, 1)
            nr = a8_r * hr - a8_i * hi + ir
            ni = a8_r * hi + a8_i * hr + ii
            last = 0 if reverse else 7
            return (jnp.broadcast_to(nr[last:last + 1], (8, half)), jnp.broadcast_to(ni[last:last + 1], (8, half)))

        def phase(base, n_tiles, hs):
            def body(i, hs):
                t = (n_tiles - 1 - i) if reverse else i
                return tuple(tile(pl.multiple_of(base + (b * n_tiles + t) * 8, 8), hs[b]) for b in range(n_batch))
            return lax.fori_loop(0, n_tiles, body, hs)

        zero = jnp.zeros((8, half), F32)
        hs = phase(lat_rows, ctx_tiles, tuple((zero, zero) for _ in range(n_batch)))
        phase(0, lat_tiles, hs)

    run(0, 0, False)
    run(1, 2 * half, True)


def _ssm_scan(merged, pw, layer, *, n_batch, lat_tiles, ctx_tiles):
    nj, m, _ = merged.shape
    half = pw.shape[4]
    w = 4 * half
    assert merged.shape[2] == 2 * w and m == n_batch * (lat_tiles + ctx_tiles) * 8
    return pl.pallas_call(
        functools.partial(_ssm_scan_kernel, n_batch=n_batch, lat_tiles=lat_tiles, ctx_tiles=ctx_tiles, half=half),
        grid=(nj,),
        in_specs=[
            pl.BlockSpec((None, m, w), lambda j: (j, 0, 1)),
            pl.BlockSpec((None, None, 4, 16, half), lambda j: (layer, j, 0, 0, 0)),
        ],
        out_specs=pl.BlockSpec((None, m, w), lambda j: (j, 0, 0)),
        out_shape=jax.ShapeDtypeStruct((nj, m, w), F32),
        compiler_params=_params("parallel"),
        name="ssm_scan",
    )(merged, pw)


def _ssm_weights(lam_re, lam_im, log_dt, b_re, b_im, c_re, c_im):
    T, H, P = SSM_CHUNK, SSM_GROUP, SSM_STATE
    G = lam_re.shape[1]
    GL = GROUPS_PER_LANE_BLOCK
    J = G // GL
    lam_re, lam_im = lam_re.astype(F32), lam_im.astype(F32)
    dt = jnp.exp(log_dt.astype(F32))[..., None]
    mag = jnp.exp(lam_re * dt)
    ab_r, ab_i = mag * jnp.cos(lam_im * dt), mag * jnp.sin(lam_im * dt)
    nr, ni = ab_r - 1.0, ab_i
    den = lam_re * lam_re + lam_im * lam_im
    cr = (nr * lam_re + ni * lam_im) / den
    ci = (ni * lam_re - nr * lam_im) / den
    b_re, b_im = b_re.astype(F32), b_im.astype(F32)
    bb_r = cr[..., None] * b_re - ci[..., None] * b_im
    bb_i = cr[..., None] * b_im + ci[..., None] * b_re
    c_re, c_im = c_re.astype(F32), c_im.astype(F32)

    def powers(d, exps):
        e = jnp.asarray(exps, F32)[:, None, None]
        pmag = jnp.exp(e * (lam_re[d] * dt[d])[None])
        pang = e * (lam_im[d] * dt[d])[None]
        return pmag * jnp.cos(pang), pmag * jnp.sin(pang)

    def drive(d, exps):
        pr, pi = powers(d, exps)
        return (pr[..., None] * bb_r[d][None] - pi[..., None] * bb_i[d][None],
                pr[..., None] * bb_i[d][None] + pi[..., None] * bb_r[d][None])

    asc = list(range(T))
    desc = [T - 1 - s for s in range(T)]

    def tap_lhs(d, exps):
        zr, zi = drive(d, exps)
        z = jnp.concatenate([zr, zi], axis=2).reshape(T, J, GL, 2 * P, H)
        return z.transpose(1, 0, 2, 4, 3).reshape(J, T * LANES, 2 * P)
    z_all = jnp.stack([tap_lhs(0, asc), tap_lhs(1, desc)], axis=1)
    c_all = jnp.concatenate([c_re, -c_im], axis=-1).reshape(2, J, LANES, 2 * P).transpose(1, 0, 2, 3)

    def summary(d, exps):
        pr, pi = powers(d, exps)
        pr, pi = (a.reshape(T, J, 1, GL * P).transpose(1, 0, 2, 3) for a in (pr, pi))
        br, bi = (a.reshape(J, GL, P, H).transpose(0, 3, 1, 2).reshape(J, 1, H, GL * P)
                  for a in (bb_r[d], bb_i[d]))
        return jnp.concatenate([pr * br - pi * bi, pr * bi + pi * br], axis=-1).reshape(J, T * H, 2 * GL * P)
    z_sum = jnp.concatenate([summary(0, desc), summary(1, asc)], axis=-1)

    def readout(d, exps):
        pr, pi = powers(d, exps)
        pr, pi = (jnp.repeat(a.reshape(T, J, GL, P).transpose(1, 3, 0, 2), H, axis=-1)
                  for a in (pr, pi))
        cr_l, ci_l = (a.reshape(J, GL, H, P).transpose(0, 3, 1, 2).reshape(J, P, 1, LANES)
                      for a in (c_re[d], c_im[d]))
        vr = (cr_l * pr - ci_l * pi).reshape(J, 1, P, T * LANES)
        vi = (cr_l * pi + ci_l * pr).reshape(J, 1, P, T * LANES)
        return jnp.concatenate([vr, -vi], axis=1)
    v_out = jnp.concatenate([readout(0, [t + 1 for t in range(T)]), readout(1, [T - t for t in range(T)])],
                            axis=1)

    def scan_powers(d, entry):
        exps = [T * k for k in entry] + [T, 2 * T, 4 * T, 8 * T] + [0] * 4
        return jnp.stack([a.reshape(16, J, GL * P).transpose(1, 0, 2) for a in powers(d, exps)], axis=1)
    pw = jnp.concatenate([scan_powers(0, list(range(8))), scan_powers(1, [7 - r for r in range(8)])], axis=1)
    return z_all, c_all, z_sum, v_out, pw


def _ssm_tap_tiles(raw):
    T, H = SSM_CHUNK, SSM_GROUP
    raw = raw.reshape(raw.shape[0], 2, T, LANES, LANES)
    fwd, bwd = raw[:, 0], raw[:, 1]
    tiles = jnp.concatenate([bwd[:, :T - 1], bwd[:, T - 1:] + fwd[:, :1], fwd[:, 1:]], axis=1)
    r = lax.broadcasted_iota(jnp.int32, (LANES, LANES), 0) // H
    c = lax.broadcasted_iota(jnp.int32, (LANES, LANES), 1) // H
    return jnp.where(r == c, tiles, 0.0).astype(BF16)


def _ssm_weights_all(*params):
    z_all, c_all, z_sum, v_out, pw = jax.vmap(_ssm_weights)(*params)
    n_layers, nj = z_all.shape[:2]
    raw = _ssm_taps(z_all.reshape((n_layers * nj,) + z_all.shape[2:]), c_all.reshape((n_layers * nj,) + c_all.shape[2:]))
    tiles = jax.vmap(_ssm_tap_tiles)(raw.reshape((n_layers, nj) + raw.shape[1:]))
    return tiles, z_sum, v_out, pw


def _s5_mixer(u, d4, layer, n_batch, seq, n_ctx, weights):
    tiles, z_sum, v_out, pw = weights
    T = SSM_CHUNK
    merged = _ssm_chunk_ops(u, tiles, z_sum, layer)
    h_in = _ssm_scan(merged, pw, layer, n_batch=n_batch, lat_tiles=seq // (8 * T), ctx_tiles=n_ctx // (8 * T))
    return _ssm_readout(h_in, v_out, merged, u, d4, layer)


def _rope_tables(seq, n_batch, n_ctx_rows):
    rows_n = seq // GRID_W
    pos = jnp.arange(seq)
    rows = (pos // GRID_W).astype(F32)
    cols = (pos % GRID_W).astype(F32)
    del rows_n
    axis_dim = HEAD_DIM // 2
    inv_freq = ROPE_THETA ** (-jnp.arange(0, axis_dim, 2, dtype=F32) / axis_dim)
    ang_r, ang_c = rows[:, None] * inv_freq, cols[:, None] * inv_freq
    cosf = jnp.concatenate([jnp.cos(ang_r)] * 2 + [jnp.cos(ang_c)] * 2, axis=1)
    sins = jnp.concatenate([-jnp.sin(ang_r), jnp.sin(ang_r), -jnp.sin(ang_c), jnp.sin(ang_c)], axis=1)
    cosf = jnp.concatenate([jnp.tile(cosf, (n_batch, 1)), jnp.ones((n_ctx_rows, HEAD_DIM), F32)], axis=0)
    sins = jnp.concatenate([jnp.tile(sins, (n_batch, 1)), jnp.zeros((n_ctx_rows, HEAD_DIM), F32)], axis=0)
    return cosf, sins


def kernel(x, c, ctx, c_ctx, w_ada, b_ada, norm_ffn1, ffn1_w_in, ffn1_w_out, norm_mix, w_mix_in, q_norm, k_norm, ssm_lam_re, ssm_lam_im, ssm_log_dt, ssm_b_re, ssm_b_im, ssm_c_re, ssm_c_im, ssm_d, w_glu, b_glu, w_mix_out, norm_ffn2, ffn2_w_in, ffn2_w_out, norm_final):
    n_batch, seq, d = x.shape
    n_ctx = ctx.shape[1]
    depth = w_ada.shape[0]
    ssm_width = w_glu.shape[1]
    attn_width = w_mix_out.shape[1] - ssm_width
    n_q = attn_width // HEAD_DIM
    n_kv = n_q // GQA_GROUP
    n_lat = n_batch * seq
    n_ctx_rows = n_batch * n_ctx
    n_all = n_lat + n_ctx_rows
    tm = n_ctx_rows
    tr = _tile(tm, 512, 8)
    assert seq % tm == 0 and n_batch == 4

    h, h_tail = x.reshape(n_lat, d), ctx.reshape(n_ctx_rows, d)
    cc =jnp.concatenate([c, c_ctx[None], jnp.zeros((8 - n_batch - 1, d), F32)], axis=0)
    cosf, sins = _rope_tables(seq, n_batch, n_ctx_rows)

    def row3(a):
        return a.reshape(a.shape[0], 1, a.shape[1])

    b_ada3 = row3(b_ada)
    g_ffn1, g_mix, g_ffn2 = row3(norm_ffn1), row3(norm_mix), row3(norm_ffn2)
    qg3, kg3, bglu3 = row3(q_norm), row3(k_norm), row3(b_glu)
    d4 = ssm_d.astype(F32).reshape(depth, ssm_width // LANES, 1, LANES)
    def ffn(h, mod4, k, gain3, layer, rows, w_in, w_out_f32, casts, h_tail=None):
        if h_tail is None:
            hn = _norm(h, gain3, layer, rows=rows, tr=tr, out_dtype=BF16, mod4=mod4, k=k, seq=seq, n_batch=n_batch)
        else:
            hn, h = _norm(h, gain3, layer, rows=rows, tr=tr // 2, out_dtype=BF16, mod4=mod4, k=k, seq=seq,
                          n_batch=n_batch, x_tail=h_tail)
        hid, cast = _mm_swiglu(hn, w_in, 0, rows=rows, tm=tm, tn=_tile(w_in.shape[2] // 2, 512),
                               casts=[(w_out_f32, layer)] + casts)
        h = _mm_resid(hid, cast[0], 0, h, mod4, k, 0.5, rows=rows, tm=tm, tn=_tile(d, 1024),
                      tk=_tile(w_out_f32.shape[1], 4096), seq=seq, n_batch=n_batch)
        return h, cast[1:]

    weights = _ssm_weights_all(ssm_lam_re, ssm_lam_im, ssm_log_dt, ssm_b_re, ssm_b_im, ssm_c_re, ssm_c_im)

    ffn1_in = ffn1_w_in[:1].astype(BF16)
    for layer in range(depth):
        last = layer == depth - 1
        mod4 = _mod_table(cc, w_ada, b_ada3, layer).reshape(8, 9, 1, d)

        h, (mix_in, glu, mix_out) = ffn(h, mod4, 0, g_ffn1, layer, n_all, ffn1_in, ffn1_w_out,
                                        [(w_mix_in, layer), (w_glu, layer), (w_mix_out, layer)], h_tail=h_tail)
        h_tail = None

        hn = _norm(h, g_mix, layer, rows=n_all, tr=tr, out_dtype=BF16, mod4=mod4, k=1, seq=seq, n_batch=n_batch)
        qkv, u = _mix_in(hn, mix_in, 0, cosf, sins, qg3, kg3, layer, rows=n_all, tm=tm, n_q=n_q, n_kv=n_kv)
        attn, (ffn2_in,) = _flash(qkv, n_q=n_q, n_kv=n_kv, n_batch=n_batch, q_row0=0, q_len=seq,
                                  kv_srcs=[(0, seq), (n_lat, n_ctx)], tq=min(512, seq), casts=[(ffn2_w_in, layer)])
        out_rows = n_lat if last else n_all
        attn_c = None
        if not last:
            attn_c, _ = _flash(qkv, n_q=n_q, n_kv=n_kv, n_batch=n_batch, q_row0=n_lat, q_len=n_ctx,
                               kv_srcs=[(n_lat, n_ctx)], tq=min(128, n_ctx))

        yg = _s5_mixer(u, d4, layer, n_batch, seq, n_ctx, weights)
        ssm_out = _mm_glu(yg, glu, bglu3, 0, layer, rows=out_rows, tm=tm, tn=_tile(ssm_width, 1024))

        h = _mm_mix_out(attn, attn_c, ssm_out, mix_out, 0, h, mod4, 1, rows=out_rows, tm=tm,
                        tn=_tile(d, 512), seq=seq, n_batch=n_batch)
        h, next_in = ffn(h, mod4, 2, g_ffn2, layer, out_rows, ffn2_in, ffn2_w_out,
                         [] if last else [(ffn1_w_in, layer + 1)])
        if not last:
            ffn1_in = next_in[0]

    out = _norm(h, row3(norm_final[None]), 0, rows=n_lat, tr=tr, out_dtype=F32)
    return out.reshape(n_batch, seq, d)
```

```python
import functools
import math

import jax
import jax.numpy as jnp
from jax import lax
from jax.experimental import pallas as pl
from jax.experimental.pallas import tpu as pltpu

F32 = jnp.float32
BF16 = jnp.bfloat16

HEAD_DIM = 128
GQA_GROUP = 4
GRID_W = 64
ROPE_THETA = 10000.0
SSM_GROUP = 16
SSM_STATE = 64
SSM_CHUNK = 16
LANES = 128
GROUPS_PER_LANE_BLOCK = LANES // SSM_GROUP
EPS = 1e-6
VMEM_LIMIT_BYTES = 56 * 1024 * 1024


def _params(*sem):
    return pltpu.CompilerParams(dimension_semantics=sem, vmem_limit_bytes=VMEM_LIMIT_BYTES)


def _tile(n, pref, mult=LANES):
    if n <= pref:
        return n
    t = (pref // mult) * mult
    while t > mult and n % t:
        t -= mult
    assert n % t == 0, (n, pref, mult)
    return t


def _mod_kernel(c_ref, w_ref, b_ref, o_ref):
    c = c_ref[...]
    s = (c * jax.nn.sigmoid(c)).astype(BF16)
    o_ref[...] = jnp.dot(s, w_ref[...].astype(BF16), preferred_element_type=F32) + b_ref[...]


def _mod_table(cc, w_ada, b_ada3, layer):
    rows, d = cc.shape
    n = w_ada.shape[2]
    tn = _tile(n, 512)
    return pl.pallas_call(
        _mod_kernel,
        grid=(n // tn,),
        in_specs=[
            pl.BlockSpec((rows, d), lambda j: (0, 0)),
            pl.BlockSpec((None, d, tn), lambda j: (layer, 0, j)),
            pl.BlockSpec((None, 1, tn), lambda j: (layer, 0, j)),
        ],
        out_specs=pl.BlockSpec((rows, tn), lambda j: (0, j)),
        out_shape=jax.ShapeDtypeStruct((rows, n), F32),
        compiler_params=_params("parallel"),
        name="mod_table",
    )(cc, w_ada, b_ada3)


def _norm_kernel(x_ref, *rest, modulate, n_head_blocks):
    o_ref = rest[-1]
    if n_head_blocks is not None:
        tail_ref, rest = rest[0], rest[1:]
        o_ref, merged_ref = rest[-2], rest[-1]
        from_head = pl.program_id(0) < n_head_blocks
    g_ref = rest[0]
    gain = g_ref[...]
    if modulate:
        sh_ref, sc_ref = rest[1], rest[2]
        gain = gain * (1.0 + sc_ref[...])
        shift = sh_ref[...]

    rc = 16

    def chunk(i, carry):
        rows = pl.ds(pl.multiple_of(i * rc, rc), rc)
        x = x_ref[rows, :]
        if n_head_blocks is not None:
            x = jnp.where(from_head, x, tail_ref[rows, :])
            merged_ref[rows, :] = x
        r = lax.rsqrt(jnp.mean(x * x, axis=-1, keepdims=True) + EPS)
        h = (x * r) * gain
        if modulate:
            h = h + shift
        o_ref[rows, :] = h.astype(o_ref.dtype)
        return carry

    lax.fori_loop(0, x_ref.shape[0] // rc, chunk, 0, unroll=True)


def _two_source_specs(head, tail, block, col_of):
    nh, nt = head.shape[0] // block[0], tail.shape[0] // block[0]
    return [
        pl.BlockSpec(block, lambda i, *g: (jnp.minimum(i, nh - 1), col_of(*g))),
        pl.BlockSpec(block, lambda i, *g: (jnp.clip(i - nh, 0, nt - 1), col_of(*g))),
    ], nh


def _norm(x, gain3, layer, *, rows, tr, out_dtype, mod4=None, k=None, seq=None, n_batch=None, x_tail=None):
    d = x.shape[1]
    modulate = mod4 is not None
    n_head_blocks = None
    if x_tail is None:
        in_specs, args = [pl.BlockSpec((tr, d), lambda i: (i, 0))], [x]
    else:
        in_specs, n_head_blocks = _two_source_specs(x, x_tail, (tr, d), lambda: 0)
        args = [x, x_tail]
    in_specs.append(pl.BlockSpec((None, 1, d), lambda i: (layer, 0, 0)))
    args.append(gain3)
    if modulate:
        def mrow(i):
            return jnp.minimum((i * tr) // seq, n_batch)
        in_specs += [
            pl.BlockSpec((None, None, 1, d), lambda i: (mrow(i), 3 * k, 0, 0)),
            pl.BlockSpec((None, None, 1, d), lambda i: (mrow(i), 3 * k + 1, 0, 0)),
        ]
        args += [mod4, mod4]
    out_specs = [pl.BlockSpec((tr, d), lambda i: (i, 0))]
    out_shape = [jax.ShapeDtypeStruct((rows, d), out_dtype)]
    if x_tail is not None:
        out_specs.append(pl.BlockSpec((tr, d), lambda i: (i, 0)))
        out_shape.append(jax.ShapeDtypeStruct((rows, d), x.dtype))
    outs = pl.pallas_call(
        functools.partial(_norm_kernel, modulate=modulate, n_head_blocks=n_head_blocks),
        grid=(rows // tr,),
        in_specs=in_specs,
        out_specs=out_specs,
        out_shape=out_shape,
        compiler_params=_params("parallel"),
        name="rmsnorm_mod" if modulate else "rmsnorm",
    )(*args)
    return outs[0] if x_tail is None else tuple(outs)


def _cast_jobs(jobs, n_steps, step_of):
    in_specs, args, out_specs, out_shapes = [], [], [], []
    for w3, layer in jobs:
        kdim, n = w3.shape[1:]
        bk = 16
        while kdim % bk or kdim // bk > n_steps:
            bk += 16
        nb = kdim // bk
        in_specs.append(pl.BlockSpec(
            (None, bk, n), lambda *g, layer=layer, nb=nb: (layer, jnp.minimum(step_of(*g), nb - 1), 0)))
        out_specs.append(pl.BlockSpec((None, bk, n), lambda *g, nb=nb: (0, jnp.minimum(step_of(*g), nb - 1), 0)))
        out_shapes.append(jax.ShapeDtypeStruct((1, kdim, n), BF16))
        args.append(w3)
    return in_specs, args, out_specs, out_shapes


def _run_cast_jobs(cast_in, cast_out):
    for src, dst in zip(cast_in, cast_out):
        dst[...] = src[...].astype(dst.dtype)


def _mm_swiglu_kernel(x_ref, wg_ref, wu_ref, *rest, n_cast):
    o_ref = rest[n_cast]
    x = x_ref[...]
    g = jnp.dot(x, wg_ref[...], preferred_element_type=F32)
    u = jnp.dot(x, wu_ref[...], preferred_element_type=F32)
    o_ref[...] = (g * jax.nn.sigmoid(g) * u).astype(o_ref.dtype)
    _run_cast_jobs(rest[:n_cast], rest[n_cast + 1:])


def _mm_swiglu(x, w3, layer, *, rows, tm, tn, casts=()):
    kdim, f = w3.shape[1], w3.shape[2] // 2
    nj = f // tn
    grid = (rows // tm, nj)
    c_in, c_args, c_out, c_shapes = _cast_jobs(casts, grid[0] * grid[1], lambda i, j: i * nj + j)
    outs = pl.pallas_call(
        functools.partial(_mm_swiglu_kernel, n_cast=len(casts)),
        grid=grid,
        in_specs=[
            pl.BlockSpec((tm, kdim), lambda i, j: (i, 0)),
            pl.BlockSpec((None, kdim, tn), lambda i, j: (layer, 0, j)),
            pl.BlockSpec((None, kdim, tn), lambda i, j: (layer, 0, j + nj)),
        ] + c_in,
        out_specs=[pl.BlockSpec((tm, tn), lambda i, j: (i, j))] + c_out,
        out_shape=[jax.ShapeDtypeStruct((rows, f), BF16)] + c_shapes,
        compiler_params=_params("arbitrary", "arbitrary"),
        name="mm_swiglu",
    )(x, w3, w3, *c_args)
    return outs[0], tuple(outs[1:])


def _mm_resid_kernel(x_ref, w_ref, gate_ref, res_ref, o_ref, *, nk, coef):
    def part():
        return (coef * gate_ref[...]) * jnp.dot(x_ref[...], w_ref[...], preferred_element_type=F32)

    if nk == 1:
        o_ref[...] = res_ref[...] + part()
    else:
        @pl.when(pl.program_id(2) == 0)
        def _():
            o_ref[...] = res_ref[...] + part()

        @pl.when(pl.program_id(2) != 0)
        def _():
            o_ref[...] += part()


def _mm_resid(x, w3, layer, res, mod4, k, coef, *, rows, tm, tn, tk, seq, n_batch):
    kdim, n = w3.shape[1], w3.shape[2]
    nk = kdim // tk

    def mrow(i):
        return jnp.minimum((i * tm) // seq, n_batch)

    return pl.pallas_call(
        functools.partial(_mm_resid_kernel, nk=nk, coef=coef),
        grid=(rows // tm, n // tn, nk),
        in_specs=[
            pl.BlockSpec((tm, tk), lambda i, j, kk: (i, kk)),
            pl.BlockSpec((None, tk, tn), lambda i, j, kk: (layer, kk, j)),
            pl.BlockSpec((None, None, 1, tn), lambda i, j, kk: (mrow(i), 3 * k + 2, 0, j)),
            pl.BlockSpec((tm, tn), lambda i, j, kk: (i, j)),
        ],
        out_specs=pl.BlockSpec((tm, tn), lambda i, j, kk: (i, j)),
        out_shape=jax.ShapeDtypeStruct((rows, n), F32),
        compiler_params=_params("parallel", "parallel", "arbitrary"),
        name="mm_resid",
    )(x, w3, mod4, res)


def _mm_glu_kernel(y_ref, w_ref, b_ref, yj_ref, o_ref):
    z = jnp.dot(y_ref[...].astype(BF16), w_ref[...], preferred_element_type=F32) + b_ref[...]
    o_ref[...] = (yj_ref[...] * jax.nn.sigmoid(z)).astype(o_ref.dtype)


def _mm_glu(y, w3, b3, w_layer, b_layer, *, rows, tm, tn):
    kdim, n = w3.shape[1], w3.shape[2]
    return pl.pallas_call(
        _mm_glu_kernel,
        grid=(rows // tm, n // tn),
        in_specs=[
            pl.BlockSpec((tm, kdim), lambda i, j: (i, 0)),
            pl.BlockSpec((None, kdim, tn), lambda i, j: (w_layer, 0, j)),
            pl.BlockSpec((None, 1, tn), lambda i, j: (b_layer, 0, j)),
            pl.BlockSpec((tm, tn), lambda i, j: (i, j)),
        ],
        out_specs=pl.BlockSpec((tm, tn), lambda i, j: (i, j)),
        out_shape=jax.ShapeDtypeStruct((rows, n), BF16),
        compiler_params=_params("parallel", "parallel"),
        name="mm_glu",
    )(y, w3, b3, y)


def _mm_mix_out_kernel(a_ref, ac_ref, s_ref, w_ref, res_ref, gate_ref, o_ref, *, n_lat_blocks, ka):
    a = jnp.where(pl.program_id(0) < n_lat_blocks, a_ref[...], ac_ref[...])
    total = jnp.dot(a, w_ref[:ka, :], preferred_element_type=F32)
    total += jnp.dot(s_ref[...], w_ref[ka:, :], preferred_element_type=F32)
    o_ref[...] = res_ref[...] + gate_ref[...] * total


def _mm_mix_out(attn, attn_c, ssm, w3, layer, res, mod4, k, *, rows, tm, tn, seq, n_batch):
    ka, ks = attn.shape[1], ssm.shape[1]
    n = w3.shape[2]
    n_lat_blocks = attn.shape[0] // tm
    if attn_c is None:
        attn_c = attn
    n_ctx_blocks = attn_c.shape[0] // tm

    def mrow(i):
        return jnp.minimum((i * tm) // seq, n_batch)

    return pl.pallas_call(
        functools.partial(_mm_mix_out_kernel, n_lat_blocks=n_lat_blocks, ka=ka),
        grid=(rows // tm, n // tn),
        in_specs=[
            pl.BlockSpec((tm, ka), lambda i, j: (jnp.minimum(i, n_lat_blocks - 1), 0)),
            pl.BlockSpec((tm, ka), lambda i, j: (jnp.clip(i - n_lat_blocks, 0, n_ctx_blocks - 1), 0)),
            pl.BlockSpec((tm, ks), lambda i, j: (i, 0)),
            pl.BlockSpec((None, ka + ks, tn), lambda i, j: (layer, 0, j)),
            pl.BlockSpec((tm, tn), lambda i, j: (i, j)),
            pl.BlockSpec((None, None, 1, tn), lambda i, j: (mrow(i), 3 * k + 2, 0, j)),
        ],
        out_specs=pl.BlockSpec((tm, tn), lambda i, j: (i, j)),
        out_shape=jax.ShapeDtypeStruct((rows, n), F32),
        compiler_params=_params("parallel", "parallel"),
        name="mm_mix_out",
    )(attn, attn_c, ssm, w3, res, mod4)


def _lane_group(shape, period, width):
    lane = lax.broadcasted_iota(jnp.int32, shape, 1)
    return (lane % period) // width


def _ssm_chunk_ops_kernel(x_ref, t_ref, z_ref, o_ref, lhs_ref, w_ref, *, n_t):
    T, H, GL, P = SSM_CHUNK, SSM_GROUP, GROUPS_PER_LANE_BLOCK, SSM_STATE
    c = pl.program_id(1)
    m = lhs_ref.shape[0]
    n_intra = T // n_t

    @pl.when(c == 0)
    def _():
        for s in range(T):
            lhs_ref[:, s * LANES:(s + 1) * LANES] = x_ref[pl.ds(s, m, stride=T), :].astype(BF16)

    for part in range(n_intra):
        @pl.when(c == part)
        def _(part=part):
            for s in range(T):
                for tt in range(n_t):
                    lag = part * n_t + tt - s + T - 1
                    w_ref[s * LANES:(s + 1) * LANES, tt * LANES:(tt + 1) * LANES] = t_ref[lag]

    @pl.when(c >= n_intra)
    def _():
        grp = _lane_group((H, w_ref.shape[1]), GL * P, P)
        for gl in range(GL):
            keep = grp == gl
            for s in range(T):
                r0 = s * LANES + gl * H
                w_ref[r0:r0 + H, :] = jnp.where(keep, z_ref[s * H:(s + 1) * H, :], 0.0).astype(BF16)

    o_ref[...] = jnp.dot(lhs_ref[...], w_ref[...], preferred_element_type=F32)


def _ssm_chunk_ops(u_tok, tiles, z, layer):
    T = SSM_CHUNK
    rows = u_tok.shape[0]
    nj = tiles.shape[1]
    m = rows // T
    n_t = T // 2
    tn = n_t * LANES
    n_intra = T // n_t
    assert z.shape[3] % tn == 0
    n_steps = n_intra + z.shape[3] // tn
    return pl.pallas_call(
        functools.partial(_ssm_chunk_ops_kernel, n_t=n_t),
        grid=(nj, n_steps),
        in_specs=[
            pl.BlockSpec((rows, LANES), lambda j, c: (0, j)),
            pl.BlockSpec((None, None, 2 * T - 1, LANES, LANES), lambda j, c: (layer, j, 0, 0, 0)),
            pl.BlockSpec((None, None, z.shape[2], tn), lambda j, c: (layer, j, 0, jnp.maximum(c - n_intra, 0))),
        ],
        out_specs=pl.BlockSpec((None, m, tn), lambda j, c: (j, 0, c)),
        out_shape=jax.ShapeDtypeStruct((nj, m, n_steps * tn), F32),
        scratch_shapes=[pltpu.VMEM((m, T * LANES), BF16), pltpu.VMEM((T * LANES, tn), BF16)],
        compiler_params=_params("parallel", "arbitrary"),
        name="ssm_chunk_ops",
    )(u_tok, tiles, z)


def _ssm_readout_kernel(h_ref, v_ref, add_ref, x_ref, d_ref, o_ref, w_ref):
    T, GL, P = SSM_CHUNK, GROUPS_PER_LANE_BLOCK, SSM_STATE

    @pl.when(pl.program_id(1) == 0)
    def _():
        grp = _lane_group((P, w_ref.shape[1]), LANES, SSM_GROUP)
        for gl in range(GL):
            keep = grp == gl
            for x in range(v_ref.shape[0]):
                r0 = (x * GL + gl) * P
                w_ref[r0:r0 + P, :] = jnp.where(keep, v_ref[x], 0.0).astype(BF16)

    y = add_ref[...] + jnp.dot(h_ref[...].astype(BF16), w_ref[...], preferred_element_type=F32)
    mq = y.shape[0]
    for t in range(T):
        tok = pl.ds(t, mq, stride=T)
        yt = y[:, t * LANES:(t + 1) * LANES] + d_ref[...] * x_ref[tok, :]
        o_ref[tok, :] = jax.nn.gelu(yt)


def _ssm_readout(h, v, merged, u_tok, d4, layer):
    T = SSM_CHUNK
    nj, m, kdim = h.shape
    n = v.shape[4]
    nq = next(q for q in (4, 3, 2, 1) if m % (8 * q) == 0)
    mq = m // nq
    return pl.pallas_call(
        _ssm_readout_kernel,
        grid=(nj, nq),
        in_specs=[
            pl.BlockSpec((None, mq, kdim), lambda j, q: (j, q, 0)),
            pl.BlockSpec((None, None, v.shape[2], v.shape[3], n), lambda j, q: (layer, j, 0, 0, 0)),
            pl.BlockSpec((None, mq, n), lambda j, q: (j, q, 0)),
            pl.BlockSpec((mq * T, LANES), lambda j, q: (q, j)),
            pl.BlockSpec((None, None, 1, LANES), lambda j, q: (layer, j, 0, 0)),
        ],
        out_specs=pl.BlockSpec((mq * T, LANES), lambda j, q: (q, j)),
        out_shape=jax.ShapeDtypeStruct(u_tok.shape, F32),
        scratch_shapes=[pltpu.VMEM((kdim, n), BF16)],
        compiler_params=_params("parallel", "arbitrary"),
        name="ssm_readout",
    )(h, v, merged, u_tok, d4)


def _mix_in_kernel(x_ref, w_ref, cos_ref, sin_ref, qg_ref, kg_ref, qkv_ref, u_ref, *, nq_tiles, heads):
    j = pl.program_id(1)

    def proj():
        return jnp.dot(x_ref[...], w_ref[...], preferred_element_type=F32)

    @pl.when(j <= nq_tiles)
    def _():
        is_q = j < nq_tiles
        gain = jnp.where(is_q, qg_ref[...], kg_ref[...])
        scale = jnp.where(is_q, HEAD_DIM ** -0.5 * math.log2(math.e), 1.0)
        rows = x_ref.shape[0]
        rc = min(rows, 256)
        for r0 in range(0, rows, rc):
            p = jnp.dot(x_ref[r0:r0 + rc, :], w_ref[...], preferred_element_type=F32)
            cosf = cos_ref[r0:r0 + rc, :]
            sins = sin_ref[r0:r0 + rc, :]
            lane = lax.broadcasted_iota(jnp.int32, cosf.shape, 1)
            first_half = (lane % (HEAD_DIM // 2)) < (HEAD_DIM // 4)
            for h in range(heads):
                sl = slice(h * HEAD_DIM, (h + 1) * HEAD_DIM)
                t = p[:, sl]
                r = lax.rsqrt(jnp.mean(t * t, axis=-1, keepdims=True) + EPS)
                t = (t * r) * gain
                partner = jnp.where(first_half,
                                    pltpu.roll(t, HEAD_DIM - HEAD_DIM // 4, axis=1),
                                    pltpu.roll(t, HEAD_DIM // 4, axis=1))
                t = (t * cosf + partner * sins) * scale
                qkv_ref[r0:r0 + rc, sl] = t.astype(qkv_ref.dtype)

    @pl.when(j == nq_tiles + 1)
    def _():
        qkv_ref[...] = proj().astype(qkv_ref.dtype)

    @pl.when(j > nq_tiles + 1)
    def _():
        u_ref[...] = proj().astype(u_ref.dtype)


def _mix_in(x, w3, layer, cosf, sins, qg3, kg3, g_layer, *, rows, tm, n_q, n_kv):
    kdim, n = w3.shape[1], w3.shape[2]
    tn = n_kv * HEAD_DIM
    nq_tiles = n_q // n_kv
    n_qkv_tiles = nq_tiles + 2
    n_tiles = n // tn
    return pl.pallas_call(
        functools.partial(_mix_in_kernel, nq_tiles=nq_tiles, heads=n_kv),
        grid=(rows // tm, n_tiles),
        in_specs=[
            pl.BlockSpec((tm, kdim), lambda i, j: (i, 0)),
            pl.BlockSpec((None, kdim, tn), lambda i, j: (layer, 0, j)),
            pl.BlockSpec((tm, HEAD_DIM), lambda i, j: (i, 0)),
            pl.BlockSpec((tm, HEAD_DIM), lambda i, j: (i, 0)),
            pl.BlockSpec((None, 1, HEAD_DIM), lambda i, j: (g_layer, 0, 0)),
            pl.BlockSpec((None, 1, HEAD_DIM), lambda i, j: (g_layer, 0, 0)),
        ],
        out_specs=[
            pl.BlockSpec((tm, tn), lambda i, j: (i, jnp.minimum(j, n_qkv_tiles - 1))),
            pl.BlockSpec((tm, tn), lambda i, j: (i, jnp.maximum(j - n_qkv_tiles, 0))),
        ],
        out_shape=[
            jax.ShapeDtypeStruct((rows, n_qkv_tiles * tn), BF16),
            jax.ShapeDtypeStruct((rows, n - n_qkv_tiles * tn), F32),
        ],
        compiler_params=_params("parallel", "arbitrary"),
        name="mix_in",
    )(x, w3, cosf, sins, qg3, kg3)


def _flash_kernel(q_ref, *rest, tq, kv_lens, tkv, n_cast):
    n_src = len(kv_lens)
    k_refs, v_refs = rest[:n_src], rest[n_src:2 * n_src]
    o_ref = rest[2 * n_src + n_cast]
    _run_cast_jobs(rest[2 * n_src:2 * n_src + n_cast], rest[2 * n_src + n_cast + 1:])
    q = jnp.concatenate([q_ref[:, h * HEAD_DIM:(h + 1) * HEAD_DIM] for h in range(GQA_GROUP)], axis=0)
    nrow = GQA_GROUP * tq

    def step(k, v, carry):
        m, acc = carry
        s = lax.dot_general(q, k, (((1,), (1,)), ((), ())), preferred_element_type=F32)
        m_new = jnp.maximum(m, jnp.max(s, axis=-1, keepdims=True))
        alpha = jnp.exp2(m - m_new)
        p = jnp.exp2(s - m_new).astype(BF16)
        v_ones = jnp.concatenate([v, jnp.ones_like(v)], axis=1)
        acc = alpha * acc + jnp.dot(p, v_ones, preferred_element_type=F32)
        return m_new, acc

    carry = (jnp.full((nrow, 1), -jnp.inf, F32), jnp.zeros((nrow, 2 * HEAD_DIM), F32))
    for k_ref, v_ref, length in zip(k_refs, v_refs, kv_lens):
        t = min(tkv, length)
        for c in range(length // t):
            carry = step(k_ref[c * t:(c + 1) * t, :], v_ref[c * t:(c + 1) * t, :], carry)
    _, acc = carry
    out = acc[:, :HEAD_DIM] / acc[:, HEAD_DIM:]
    for h in range(GQA_GROUP):
        o_ref[:, h * HEAD_DIM:(h + 1) * HEAD_DIM] = out[h * tq:(h + 1) * tq].astype(o_ref.dtype)


def _flash(qkv, *, n_q, n_kv, n_batch, q_row0, q_len, kv_srcs, tq, casts=()):
    qw = GQA_GROUP * HEAD_DIM
    nq_t = q_len // tq
    assert q_row0 % tq == 0
    in_specs = [pl.BlockSpec((tq, qw), lambda b, g, i: (q_row0 // tq + b * nq_t + i, g))]
    args = [qkv]
    for col0 in (n_q, n_q + n_kv):
        for (row0, length) in kv_srcs:
            assert row0 % length == 0
            in_specs.append(pl.BlockSpec(
                (length, HEAD_DIM),
                lambda b, g, i, row0=row0, length=length, col0=col0: (row0 // length + b, col0 + g)))
            args.append(qkv)
    c_in, c_args, c_out, c_shapes = _cast_jobs(casts, n_batch * n_kv * nq_t,
                                               lambda b, g, i: (b * n_kv + g) * nq_t + i)
    outs = pl.pallas_call(
        functools.partial(_flash_kernel, tq=tq, kv_lens=tuple(l for _, l in kv_srcs), tkv=256,
                          n_cast=len(casts)),
        grid=(n_batch, n_kv, nq_t),
        in_specs=in_specs + c_in,
        out_specs=[pl.BlockSpec((tq, qw), lambda b, g, i: (b * nq_t + i, g))] + c_out,
        out_shape=[jax.ShapeDtypeStruct((n_batch * q_len, n_q * HEAD_DIM), BF16)] + c_shapes,
        compiler_params=_params("arbitrary", "arbitrary", "arbitrary"),
        name="gqa_attention",
    )(*args, *c_args)
    return outs[0], tuple(outs[1:])


def _ssm_taps_kernel(z_ref, c_ref, o_ref):
    o_ref[...] = lax.dot_general(z_ref[...], c_ref[...], (((1,), (1,)), ((), ())),
                                 precision=lax.Precision.HIGHEST, preferred_element_type=F32)


def _ssm_taps(z, c):
    nj, nd, rows, p2 = z.shape
    return pl.pallas_call(
        _ssm_taps_kernel,
        grid=(nj, nd),
        in_specs=[
            pl.BlockSpec((None, None, rows, p2), lambda j, d: (j, d, 0, 0)),
            pl.BlockSpec((None, None, LANES, p2), lambda j, d: (j, d, 0, 0)),
        ],
        out_specs=pl.BlockSpec((None, None, rows, LANES), lambda j, d: (j, d, 0, 0)),
        out_shape=jax.ShapeDtypeStruct((nj, nd, rows, LANES), F32),
        compiler_params=_params("parallel", "parallel"),
        name="ssm_taps",
    )(z, c)


def _ssm_scan_kernel(s_ref, p_ref, o_ref, *, n_batch, lat_tiles, ctx_tiles, half):
    row = lax.broadcasted_iota(jnp.int32, (8, half), 0)
    lat_rows = n_batch * lat_tiles * 8

    def run(d, col0, reverse):
        pr, pi = p_ref[2 * d], p_ref[2 * d + 1]
        ent_r, ent_i = pr[0:8], pi[0:8]
        steps = [(k, pr[8 + n:9 + n], pi[8 + n:9 + n]) for n, k in enumerate((1, 2, 4))]
        a8_r, a8_i = pr[11:12], pi[11:12]

        def shift(x, k):
            if reverse:
                return jnp.where(row < 8 - k, pltpu.roll(x, 8 - k, axis=0), 0.0)
            return jnp.where(row >= k, pltpu.roll(x, k, axis=0), 0.0)

        def tile(r0, h):
            hr, hi = h
            ir = s_ref[pl.ds(r0, 8), col0:col0 + half]
            ii = s_ref[pl.ds(r0, 8), col0 + half:col0 + 2 * half]
            for k, ar, ai in steps:
                tr, ti = shift(ir, k), shift(ii, k)
                ir, ii = ir + ar * tr - ai * ti, ii + ar * ti + ai * tr
            o_ref[pl.ds(r0, 8), col0:col0 + half] = ent_r * hr - ent_i * hi + shift(ir, 1)
            o_ref[pl.ds(r0, 8), col0 + half:col0 + 2 * half] = ent_r * hi + ent_i * hr + shift(ii, 1)
            nr = a8_r * hr - a8_i * hi + ir
            ni = a8_r * hi + a8_i * hr + ii
            last = 0 if reverse else 7
            return (jnp.broadcast_to(nr[last:last + 1], (8, half)), jnp.broadcast_to(ni[last:last + 1], (8, half)))

        def phase(base, n_tiles, hs):
            def body(i, hs):
                t = (n_tiles - 1 - i) if reverse else i
                return tuple(tile(pl.multiple_of(base + (b * n_tiles + t) * 8, 8), hs[b]) for b in range(n_batch))
            return lax.fori_loop(0, n_tiles, body, hs)

        zero = jnp.zeros((8, half), F32)
        hs = phase(lat_rows, ctx_tiles, tuple((zero, zero) for _ in range(n_batch)))
        phase(0, lat_tiles, hs)

    run(0, 0, False)
    run(1, 2 * half, True)


def _ssm_scan(merged, pw, layer, *, n_batch, lat_tiles, ctx_tiles):
    nj, m, _ = merged.shape
    half = pw.shape[4]
    w = 4 * half
    assert merged.shape[2] == 2 * w and m == n_batch * (lat_tiles + ctx_tiles) * 8
    return pl.pallas_call(
        functools.partial(_ssm_scan_kernel, n_batch=n_batch, lat_tiles=lat_tiles, ctx_tiles=ctx_tiles, half=half),
        grid=(nj,),
        in_specs=[
            pl.BlockSpec((None, m, w), lambda j: (j, 0, 1)),
            pl.BlockSpec((None, None, 4, 16, half), lambda j: (layer, j, 0, 0, 0)),
        ],
        out_specs=pl.BlockSpec((None, m, w), lambda j: (j, 0, 0)),
        out_shape=jax.ShapeDtypeStruct((nj, m, w), F32),
        compiler_params=_params("parallel"),
        name="ssm_scan",
    )(merged, pw)


def _ssm_weights(lam_re, lam_im, log_dt, b_re, b_im, c_re, c_im):
    T, H, P = SSM_CHUNK, SSM_GROUP, SSM_STATE
    G = lam_re.shape[1]
    GL = GROUPS_PER_LANE_BLOCK
    J = G // GL
    lam_re, lam_im = lam_re.astype(F32), lam_im.astype(F32)
    dt = jnp.exp(log_dt.astype(F32))[..., None]
    mag = jnp.exp(lam_re * dt)
    ab_r, ab_i = mag * jnp.cos(lam_im * dt), mag * jnp.sin(lam_im * dt)
    nr, ni = ab_r - 1.0, ab_i
    den = lam_re * lam_re + lam_im * lam_im
    cr = (nr * lam_re + ni * lam_im) / den
    ci = (ni * lam_re - nr * lam_im) / den
    b_re, b_im = b_re.astype(F32), b_im.astype(F32)
    bb_r = cr[..., None] * b_re - ci[..., None] * b_im
    bb_i = cr[..., None] * b_im + ci[..., None] * b_re
    c_re, c_im = c_re.astype(F32), c_im.astype(F32)

    def powers(d, exps):
        e = jnp.asarray(exps, F32)[:, None, None]
        pmag = jnp.exp(e * (lam_re[d] * dt[d])[None])
        pang = e * (lam_im[d] * dt[d])[None]
        return pmag * jnp.cos(pang), pmag * jnp.sin(pang)

    def drive(d, exps):
        pr, pi = powers(d, exps)
        return (pr[..., None] * bb_r[d][None] - pi[..., None] * bb_i[d][None],
                pr[..., None] * bb_i[d][None] + pi[..., None] * bb_r[d][None])

    asc = list(range(T))
    desc = [T - 1 - s for s in range(T)]

    def tap_lhs(d, exps):
        zr, zi = drive(d, exps)
        z = jnp.concatenate([zr, zi], axis=2).reshape(T, J, GL, 2 * P, H)
        return z.transpose(1, 0, 2, 4, 3).reshape(J, T * LANES, 2 * P)
    z_all = jnp.stack([tap_lhs(0, asc), tap_lhs(1, desc)], axis=1)
    c_all = jnp.concatenate([c_re, -c_im], axis=-1).reshape(2, J, LANES, 2 * P).transpose(1, 0, 2, 3)

    def summary(d, exps):
        pr, pi = powers(d, exps)
        pr, pi = (a.reshape(T, J, 1, GL * P).transpose(1, 0, 2, 3) for a in (pr, pi))
        br, bi = (a.reshape(J, GL, P, H).transpose(0, 3, 1, 2).reshape(J, 1, H, GL * P)
                  for a in (bb_r[d], bb_i[d]))
        return jnp.concatenate([pr * br - pi * bi, pr * bi + pi * br], axis=-1).reshape(J, T * H, 2 * GL * P)
    z_sum = jnp.concatenate([summary(0, desc), summary(1, asc)], axis=-1)

    def readout(d, exps):
        pr, pi = powers(d, exps)
        pr, pi = (jnp.repeat(a.reshape(T, J, GL, P).transpose(1, 3, 0, 2), H, axis=-1)
                  for a in (pr, pi))
        cr_l, ci_l = (a.reshape(J, GL, H, P).transpose(0, 3, 1, 2).reshape(J, P, 1, LANES)
                      for a in (c_re[d], c_im[d]))
        vr = (cr_l * pr - ci_l * pi).reshape(J, 1, P, T * LANES)
        vi = (cr_l * pi + ci_l * pr).reshape(J, 1, P, T * LANES)
        return jnp.concatenate([vr, -vi], axis=1)
    v_out = jnp.concatenate([readout(0, [t + 1 for t in range(T)]), readout(1, [T - t for t in range(T)])],
                            axis=1)

    def scan_powers(d, entry):
        exps = [T * k for k in entry] + [T, 2 * T, 4 * T, 8 * T] + [0] * 4
        return jnp.stack([a.reshape(16, J, GL * P).transpose(1, 0, 2) for a in powers(d, exps)], axis=1)
    pw = jnp.concatenate([scan_powers(0, list(range(8))), scan_powers(1, [7 - r for r in range(8)])], axis=1)
    return z_all, c_all, z_sum, v_out, pw


def _ssm_tap_tiles(raw):
    T, H = SSM_CHUNK, SSM_GROUP
    raw = raw.reshape(raw.shape[0], 2, T, LANES, LANES)
    fwd, bwd = raw[:, 0], raw[:, 1]
    tiles = jnp.concatenate([bwd[:, :T - 1], bwd[:, T - 1:] + fwd[:, :1], fwd[:, 1:]], axis=1)
    r = lax.broadcasted_iota(jnp.int32, (LANES, LANES), 0) // H
    c = lax.broadcasted_iota(jnp.int32, (LANES, LANES), 1) // H
    return jnp.where(r == c, tiles, 0.0).astype(BF16)


def _ssm_weights_all(*params):
    z_all, c_all, z_sum, v_out, pw = jax.vmap(_ssm_weights)(*params)
    n_layers, nj = z_all.shape[:2]
    raw = _ssm_taps(z_all.reshape((n_layers * nj,) + z_all.shape[2:]), c_all.reshape((n_layers * nj,) + c_all.shape[2:]))
    tiles = jax.vmap(_ssm_tap_tiles)(raw.reshape((n_layers, nj) + raw.shape[1:]))
    return tiles, z_sum, v_out, pw


def _s5_mixer(u, d4, layer, n_batch, seq, n_ctx, weights):
    tiles, z_sum, v_out, pw = weights
    T = SSM_CHUNK
    merged = _ssm_chunk_ops(u, tiles, z_sum, layer)
    h_in = _ssm_scan(merged, pw, layer, n_batch=n_batch, lat_tiles=seq // (8 * T), ctx_tiles=n_ctx // (8 * T))
    return _ssm_readout(h_in, v_out, merged, u, d4, layer)


def _rope_tables(seq, n_batch, n_ctx_rows):
    pos = jnp.arange(seq)
    rows = (pos // GRID_W).astype(F32)
    cols = (pos % GRID_W).astype(F32)
    axis_dim = HEAD_DIM // 2
    inv_freq = ROPE_THETA ** (-jnp.arange(0, axis_dim, 2, dtype=F32) / axis_dim)
    ang_r, ang_c = rows[:, None] * inv_freq, cols[:, None] * inv_freq
    cosf = jnp.concatenate([jnp.cos(ang_r)] * 2 + [jnp.cos(ang_c)] * 2, axis=1)
    sins = jnp.concatenate([-jnp.sin(ang_r), jnp.sin(ang_r), -jnp.sin(ang_c), jnp.sin(ang_c)], axis=1)
    cosf = jnp.concatenate([jnp.tile(cosf, (n_batch, 1)), jnp.ones((n_ctx_rows, HEAD_DIM), F32)], axis=0)
    sins = jnp.concatenate([jnp.tile(sins, (n_batch, 1)), jnp.zeros((n_ctx_rows, HEAD_DIM), F32)], axis=0)
    return cosf, sins


def kernel(x, c, ctx, c_ctx, w_ada, b_ada, norm_ffn1, ffn1_w_in, ffn1_w_out, norm_mix, w_mix_in, q_norm, k_norm, ssm_lam_re, ssm_lam_im, ssm_log_dt, ssm_b_re, ssm_b_im, ssm_c_re, ssm_c_im, ssm_d, w_glu, b_glu, w_mix_out, norm_ffn2, ffn2_w_in, ffn2_w_out, norm_final):
    n_batch, seq, d = x.shape
    n_ctx = ctx.shape[1]
    depth = w_ada.shape[0]
    ssm_width = w_glu.shape[1]
    attn_width = w_mix_out.shape[1] - ssm_width
    n_q = attn_width // HEAD_DIM
    n_kv = n_q // GQA_GROUP
    n_lat = n_batch * seq
    n_ctx_rows = n_batch * n_ctx
    n_all = n_lat + n_ctx_rows
    tm = n_ctx_rows
    tr = _tile(tm, 512, 8)
    assert seq % tm == 0 and seq % (8 * SSM_CHUNK) == 0 and n_ctx % (8 * SSM_CHUNK) == 0 and n_batch < 8

    h, h_tail = x.reshape(n_lat, d), ctx.reshape(n_ctx_rows, d)
    cc = jnp.concatenate([c, c_ctx[None], jnp.zeros((8 - n_batch - 1, d), F32)], axis=0)
    cosf, sins = _rope_tables(seq, n_batch, n_ctx_rows)

    def row3(a):
        return a.reshape(a.shape[0], 1, a.shape[1])

    b_ada3 = row3(b_ada)
    g_ffn1, g_mix, g_ffn2 = row3(norm_ffn1), row3(norm_mix), row3(norm_ffn2)
    qg3, kg3, bglu3 = row3(q_norm), row3(k_norm), row3(b_glu)
    d4 = ssm_d.astype(F32).reshape(depth, ssm_width // LANES, 1, LANES)
    def ffn(h, mod4, k, gain3, layer, rows, w_in, w_out_f32, casts, h_tail=None):
        if h_tail is None:
            hn = _norm(h, gain3, layer, rows=rows, tr=tr, out_dtype=BF16, mod4=mod4, k=k, seq=seq, n_batch=n_batch)
        else:
            hn, h = _norm(h, gain3, layer, rows=rows, tr=tr // 2, out_dtype=BF16, mod4=mod4, k=k, seq=seq,
                          n_batch=n_batch, x_tail=h_tail)
        hid, cast = _mm_swiglu(hn, w_in, 0, rows=rows, tm=tm, tn=_tile(w_in.shape[2] // 2, 512),
                               casts=[(w_out_f32, layer)] + casts)
        h = _mm_resid(hid, cast[0], 0, h, mod4, k, 0.5, rows=rows, tm=tm, tn=_tile(d, 1024),
                      tk=_tile(w_out_f32.shape[1], 4096), seq=seq, n_batch=n_batch)
        return h, cast[1:]

    weights = _ssm_weights_all(ssm_lam_re, ssm_lam_im, ssm_log_dt, ssm_b_re, ssm_b_im, ssm_c_re, ssm_c_im)

    ffn1_in = ffn1_w_in[:1].astype(BF16)
    for layer in range(depth):
        last = layer == depth - 1
        mod4 = _mod_table(cc, w_ada, b_ada3, layer).reshape(8, 9, 1, d)

        h, (mix_in, glu, mix_out) = ffn(h, mod4, 0, g_ffn1, layer, n_all, ffn1_in, ffn1_w_out,
                                        [(w_mix_in, layer), (w_glu, layer), (w_mix_out, layer)], h_tail=h_tail)
        h_tail = None

        hn = _norm(h, g_mix, layer, rows=n_all, tr=tr, out_dtype=BF16, mod4=mod4, k=1, seq=seq, n_batch=n_batch)
        qkv, u = _mix_in(hn, mix_in, 0, cosf, sins, qg3, kg3, layer, rows=n_all, tm=tm, n_q=n_q, n_kv=n_kv)
        attn, (ffn2_in,) = _flash(qkv, n_q=n_q, n_kv=n_kv, n_batch=n_batch, q_row0=0, q_len=seq,
                                  kv_srcs=[(0, seq), (n_lat, n_ctx)], tq=min(512, seq), casts=[(ffn2_w_in, layer)])
        out_rows = n_lat if last else n_all
        attn_c = None
        if not last:
            attn_c, _ = _flash(qkv, n_q=n_q, n_kv=n_kv, n_batch=n_batch, q_row0=n_lat, q_len=n_ctx,
                               kv_srcs=[(n_lat, n_ctx)], tq=min(128, n_ctx))

        yg = _s5_mixer(u, d4, layer, n_batch, seq, n_ctx, weights)
        ssm_out = _mm_glu(yg, glu, bglu3, 0, layer, rows=out_rows, tm=tm, tn=_tile(ssm_width, 1024))

        h = _mm_mix_out(attn, attn_c, ssm_out, mix_out, 0, h, mod4, 1, rows=out_rows, tm=tm,
                        tn=_tile(d, 512), seq=seq, n_batch=n_batch)
        h, next_in = ffn(h, mod4, 2, g_ffn2, layer, out_rows, ffn2_in, ffn2_w_out,
                         [] if last else [(ffn1_w_in, layer + 1)])
        if not last:
            ffn1_in = next_in[0]

    out = _norm(h, row3(norm_final[None]), 0, rows=n_lat, tr=tr, out_dtype=F32)
    return out.reshape(n_batch, seq, d)
```
